```python
import functools
import jax, jax.numpy as jnp
from jax import lax
import numpy as np

D_MODEL = 2048
BATCH = 4
SEQ = 4096
DEPTH = 1
DEC_BATCH = 32
DEC_SEQ = 64
PAST_LEN = 1024

CHUNK = 64
H_RET = 8
DK_RET = 128
DV_RET = 256
H_FOX = 16
DH_FOX = 128
Q_BLOCK = 128
N_EXPERTS = 64
TOP_K = 6
D_EXPERT = 1408
D_SHARED = 1408
ROUTE_SCALE = 2.5
ROW_BLOCK = 128
ROPE_BASE = 10000.0
EPS = 1e-6

QR_W = H_RET * DK_RET
VR_W = H_RET * DV_RET
QF_W = H_FOX * DH_FOX
SPLITS = (QR_W, QR_W, VR_W, VR_W, QF_W, QF_W, QF_W, H_FOX, D_MODEL, D_MODEL)
D_IN = sum(SPLITS)

kernel_name = "hybrid_retention_fox_moe_stream_step"


def rms_norm(x, g):
    x32 = x.astype(jnp.float32)
    y = x32 * lax.rsqrt(jnp.mean(x32 * x32, axis=-1, keepdims=True) + EPS)
    return y.astype(x.dtype) * g


def adaln(c, w_ada, b_ada):
    m = jax.nn.silu(c) @ w_ada + b_ada
    return jnp.split(m[:, None, :], 6, axis=-1)


def rope(x, pos):
    half = x.shape[-1] // 2
    inv = 1.0 / (ROPE_BASE ** (jnp.arange(half, dtype=jnp.float32) * 2.0 / x.shape[-1]))
    ang = pos.astype(jnp.float32)[:, None] * inv[None, :]
    cos = jnp.cos(ang)[None, :, None, :]
    sin = jnp.sin(ang)[None, :, None, :]
    x32 = x.astype(jnp.float32)
    x1, x2 = x32[..., :half], x32[..., half:]
    return jnp.concatenate([x1 * cos - x2 * sin, x1 * sin + x2 * cos], axis=-1).astype(x.dtype)


def project(hn, w_in, b_forget):
    B, L, _ = hn.shape
    z = hn @ w_in
    parts, off = [], 0
    for w in SPLITS:
        parts.append(z[..., off:off + w])
        off += w
    q_r, k_r, v_r, g_r, q_f, k_f, v_f, f_l, gm_r, gm_f = parts
    hd = lambda t, h: t.reshape(B, L, h, -1)
    logf = jax.nn.log_sigmoid((f_l + b_forget).astype(jnp.float32))
    return (hd(q_r, H_RET), hd(k_r, H_RET), hd(v_r, H_RET), hd(g_r, H_RET),
            hd(q_f, H_FOX), hd(k_f, H_FOX), hd(v_f, H_FOX), logf, gm_r, gm_f)


def log_gamma():
    return jnp.log1p(-jnp.exp2(-5.0 - jnp.arange(H_RET, dtype=jnp.float32)))


def retention_chunk(S, q, k, v):
    L = q.shape[1]
    lg = log_gamma()
    i = jnp.arange(L, dtype=jnp.float32)
    intra = jnp.exp(jnp.abs(i[:, None] - i[None, :])[None] * lg[:, None, None])
    s = jnp.einsum('bihd,bjhd->bhij', q, k).astype(jnp.float32) * intra
    o = jnp.einsum('bhij,bjhe->bihe', s, v.astype(jnp.float32))
    q_dec = jnp.exp((i[:, None] + 1.0) * lg[None, :])
    o = o + jnp.einsum('bihd,bhde->bihe', q.astype(jnp.float32), S.astype(jnp.float32)) * q_dec[None, :, :, None]
    k_dec = jnp.exp((L - 1.0 - i)[:, None] * lg[None, :])
    S_new = (jnp.exp(L * lg)[None, :, None, None] * S.astype(jnp.float32)
             + jnp.einsum('bjhd,bjhe->bhde', k.astype(jnp.float32) * k_dec[None, :, :, None], v.astype(jnp.float32)))
    return S_new, o


def retention_prompt(q, k, v):
    B, S_, H, _ = q.shape
    nc = S_ // CHUNK
    to_chunks = lambda t: t.reshape(B, nc, CHUNK, H, t.shape[-1]).swapaxes(0, 1)
    S0 = jnp.zeros((B, H, DK_RET, DV_RET), jnp.float32)
    step = lambda S, qkv: retention_chunk(S, *qkv)
    S_fin, o = lax.scan(step, S0, (to_chunks(q), to_chunks(k), to_chunks(v)))
    return o.swapaxes(0, 1).reshape(B, S_, H, DV_RET), S_fin


def fox_attend(q, k, v, Fq, Fk, q_pos, k_pos):
    s = jnp.einsum('bqhd,bkhd->bhqk', q, k).astype(jnp.float32) * (DH_FOX ** -0.5)
    s = s + Fq[..., :, None] - Fk[..., None, :]
    s = jnp.where(k_pos[None, :] <= q_pos[:, None], s, -jnp.inf)
    p = jax.nn.softmax(s, axis=-1)
    return jnp.einsum('bhqk,bkhd->bqhd', p, v.astype(jnp.float32)).astype(v.dtype)


def fox_prompt(q, k, v, logf):
    B, S_, H, D = q.shape
    F = jnp.cumsum(logf, axis=1).swapaxes(1, 2)
    pos = jnp.arange(S_)

    def blk(i):
        qs = i * Q_BLOCK
        qb = lax.dynamic_slice_in_dim(q, qs, Q_BLOCK, axis=1)
        Fq = lax.dynamic_slice_in_dim(F, qs, Q_BLOCK, axis=2)
        return fox_attend(qb, k, v, Fq, F, qs + jnp.arange(Q_BLOCK), pos)

    o = lax.map(blk, jnp.arange(S_ // Q_BLOCK))
    return o.swapaxes(0, 1).reshape(B, S_, H, D)


def fox_sample(q, k, v, logf, cache_k, cache_v, cache_logf):
    P, L = cache_k.shape[1], q.shape[1]
    kk = jnp.concatenate([cache_k.astype(k.dtype), k], axis=1)
    vv = jnp.concatenate([cache_v.astype(v.dtype), v], axis=1)
    F = jnp.cumsum(jnp.concatenate([cache_logf.astype(jnp.float32), logf], axis=1), axis=1).swapaxes(1, 2)
    return fox_attend(q, kk, vv, F[:, :, P:], F, P + jnp.arange(L), jnp.arange(P + L))


def mixer_prompt(q_r, k_r, v_r, q_f, k_f, v_f, logf):
    pos = jnp.arange(q_r.shape[1])
    q_r = rope(q_r, pos)
    k_r = rope(k_r, pos) * (DK_RET ** -0.5)
    o_r, S_fin = retention_prompt(q_r, k_r, v_r)
    o_f = fox_prompt(q_f, k_f, v_f, logf)
    return o_r, o_f, (k_f, v_f, logf, S_fin)


def mixer_sample(q_r, k_r, v_r, q_f, k_f, v_f, logf, cache_k, cache_v, cache_logf, state):
    pos = cache_k.shape[1] + jnp.arange(q_r.shape[1])
    q_r = rope(q_r, pos)
    k_r = rope(k_r, pos) * (DK_RET ** -0.5)
    S_new, o_r = retention_chunk(state, q_r, k_r, v_r)
    o_f = fox_sample(q_f, k_f, v_f, logf, cache_k, cache_v, cache_logf)
    return o_r, o_f, (k_f, v_f, logf, S_new)


def head_norm(o, g):
    mu = jnp.mean(o, axis=-1, keepdims=True)
    var = jnp.mean(jnp.square(o - mu), axis=-1, keepdims=True)
    return (o - mu) * lax.rsqrt(var + EPS) * g.reshape(H_RET, DV_RET).astype(jnp.float32)


def swiglu(x, wg, wu, wd):
    return (jax.nn.silu(x @ wg) * (x @ wu)) @ wd


def routed_moe(x, w_router, b_router, w_eg, w_eu, w_ed):
    T, D = x.shape
    scores = jax.nn.sigmoid(x.astype(jnp.float32) @ w_router.astype(jnp.float32))
    _, idx = lax.top_k(scores + b_router.astype(jnp.float32), TOP_K)
    w = jnp.take_along_axis(scores, idx, axis=-1)
    w = w / jnp.sum(w, axis=-1, keepdims=True) * ROUTE_SCALE
    A = T * TOP_K
    flat_e = idx.reshape(-1)
    order = jnp.argsort(flat_e)
    e_sorted = flat_e[order]
    tok = order // TOP_K
    counts = jnp.bincount(flat_e, length=N_EXPERTS)
    padded = (counts + ROW_BLOCK - 1) // ROW_BLOCK * ROW_BLOCK
    pad_end = jnp.cumsum(padded)
    pad_start = pad_end - padded
    start = jnp.cumsum(counts) - counts
    dest = pad_start[e_sorted] + jnp.arange(A) - start[e_sorted]
    n_blocks = (A + N_EXPERTS * (ROW_BLOCK - 1) + ROW_BLOCK - 1) // ROW_BLOCK
    rows = n_blocks * ROW_BLOCK
    buf = jnp.zeros((rows, D), x.dtype).at[dest].set(x[tok])
    block_e = jnp.minimum(jnp.searchsorted(pad_end, jnp.arange(n_blocks) * ROW_BLOCK, side='right'), N_EXPERTS - 1)

    def expert_block(args):
        xb, e = args
        return swiglu(xb, w_eg[e], w_eu[e], w_ed[e])

    out = lax.map(expert_block, (buf.reshape(n_blocks, ROW_BLOCK, D), block_e)).reshape(rows, D)
    w_sorted = w.reshape(-1)[order].astype(out.dtype)
    return jax.ops.segment_sum(out[dest] * w_sorted[:, None], tok, num_segments=T)


def trunk_layer(h, mod, mixer, lw):
    (w_in, b_forget, g_norm1, g_ret_gn, w_ret_out, w_fox_out, w_out, g_norm2,
     w_router, b_router, w_eg, w_eu, w_ed, w_sg, w_su, w_sd) = lw
    sh1, sc1, gt1, sh2, sc2, gt2 = mod
    B, L, D = h.shape
    hn = rms_norm(h, g_norm1) * (1.0 + sc1) + sh1
    q_r, k_r, v_r, g_r, q_f, k_f, v_f, logf, gm_r, gm_f = project(hn, w_in, b_forget)
    o_r, o_f, new_state = mixer(q_r, k_r, v_r, q_f, k_f, v_f, logf)
    o_r = (head_norm(o_r, g_ret_gn).astype(hn.dtype) * jax.nn.silu(g_r)).reshape(B, L, VR_W)
    a = o_r @ w_ret_out
    b = o_f.reshape(B, L, QF_W) @ w_fox_out
    mix = (jax.nn.sigmoid(gm_r) * a + jax.nn.sigmoid(gm_f) * b) @ w_out
    h = h + gt1 * mix
    hn2 = (rms_norm(h, g_norm2) * (1.0 + sc2) + sh2).reshape(B * L, D)
    ffn = swiglu(hn2, w_sg, w_su, w_sd) + routed_moe(hn2, w_router, b_router, w_eg, w_eu, w_ed)
    h = h + gt2 * ffn.reshape(B, L, D)
    return h, new_state


def setup_inputs(seed: int = 0) -> dict:
    key = jax.random.key(seed)
    ks = jax.random.split(key, 32)
    nrm = lambda k, shape, scale: jax.random.normal(k, shape, jnp.float32) * scale
    D = D_MODEL
    return {
        "x_prompt": nrm(ks[0], (BATCH, SEQ, D), 1.0),
        "x_sample": nrm(ks[1], (DEC_BATCH, DEC_SEQ, D), 1.0),
        "c_prompt": nrm(ks[2], (BATCH, D), 1.0),
        "c_sample": nrm(ks[3], (DEC_BATCH, D), 1.0),
        "cache_fox_k": nrm(ks[4], (DEPTH, DEC_BATCH, PAST_LEN, H_FOX, DH_FOX), 1.0),
        "cache_fox_v": nrm(ks[5], (DEPTH, DEC_BATCH, PAST_LEN, H_FOX, DH_FOX), 1.0),
        "cache_fox_logf": jax.nn.log_sigmoid(2.0 + nrm(ks[6], (DEPTH, DEC_BATCH, PAST_LEN, H_FOX), 1.0)),
        "state_ret": nrm(ks[7], (DEPTH, DEC_BATCH, H_RET, DK_RET, DV_RET), 1.0),
        "w_ada": nrm(ks[8], (DEPTH, D, 6 * D), 0.5 * D ** -0.5),
        "b_ada": nrm(ks[9], (DEPTH, 6 * D), 0.01),
        "g_norm1": 1.0 + nrm(ks[10], (DEPTH, D), 0.05),
        "w_in": nrm(ks[11], (DEPTH, D, D_IN), D ** -0.5),
        "b_forget": 2.0 + nrm(ks[12], (DEPTH, H_FOX), 0.1),
        "g_ret_gn": 1.0 + nrm(ks[13], (DEPTH, VR_W), 0.05),
        "w_ret_out": nrm(ks[14], (DEPTH, VR_W, D), VR_W ** -0.5),
        "w_fox_out": nrm(ks[15], (DEPTH, QF_W, D), QF_W ** -0.5),
        "w_out": nrm(ks[16], (DEPTH, D, D), D ** -0.5),
        "g_norm2": 1.0 + nrm(ks[17], (DEPTH, D), 0.05),
        "w_router": nrm(ks[18], (DEPTH, D, N_EXPERTS), D ** -0.5),
        "b_router": nrm(ks[19], (DEPTH, N_EXPERTS), 0.01),
        "w_exp_gate": nrm(ks[20], (DEPTH, N_EXPERTS, D, D_EXPERT), D ** -0.5),
        "w_exp_up": nrm(ks[21], (DEPTH, N_EXPERTS, D, D_EXPERT), D ** -0.5),
        "w_exp_down": nrm(ks[22], (DEPTH, N_EXPERTS, D_EXPERT, D), D_EXPERT ** -0.5),
        "w_sh_gate": nrm(ks[23], (DEPTH, D, D_SHARED), D ** -0.5),
        "w_sh_up": nrm(ks[24], (DEPTH, D, D_SHARED), D ** -0.5),
        "w_sh_down": nrm(ks[25], (DEPTH, D_SHARED, D), D_SHARED ** -0.5),
        "g_final": 1.0 + nrm(ks[26], (D,), 0.05),
    }


def reference(x_prompt, x_sample, c_prompt, c_sample, cache_fox_k, cache_fox_v, cache_fox_logf, state_ret,
              w_ada, b_ada, g_norm1, w_in, b_forget, g_ret_gn, w_ret_out, w_fox_out, w_out, g_norm2,
              w_router, b_router, w_exp_gate, w_exp_up, w_exp_down, w_sh_gate, w_sh_up, w_sh_down, g_final):
    h_p, h_s = x_prompt, x_sample
    st_p, st_s = [], []
    for l in range(DEPTH):
        lw = (w_in[l], b_forget[l], g_norm1[l], g_ret_gn[l], w_ret_out[l], w_fox_out[l], w_out[l], g_norm2[l],
              w_router[l], b_router[l], w_exp_gate[l], w_exp_up[l], w_exp_down[l],
              w_sh_gate[l], w_sh_up[l], w_sh_down[l])
        h_p, sp = trunk_layer(h_p, adaln(c_prompt, w_ada[l], b_ada[l]), mixer_prompt, lw)
        mix_s = functools.partial(mixer_sample, cache_k=cache_fox_k[l], cache_v=cache_fox_v[l],
                                  cache_logf=cache_fox_logf[l], state=state_ret[l])
        h_s, ss = trunk_layer(h_s, adaln(c_sample, w_ada[l], b_ada[l]), mix_s, lw)
        st_p.append(sp)
        st_s.append(ss)
    y_prompt = rms_norm(h_p, g_final)
    y_sample = rms_norm(h_s, g_final)
    fox_k_prompt = jnp.stack([s[0] for s in st_p])
    fox_v_prompt = jnp.stack([s[1] for s in st_p])
    fox_logf_prompt = jnp.stack([s[2] for s in st_p])
    ret_state_prompt = jnp.stack([s[3] for s in st_p])
    fox_k_sample = jnp.stack([s[0] for s in st_s])
    fox_v_sample = jnp.stack([s[1] for s in st_s])
    fox_logf_sample = jnp.stack([s[2] for s in st_s])
    ret_state_sample = jnp.stack([s[3] for s in st_s])
    return (y_prompt, y_sample, fox_k_prompt, fox_v_prompt, fox_logf_prompt, ret_state_prompt,
            fox_k_sample, fox_v_sample, fox_logf_sample, ret_state_sample)
```

```python
import functools
import math

import jax
import jax.numpy as jnp
from jax import lax
from jax.experimental import pallas as pl
from jax.experimental.pallas import tpu as pltpu

F32 = jnp.float32
BF16 = jnp.bfloat16
I32 = jnp.int32

CHUNK = 64
H_RET = 8
DK_RET = 128
DV_RET = 256
H_FOX = 16
DH_FOX = 128
TOP_K = 6
ROUTE_SCALE = 2.5
ROPE_BASE = 10000.0
EPS = 1e-6

LANES = 128
DEST_W = 8
EXPERT_ROWS = 512
F_TILE = 256
VMEM_LIMIT = 56 * 1024 * 1024


def _params(sem, vmem=None):
    return pltpu.CompilerParams(dimension_semantics=sem, vmem_limit_bytes=vmem)


def _sigmoid(x):
    return 1.0 / (1.0 + jnp.exp(-x))


def _nt_dot(a, b):
    return lax.dot_general(a, b, (((1,), (1,)), ((), ())), preferred_element_type=F32)


def _pack_halves(x):
    n = x.shape[-1] // 2
    hi = pltpu.bitcast(x[:, :n].astype(BF16).astype(F32), I32)
    lo = pltpu.bitcast(x[:, n:].astype(BF16).astype(F32), I32)
    return hi | lax.shift_right_logical(lo, jnp.int32(16))


def _unpack_halves(u):
    hi = pltpu.bitcast(u & jnp.int32(-65536), F32)
    lo = pltpu.bitcast(lax.shift_left(u, jnp.int32(16)), F32)
    return hi, lo


def _ada_kernel(c_ref, w_ref, b_ref, o_ref):
    c = c_ref[...]
    s = (c * _sigmoid(c)).astype(BF16)
    o_ref[...] = jnp.dot(s, w_ref[...].astype(BF16), preferred_element_type=F32) + b_ref[...]


def _adaln(c, w_ada, b_ada):
    nb, d = c.shape
    n = w_ada.shape[1]
    tn = 1024
    return pl.pallas_call(
        _ada_kernel,
        grid=(n // tn,),
        in_specs=[pl.BlockSpec((nb, d), lambda j: (0, 0)),
                  pl.BlockSpec((d, tn), lambda j: (0, j)),
                  pl.BlockSpec((1, tn), lambda j: (0, j))],
        out_specs=pl.BlockSpec((nb, tn), lambda j: (0, j)),
        out_shape=jax.ShapeDtypeStruct((nb, n), F32),
        compiler_params=_params(("arbitrary",), VMEM_LIMIT),
        name="adaln",
    )(c, w_ada, b_ada.reshape(1, n))


def _norm1_kernel(x_ref, g_ref, sh_ref, sc_ref, wf_ref, bf_ref, hn_ref, logf_ref):
    bb, tl, d = x_ref.shape
    x = x_ref[...]
    y = x * lax.rsqrt(jnp.mean(x * x, axis=-1, keepdims=True) + EPS) * g_ref[...]
    hn = (y * (1.0 + sc_ref[...]) + sh_ref[...]).astype(BF16)
    hn_ref[...] = hn
    z = jnp.dot(hn.reshape(bb * tl, d), wf_ref[...], preferred_element_type=F32) + bf_ref[...]
    logf = jnp.minimum(z, 0.0) - jnp.log(1.0 + jnp.exp(-jnp.abs(z)))
    logf_ref[...] = logf[:, :H_FOX].reshape(bb, tl, H_FOX)


def _norm1(x, g, mod, wf, bfg, bb, tl):
    nb, l, d = x.shape
    mod_spec = lambda k: pl.BlockSpec((bb, 1, d), lambda i, j: (i, 0, k))
    return pl.pallas_call(
        _norm1_kernel,
        grid=(nb // bb, l // tl),
        in_specs=[pl.BlockSpec((bb, tl, d), lambda i, j: (i, j, 0)),
                  pl.BlockSpec((1, 1, d), lambda i, j: (0, 0, 0)),
                  mod_spec(0), mod_spec(1),
                  pl.BlockSpec((d, LANES), lambda i, j: (0, 0)),
                  pl.BlockSpec((1, LANES), lambda i, j: (0, 0))],
        out_specs=[pl.BlockSpec((bb, tl, d), lambda i, j: (i, j, 0)),
                   pl.BlockSpec((bb, tl, H_FOX), lambda i, j: (i, j, 0))],
        out_shape=[jax.ShapeDtypeStruct((nb, l, d), BF16),
                   jax.ShapeDtypeStruct((nb, l, H_FOX), F32)],
        compiler_params=_params(("arbitrary", "arbitrary"), VMEM_LIMIT),
        name="norm1",
    )(x, g.reshape(1, 1, d), mod, mod, wf, bfg)


def _mm_kernel(a_ref, w_ref, o_ref):
    o_ref[...] = jnp.dot(a_ref[...], w_ref[...], preferred_element_type=F32).astype(o_ref.dtype)


def _matmul(a, w, out_dtype, tm, tn, name):
    m, k = a.shape
    n = w.shape[1]
    return pl.pallas_call(
        _mm_kernel,
        grid=(m // tm, n // tn),
        in_specs=[pl.BlockSpec((tm, k), lambda i, j: (i, 0)),
                  pl.BlockSpec((k, tn), lambda i, j: (0, j))],
        out_specs=pl.BlockSpec((tm, tn), lambda i, j: (i, j)),
        out_shape=jax.ShapeDtypeStruct((m, n), out_dtype),
        compiler_params=_params(("arbitrary", "arbitrary"), VMEM_LIMIT),
        name=name,
    )(a, w)


def _cumsum_kernel(x_ref, o_ref, carry_ref):
    tl = x_ref.shape[-1]

    @pl.when(pl.program_id(1) == 0)
    def _():
        carry_ref[...] = jnp.zeros_like(carry_ref)

    r = lax.broadcasted_iota(I32, (tl, tl), 0)
    c = lax.broadcasted_iota(I32, (tl, tl), 1)
    tri = (r <= c).astype(F32)
    y = jnp.dot(x_ref[0], tri, precision=lax.Precision.HIGHEST, preferred_element_type=F32)
    y = y + carry_ref[:, 0:1]
    o_ref[0] = y
    carry_ref[...] = jnp.broadcast_to(y[:, tl - 1:tl], carry_ref.shape)


def _cumsum_rows(x, tl):
    nb, h, s = x.shape
    return pl.pallas_call(
        _cumsum_kernel,
        grid=(nb, s // tl),
        in_specs=[pl.BlockSpec((1, h, tl), lambda b, t: (b, 0, t))],
        out_specs=pl.BlockSpec((1, h, tl), lambda b, t: (b, 0, t)),
        out_shape=jax.ShapeDtypeStruct((nb, h, s), F32),
        scratch_shapes=[pltpu.VMEM((h, LANES), F32)],
        compiler_params=_params(("arbitrary", "arbitrary"), VMEM_LIMIT),
        name="cumsum_logf",
    )(x)


def _ret_kernel(q_ref, k_ref, v_ref, g_ref, cos_ref, sin_ref, dmat_ref, qdec_ref, kdec_ref, sdec_ref,
                s0_ref, gn_ref, o_ref, sout_ref, s_scr):
    t = pl.program_id(2)

    @pl.when(t == 0)
    def _():
        s_scr[...] = s0_ref[0, 0]

    cos = cos_ref[...]
    sin = sin_ref[...]
    q = q_ref[0]
    k = k_ref[0]
    half = q.shape[-1] // 2
    qr = q * cos + pltpu.roll(q, half, 1) * sin
    kr = k * cos + pltpu.roll(k, half, 1) * sin
    vb = v_ref[0]
    state = s_scr[...]
    s = _nt_dot(qr.astype(BF16), kr.astype(BF16)) * dmat_ref[0]
    o = jnp.dot(s.astype(BF16), vb, preferred_element_type=F32)
    o = o + jnp.dot((qr * qdec_ref[0]).astype(BF16), state.astype(BF16), preferred_element_type=F32)
    kd_t = (kr * kdec_ref[0]).T.astype(BF16)
    s_scr[...] = state * sdec_ref[0] + jnp.dot(kd_t, vb, preferred_element_type=F32)

    @pl.when(t == pl.num_programs(2) - 1)
    def _():
        sout_ref[0, 0] = s_scr[...]

    mu = jnp.mean(o, axis=-1, keepdims=True)
    oc = o - mu
    var = jnp.mean(oc * oc, axis=-1, keepdims=True)
    on = oc * lax.rsqrt(var + EPS) * gn_ref[...]
    g = g_ref[0].astype(F32)
    o_ref[0] = (on * (g * _sigmoid(g))).astype(BF16)


def _retention(zqk, zb, s0, g_ret_gn, pos0, tb):
    nb, l, _ = zqk.shape
    nt = l // tb
    half = DK_RET // 2
    inv = 1.0 / (ROPE_BASE ** (jnp.arange(half, dtype=F32) * 2.0 / DK_RET))
    ang = (pos0 + jnp.arange(l)).astype(F32)[:, None] * inv[None, :]
    cos = jnp.concatenate([jnp.cos(ang), jnp.cos(ang)], axis=-1)
    sin = jnp.concatenate([-jnp.sin(ang), jnp.sin(ang)], axis=-1)
    lg = jnp.log1p(-jnp.exp2(-5.0 - jnp.arange(H_RET, dtype=F32)))
    i = jnp.arange(tb)
    d = (i[:, None] - i[None, :]).astype(F32)
    ci, cj = i[:, None] // CHUNK, i[None, :] // CHUNK
    expo = jnp.where(ci == cj, jnp.abs(d), d)
    dmat = jnp.where((cj <= ci)[None], jnp.exp(expo[None] * lg[:, None, None]), 0.0)
    fi = i.astype(F32)
    qdec = jnp.broadcast_to(jnp.exp((fi[None, :] + 1.0) * lg[:, None])[:, :, None], (H_RET, tb, DK_RET))
    kdec = jnp.broadcast_to(jnp.exp((tb - 1.0 - fi[None, :]) * lg[:, None])[:, :, None], (H_RET, tb, DK_RET))
    sdec = jnp.broadcast_to(jnp.exp(tb * lg)[:, None, None], (H_RET, 1, DV_RET))
    gr_off = (H_RET * DV_RET) // DV_RET
    return pl.pallas_call(
        _ret_kernel,
        grid=(nb, H_RET, nt),
        in_specs=[pl.BlockSpec((1, tb, DK_RET), lambda b, h, t: (b, t, h)),
                  pl.BlockSpec((1, tb, DK_RET), lambda b, h, t: (b, t, H_RET + h)),
                  pl.BlockSpec((1, tb, DV_RET), lambda b, h, t: (b, t, h)),
                  pl.BlockSpec((1, tb, DV_RET), lambda b, h, t: (b, t, gr_off + h)),
                  pl.BlockSpec((tb, DK_RET), lambda b, h, t: (t, 0)),
                  pl.BlockSpec((tb, DK_RET), lambda b, h, t: (t, 0)),
                  pl.BlockSpec((1, tb, tb), lambda b, h, t: (h, 0, 0)),
                  pl.BlockSpec((1, tb, DK_RET), lambda b, h, t: (h, 0, 0)),
                  pl.BlockSpec((1, tb, DK_RET), lambda b, h, t: (h, 0, 0)),
                  pl.BlockSpec((1, 1, DV_RET), lambda b, h, t: (h, 0, 0)),
                  pl.BlockSpec((1, 1, DK_RET, DV_RET), lambda b, h, t: (b, h, 0, 0)),
                  pl.BlockSpec((1, DV_RET), lambda b, h, t: (0, h))],
        out_specs=[pl.BlockSpec((1, tb, DV_RET), lambda b, h, t: (b, t, h)),
                   pl.BlockSpec((1, 1, DK_RET, DV_RET), lambda b, h, t: (b, h, 0, 0))],
        out_shape=[jax.ShapeDtypeStruct((nb, l, H_RET * DV_RET), BF16),
                   jax.ShapeDtypeStruct((nb, H_RET, DK_RET, DV_RET), F32)],
        scratch_shapes=[pltpu.VMEM((DK_RET, DV_RET), F32)],
        compiler_params=_params(("arbitrary", "arbitrary", "arbitrary"), VMEM_LIMIT),
        name="retention",
    )(zqk, zqk, zb, zb, cos, sin, dmat, qdec, kdec, sdec, s0, g_ret_gn.reshape(1, -1))


def _fox_prompt_kernel(q_ref, k_ref, v_ref, fk_ref, ftok_ref, o_ref, m_scr, l_scr, acc_scr, fq_scr):
    h = pl.program_id(1)
    qi = pl.program_id(2)
    ki = pl.program_id(3)
    tq = q_ref.shape[1]
    tk = k_ref.shape[1]

    @pl.when(ki == 0)
    def _():
        m_scr[...] = jnp.full_like(m_scr, -jnp.inf)
        l_scr[...] = jnp.zeros_like(l_scr)
        acc_scr[...] = jnp.zeros_like(acc_scr)
        ft = ftok_ref[0]
        lane = lax.broadcasted_iota(I32, ft.shape, 1)
        fq_scr[...] = jnp.sum(jnp.where(lane == h, ft, 0.0), axis=-1, keepdims=True)

    def step(masked):
        s = _nt_dot(q_ref[0], k_ref[0].astype(BF16))
        s = s + fq_scr[...] - fk_ref[0, 0]
        if masked:
            row = lax.broadcasted_iota(I32, (tq, tk), 0)
            col = lax.broadcasted_iota(I32, (tq, tk), 1)
            s = jnp.where(col <= row, s, -jnp.inf)
        m_prev = m_scr[...]
        m_new = jnp.maximum(m_prev, jnp.max(s, axis=-1, keepdims=True))
        alpha = jnp.exp(m_prev - m_new)
        p = jnp.exp(s - m_new)
        l_scr[...] = alpha * l_scr[...] + jnp.sum(p, axis=-1, keepdims=True)
        acc_scr[...] = alpha * acc_scr[...] + jnp.dot(p.astype(BF16), v_ref[0].astype(BF16),
                                                      preferred_element_type=F32)
        m_scr[...] = m_new

    @pl.when(ki < qi)
    def _():
        step(False)

    @pl.when(ki == qi)
    def _():
        step(True)
        o_ref[0] = (acc_scr[...] / l_scr[...]).astype(BF16)


def _fox_prompt(zb, kf, vf, f_rows, f_tok, tq):
    nb, s, _ = kf.shape
    nq = s // tq
    q_off = (2 * H_FOX * DH_FOX) // DH_FOX
    kv_spec = pl.BlockSpec((1, tq, DH_FOX), lambda b, h, qi, ki: (b, jnp.minimum(ki, qi), h))
    return pl.pallas_call(
        _fox_prompt_kernel,
        grid=(nb, H_FOX, nq, nq),
        in_specs=[pl.BlockSpec((1, tq, DH_FOX), lambda b, h, qi, ki: (b, qi, q_off + h)),
                  kv_spec, kv_spec,
                  pl.BlockSpec((1, 1, 1, tq), lambda b, h, qi, ki: (b, h, 0, jnp.minimum(ki, qi))),
                  pl.BlockSpec((1, tq, H_FOX), lambda b, h, qi, ki: (b, qi, 0))],
        out_specs=pl.BlockSpec((1, tq, DH_FOX), lambda b, h, qi, ki: (b, qi, h)),
        out_shape=jax.ShapeDtypeStruct((nb, s, H_FOX * DH_FOX), BF16),
        scratch_shapes=[pltpu.VMEM((tq, 1), F32), pltpu.VMEM((tq, 1), F32),
                        pltpu.VMEM((tq, DH_FOX), F32), pltpu.VMEM((tq, 1), F32)],
        compiler_params=_params(("arbitrary",) * 4, VMEM_LIMIT),
        name="fox_prompt",
    )(zb, kf, vf, f_rows.reshape(nb, H_FOX, 1, s), f_tok)


def _fox_sample_kernel(q_ref, ck_ref, cv_ref, kn_ref, vn_ref, fk_ref, ftok_ref, o_ref, *, heads):
    g = pl.program_id(1)
    l = q_ref.shape[1]
    p_len = ck_ref.shape[1]
    ft = ftok_ref[0]
    lane = lax.broadcasted_iota(I32, ft.shape, 1)
    row = lax.broadcasted_iota(I32, (l, l), 0)
    col = lax.broadcasted_iota(I32, (l, l), 1)
    for hh in range(heads):
        hg = g * heads + hh
        sl = slice(hh * DH_FOX, (hh + 1) * DH_FOX)
        fk = fk_ref[0, pl.ds(hg, 1), :]
        fq = jnp.sum(jnp.where(lane == hg, ft, 0.0), axis=-1, keepdims=True)
        qh = q_ref[0, :, sl]
        s1 = _nt_dot(qh, ck_ref[0, :, sl].astype(BF16)) + fq - fk[:, :p_len]
        s2 = _nt_dot(qh, kn_ref[0, :, sl].astype(BF16)) + fq - fk[:, p_len:p_len + l]
        s2 = jnp.where(col <= row, s2, -jnp.inf)
        m = jnp.maximum(jnp.max(s1, axis=-1, keepdims=True), jnp.max(s2, axis=-1, keepdims=True))
        p1 = jnp.exp(s1 - m)
        p2 = jnp.exp(s2 - m)
        den = jnp.sum(p1, axis=-1, keepdims=True) + jnp.sum(p2, axis=-1, keepdims=True)
        o = jnp.dot(p1.astype(BF16), cv_ref[0, :, sl].astype(BF16), preferred_element_type=F32)
        o = o + jnp.dot(p2.astype(BF16), vn_ref[0, :, sl].astype(BF16), preferred_element_type=F32)
        o_ref[0, :, sl] = (o / den).astype(BF16)


def _fox_sample(zb, kf, vf, cache_k, cache_v, f_rows, f_tok, heads=4):
    nb, l, hd = kf.shape
    p_len = cache_k.shape[1]
    sp = f_rows.shape[-1]
    w = heads * DH_FOX
    q_off = (2 * hd) // w
    assert p_len % l == 0
    new_spec = pl.BlockSpec((1, l, w), lambda b, g: (b, 0, g))
    cache_spec = pl.BlockSpec((1, p_len, w), lambda b, g: (b, 0, g))
    return pl.pallas_call(
        functools.partial(_fox_sample_kernel, heads=heads),
        grid=(nb, H_FOX // heads),
        in_specs=[pl.BlockSpec((1, l, w), lambda b, g: (b, 0, q_off + g)),
                  cache_spec, cache_spec, new_spec, new_spec,
                  pl.BlockSpec((1, H_FOX, sp), lambda b, g: (b, 0, 0)),
                  pl.BlockSpec((1, l, H_FOX), lambda b, g: (b, p_len // l, 0))],
        out_specs=new_spec,
        out_shape=jax.ShapeDtypeStruct((nb, l, hd), BF16),
        compiler_params=_params(("arbitrary", "arbitrary"), VMEM_LIMIT),
        name="fox_sample",
    )(zb, cache_k, cache_v, kf, vf, f_rows, f_tok)


def _mix1_kernel(or_ref, of_ref, w1_ref, w2_ref, gr_ref, gf_ref, o_ref):
    a = jnp.dot(or_ref[...], w1_ref[...], preferred_element_type=F32)
    b = jnp.dot(of_ref[...], w2_ref[...], preferred_element_type=F32)
    gr = _sigmoid(gr_ref[...].astype(F32))
    gf = _sigmoid(gf_ref[...].astype(F32))
    o_ref[...] = (gr * a + gf * b).astype(BF16)


def _mix1(o_r, o_f, w1, w2, zb, tm, tn):
    m, d = o_r.shape
    gr_off = (3 * d) // tn
    gf_off = (4 * d) // tn
    return pl.pallas_call(
        _mix1_kernel,
        grid=(m // tm, d // tn),
        in_specs=[pl.BlockSpec((tm, d), lambda i, j: (i, 0)),
                  pl.BlockSpec((tm, d), lambda i, j: (i, 0)),
                  pl.BlockSpec((d, tn), lambda i, j: (0, j)),
                  pl.BlockSpec((d, tn), lambda i, j: (0, j)),
                  pl.BlockSpec((tm, tn), lambda i, j: (i, gr_off + j)),
                  pl.BlockSpec((tm, tn), lambda i, j: (i, gf_off + j))],
        out_specs=pl.BlockSpec((tm, tn), lambda i, j: (i, j)),
        out_shape=jax.ShapeDtypeStruct((m, d), BF16),
        compiler_params=_params(("arbitrary", "arbitrary"), VMEM_LIMIT),
        name="mix_gates",
    )(o_r, o_f, w1, w2, zb, zb)


def _mix2_kernel(m_ref, w_ref, x_ref, gt_ref, sh_ref, sc_ref, g_ref, wr_ref, br_ref, cnt0_ref,
                 h1_ref, hn_ref, eidx_ref, rank_ref, wts_ref, cnt_ref):
    bb, tl, d = x_ref.shape
    tm = bb * tl
    ne = wr_ref.shape[1]

    @pl.when((pl.program_id(0) == 0) & (pl.program_id(1) == 0))
    def _():
        cnt_ref[...] = cnt0_ref[...]

    mix = jnp.dot(m_ref[...].reshape(tm, d), w_ref[...], preferred_element_type=F32)
    h1 = x_ref[...] + gt_ref[...] * mix.reshape(bb, tl, d)
    h1_ref[...] = h1
    y = h1 * lax.rsqrt(jnp.mean(h1 * h1, axis=-1, keepdims=True) + EPS) * g_ref[...]
    hn = (y * (1.0 + sc_ref[...]) + sh_ref[...]).reshape(tm, d)
    hn_ref[...] = _pack_halves(hn)

    logits = jnp.dot(hn, wr_ref[...], precision=lax.Precision.HIGHEST, preferred_element_type=F32)
    scores = _sigmoid(logits)
    cur = scores + br_ref[...]
    lane = lax.broadcasted_iota(I32, (tm, ne), 1).astype(F32)
    slot = lax.broadcasted_iota(I32, (tm, LANES), 1)
    r = lax.broadcasted_iota(I32, (tm, tm), 0)
    c = lax.broadcasted_iota(I32, (tm, tm), 1)
    lower = (c < r).astype(BF16)
    picks = []
    mask = jnp.zeros((tm, ne), F32)
    for _ in range(TOP_K):
        mx = jnp.max(cur, axis=-1, keepdims=True)
        idx = jnp.min(jnp.where(cur == mx, lane, float(ne)), axis=-1, keepdims=True)
        pick = lane == idx
        picks.append((idx, pick))
        mask = jnp.where(pick, 1.0, mask)
        cur = jnp.where(pick, -jnp.inf, cur)
    rank = jnp.dot(lower, mask.astype(BF16), preferred_element_type=F32) + cnt_ref[0:1, :]
    cnt_ref[...] = cnt_ref[...] + jnp.sum(mask, axis=0, keepdims=True)
    wsum = jnp.sum(mask * scores, axis=-1, keepdims=True)
    eidx = jnp.zeros((tm, LANES), I32)
    rnk = jnp.zeros((tm, LANES), I32)
    wts = jnp.zeros((tm, LANES), F32)
    for kk, (idx, pick) in enumerate(picks):
        sc_k = jnp.sum(jnp.where(pick, scores, 0.0), axis=-1, keepdims=True)
        rk_k = jnp.sum(jnp.where(pick, rank, 0.0), axis=-1, keepdims=True)
        eidx = jnp.where(slot == kk, idx.astype(I32), eidx)
        rnk = jnp.where(slot == kk, rk_k.astype(I32), rnk)
        wts = jnp.where(slot == kk, sc_k / wsum * ROUTE_SCALE, wts)
    eidx_ref[...] = eidx
    rank_ref[...] = rnk
    wts_ref[...] = wts


def _mix2(m3, w_out, x, mod, g2, w_router, b_router, cnt0, bb, tl):
    nb, l, d = x.shape
    ne = w_router.shape[1]
    t = nb * l
    tm = bb * tl
    nj = l // tl
    mod_spec = lambda k: pl.BlockSpec((bb, 1, d), lambda i, j: (i, 0, k))
    tok_spec = lambda w: pl.BlockSpec((tm, w), lambda i, j: (i * nj + j, 0))
    blk3 = pl.BlockSpec((bb, tl, d), lambda i, j: (i, j, 0))
    return pl.pallas_call(
        _mix2_kernel,
        grid=(nb // bb, nj),
        in_specs=[blk3,
                  pl.BlockSpec((d, d), lambda i, j: (0, 0)),
                  blk3,
                  mod_spec(2), mod_spec(3), mod_spec(4),
                  pl.BlockSpec((1, 1, d), lambda i, j: (0, 0, 0)),
                  pl.BlockSpec((d, ne), lambda i, j: (0, 0)),
                  pl.BlockSpec((1, ne), lambda i, j: (0, 0)),
                  pl.BlockSpec((8, ne), lambda i, j: (0, 0))],
        out_specs=[blk3, tok_spec(d // 2), tok_spec(LANES), tok_spec(LANES), tok_spec(LANES),
                   pl.BlockSpec((8, ne), lambda i, j: (0, 0))],
        out_shape=[jax.ShapeDtypeStruct((nb, l, d), F32),
                   jax.ShapeDtypeStruct((t, d // 2), I32),
                   jax.ShapeDtypeStruct((t, LANES), I32),
                   jax.ShapeDtypeStruct((t, LANES), I32),
                   jax.ShapeDtypeStruct((t, LANES), F32),
                   jax.ShapeDtypeStruct((8, ne), F32)],
        compiler_params=_params(("arbitrary", "arbitrary"), VMEM_LIMIT),
        name="outproj_norm2_router",
    )(m3, w_out, x, mod, mod, mod, g2.reshape(1, 1, d), w_router, b_router.reshape(1, ne), cnt0)


def _dispatch_kernel(dest_ref, x_ref, buf_in_ref, buf_ref, sem):
    del buf_in_ref
    tb = x_ref.shape[0]

    def row_copy(r, kk):
        d = dest_ref[r * DEST_W + kk]
        return pltpu.make_async_copy(x_ref.at[pl.ds(r, 1), :], buf_ref.at[pl.ds(d, 1), :], sem)

    def issue(r, carry):
        for kk in range(TOP_K):
            row_copy(r, kk).start()
        return carry

    def drain(r, carry):
        for kk in range(TOP_K):
            row_copy(r, kk).wait()
        return carry

    lax.fori_loop(0, tb, issue, 0)
    lax.fori_loop(0, tb, drain, 0)


def _dispatch(xp, dest, buf0, tb):
    t, w = xp.shape
    return pl.pallas_call(
        _dispatch_kernel,
        grid=(t // tb,),
        in_specs=[pl.BlockSpec((tb * DEST_W,), lambda i: (i,), memory_space=pltpu.SMEM),
                  pl.BlockSpec((tb, w), lambda i: (i, 0)),
                  pl.BlockSpec(memory_space=pl.ANY)],
        out_specs=pl.BlockSpec(memory_space=pl.ANY),
        out_shape=jax.ShapeDtypeStruct(buf0.shape, buf0.dtype),
        scratch_shapes=[pltpu.SemaphoreType.DMA(())],
        input_output_aliases={2: 0},
        compiler_params=_params(("arbitrary",), VMEM_LIMIT),
        name="moe_dispatch",
    )(dest, xp, buf0)


def _expert_kernel(be_ref, nu_ref, x_ref, wg_ref, wu_ref, wd_ref, wgt_ref, wut_ref, wdt_ref, o_ref,
                   xb_scr, acc_scr, *, n_main):
    del be_ref
    i = pl.program_id(0)
    j = pl.program_id(1)
    half = x_ref.shape[1]

    @pl.when(i < nu_ref[0])
    def _():
        @pl.when(j == 0)
        def _():
            hi, lo = _unpack_halves(x_ref[...])
            xb_scr[:, :half] = hi.astype(BF16)
            xb_scr[:, half:] = lo.astype(BF16)
            acc_scr[...] = jnp.zeros_like(acc_scr)

        def ffn(wg, wu, wd):
            xb = xb_scr[...]
            gate = jnp.dot(xb, wg[0].astype(BF16), preferred_element_type=F32)
            up = jnp.dot(xb, wu[0].astype(BF16), preferred_element_type=F32)
            hid = (gate * _sigmoid(gate) * up).astype(BF16)
            acc_scr[...] += jnp.dot(hid, wd[0].astype(BF16), preferred_element_type=F32)

        @pl.when(j < n_main)
        def _():
            ffn(wg_ref, wu_ref, wd_ref)

        @pl.when(j == n_main)
        def _():
            ffn(wgt_ref, wut_ref, wdt_ref)
            o_ref[...] = _pack_halves(acc_scr[...])

    @pl.when((i >= nu_ref[0]) & (j == n_main))
    def _():
        o_ref[...] = jnp.zeros_like(o_ref)


def _expert_ffn(xp, wg, wu, wd, block_expert, n_used, rows_blk):
    rows, half = xp.shape
    d = 2 * half
    f = wg.shape[2]
    n_main = f // F_TILE
    tail = f - n_main * F_TILE
    assert tail > 0 and tail % LANES == 0 and (n_main * F_TILE) % tail == 0
    tail_blk = (n_main * F_TILE) // tail
    nblk = rows // rows_blk

    def row_map(i, j, be, nu):
        return (jnp.minimum(i, nu[0] - 1), 0)

    def main_col(i, j, be, nu):
        return (be[i], 0, jnp.minimum(j, n_main - 1))

    def main_row(i, j, be, nu):
        return (be[i], jnp.minimum(j, n_main - 1), 0)

    grid_spec = pltpu.PrefetchScalarGridSpec(
        num_scalar_prefetch=2,
        grid=(nblk, n_main + 1),
        in_specs=[pl.BlockSpec((rows_blk, half), row_map),
                  pl.BlockSpec((1, d, F_TILE), main_col),
                  pl.BlockSpec((1, d, F_TILE), main_col),
                  pl.BlockSpec((1, F_TILE, d), main_row),
                  pl.BlockSpec((1, d, tail), lambda i, j, be, nu: (be[i], 0, tail_blk)),
                  pl.BlockSpec((1, d, tail), lambda i, j, be, nu: (be[i], 0, tail_blk)),
                  pl.BlockSpec((1, tail, d), lambda i, j, be, nu: (be[i], tail_blk, 0))],
        out_specs=pl.BlockSpec((rows_blk, half), lambda i, j, be, nu: (i, 0)),
        scratch_shapes=[pltpu.VMEM((rows_blk, d), BF16), pltpu.VMEM((rows_blk, d), F32)],
    )
    return pl.pallas_call(
        functools.partial(_expert_kernel, n_main=n_main),
        grid_spec=grid_spec,
        out_shape=jax.ShapeDtypeStruct((rows, half), I32),
        compiler_params=_params(("arbitrary", "arbitrary"), VMEM_LIMIT),
        name="expert_ffn",
    )(block_expert, n_used, xp, wg, wu, wd, wg, wu, wd)


def _combine_kernel(dcur_ref, dnext_ref, wts_ref, sh_ref, h1_ref, gt_ref, g_ref, eo_ref, y_ref,
                    gbuf, sems):
    bb, tl, d = h1_ref.shape
    tb = bb * tl
    half = d // 2
    nj = pl.num_programs(1)
    step = pl.program_id(0) * nj + pl.program_id(1)
    nsteps = pl.num_programs(0) * nj
    slot = step % 2

    def row_copy(dref, sl, r, kk):
        dd = dref[r * DEST_W + kk]
        return pltpu.make_async_copy(eo_ref.at[pl.ds(dd, 1), :], gbuf.at[sl, kk, pl.ds(r, 1), :],
                                     sems.at[sl])

    def issue(dref, sl):
        def body(r, carry):
            for kk in range(TOP_K):
                row_copy(dref, sl, r, kk).start()
            return carry
        lax.fori_loop(0, tb, body, 0)

    @pl.when(step == 0)
    def _():
        issue(dcur_ref, 0)

    @pl.when(step + 1 < nsteps)
    def _():
        issue(dnext_ref, 1 - slot)

    def drain(r, carry):
        for kk in range(TOP_K):
            row_copy(dcur_ref, slot, r, kk).wait()
        return carry
    lax.fori_loop(0, tb, drain, 0)

    wts = wts_ref[...]
    s_hi, s_lo = _unpack_halves(sh_ref[...])
    for kk in range(TOP_K):
        hi, lo = _unpack_halves(gbuf[slot, kk])
        wk = wts[:, kk:kk + 1]
        s_hi = s_hi + wk * hi
        s_lo = s_lo + wk * lo
    h2_hi = h1_ref[:, :, :half] + gt_ref[:, :, :half] * s_hi.reshape(bb, tl, half)
    h2_lo = h1_ref[:, :, half:] + gt_ref[:, :, half:] * s_lo.reshape(bb, tl, half)
    ms = (jnp.sum(h2_hi * h2_hi, axis=-1, keepdims=True)
          + jnp.sum(h2_lo * h2_lo, axis=-1, keepdims=True)) / d
    inv = lax.rsqrt(ms + EPS)
    y_ref[:, :, :half] = h2_hi * inv * g_ref[:, :, :half]
    y_ref[:, :, half:] = h2_lo * inv * g_ref[:, :, half:]


def _combine(dest, wts, shared, h1, mod, g_final, eo, bb, tl):
    nb, l, d = h1.shape
    tb = bb * tl
    nj = l // tl
    nsteps = (nb // bb) * nj
    lin = lambda i, j: i * nj + j
    return pl.pallas_call(
        _combine_kernel,
        grid=(nb // bb, nj),
        in_specs=[pl.BlockSpec((tb * DEST_W,), lambda i, j: (lin(i, j),), memory_space=pltpu.SMEM),
                  pl.BlockSpec((tb * DEST_W,), lambda i, j: (jnp.minimum(lin(i, j) + 1, nsteps - 1),),
                               memory_space=pltpu.SMEM),
                  pl.BlockSpec((tb, LANES), lambda i, j: (lin(i, j), 0)),
                  pl.BlockSpec((tb, d // 2), lambda i, j: (lin(i, j), 0)),
                  pl.BlockSpec((bb, tl, d), lambda i, j: (i, j, 0)),
                  pl.BlockSpec((bb, 1, d), lambda i, j: (i, 0, 5)),
                  pl.BlockSpec((1, 1, d), lambda i, j: (0, 0, 0)),
                  pl.BlockSpec(memory_space=pl.ANY)],
        out_specs=pl.BlockSpec((bb, tl, d), lambda i, j: (i, j, 0)),
        out_shape=jax.ShapeDtypeStruct((nb, l, d), F32),
        scratch_shapes=[pltpu.VMEM((2, TOP_K, tb, d // 2), I32), pltpu.SemaphoreType.DMA((2,))],
        compiler_params=_params(("arbitrary", "arbitrary"), VMEM_LIMIT),
        name="moe_combine",
    )(dest, dest, wts, shared, h1, mod, g_final.reshape(1, 1, d), eo)


def _prep_w_in(w_in):
    d = w_in.shape[0]
    qr_w, vr_w, qf_w = H_RET * DK_RET, H_RET * DV_RET, H_FOX * DH_FOX
    sizes = (qr_w, qr_w, vr_w, vr_w, qf_w, qf_w, qf_w, H_FOX, d, d)
    offs = [0]
    for s in sizes:
        offs.append(offs[-1] + s)
    part = lambda n: w_in[:, offs[n]:offs[n + 1]]
    w_qk = jnp.concatenate([part(0), part(1) * (DK_RET ** -0.5)], axis=1).astype(BF16)
    w_b = jnp.concatenate([part(2), part(3), part(4) * (DH_FOX ** -0.5), part(8), part(9)], axis=1).astype(BF16)
    w_kf = part(5).astype(BF16)
    w_vf = part(6).astype(BF16)
    w_fl = jnp.pad(part(7), ((0, 0), (0, LANES - H_FOX))).astype(BF16)
    return w_qk, w_b, w_kf, w_vf, w_fl


def _trunk_front(x, mod, lw, bb, tl, tm, fox, s0, pos0, ret_tb):
    nb, l, d = x.shape
    t = nb * l
    hn, logf = _norm1(x, lw["g_norm1"], mod, lw["w_fl"], lw["b_fl"], bb, tl)
    hn2d = hn.reshape(t, d)
    zqk = _matmul(hn2d, lw["w_qk"], F32, tm, 512, "proj_qk_ret")
    kf = _matmul(hn2d, lw["w_kf"], F32, tm, 512, "proj_k_fox")
    vf = _matmul(hn2d, lw["w_vf"], F32, tm, 512, "proj_v_fox")
    zb = _matmul(hn2d, lw["w_b"], BF16, tm, 512, "proj_rest")
    zb3 = zb.reshape(nb, l, -1)
    kf3 = kf.reshape(nb, l, -1)
    vf3 = vf.reshape(nb, l, -1)
    o_r, s_new = _retention(zqk.reshape(nb, l, -1), zb3, s0, lw["g_ret_gn"], pos0, ret_tb)
    o_f = fox(zb3, kf3, vf3, logf)
    m = _mix1(o_r.reshape(t, d), o_f.reshape(t, d), lw["w_ret_out"], lw["w_fox_out"], zb, tm, 512)
    return m.reshape(nb, l, d), (kf3, vf3, logf, s_new)


def kernel(x_prompt, x_sample, c_prompt, c_sample, cache_fox_k, cache_fox_v, cache_fox_logf, state_ret, w_ada, b_ada, g_norm1, w_in, b_forget, g_ret_gn, w_ret_out, w_fox_out, w_out, g_norm2, w_router, b_router, w_exp_gate, w_exp_up, w_exp_down, w_sh_gate, w_sh_up, w_sh_down, g_final):
    depth = w_in.shape[0]
    assert depth == 1
    nbp, s, d = x_prompt.shape
    nbs, ls, _ = x_sample.shape
    p_len = cache_fox_k.shape[2]
    ne = w_router.shape[-1]
    tp, ts = nbp * s, nbs * ls

    nb_all = nbp + nbs
    nb_pad = -(-nb_all // 8) * 8
    c_all = jnp.pad(jnp.concatenate([c_prompt, c_sample], axis=0), ((0, nb_pad - nb_all), (0, 0)))
    mod_all = _adaln(c_all, w_ada[0], b_ada[0])
    mod_p = mod_all[:nbp].reshape(nbp, 1, 6 * d)
    mod_s = mod_all[nbp:nb_all].reshape(nbs, 1, 6 * d)

    w_qk, w_b, w_kf, w_vf, w_fl = _prep_w_in(w_in[0])
    lw = dict(g_norm1=g_norm1[0], w_fl=w_fl,
              b_fl=jnp.pad(b_forget[0], (0, LANES - H_FOX)).reshape(1, LANES),
              w_qk=w_qk, w_b=w_b, w_kf=w_kf, w_vf=w_vf, g_ret_gn=g_ret_gn[0],
              w_ret_out=w_ret_out[0].astype(BF16), w_fox_out=w_fox_out[0].astype(BF16))
    w_out_b = w_out[0].astype(BF16)

    tl_p = min(s, 512)
    tm_p = min(tp, 1024)
    bb_s = max(1, min(nbs, 512 // ls))
    tm_s = min(ts, 1024)

    def fox_p(zb3, kf3, vf3, logf):
        f_rows = _cumsum_rows(logf.swapaxes(1, 2), min(s, 512))
        return _fox_prompt(zb3, kf3, vf3, f_rows, f_rows.swapaxes(1, 2), min(s, 512))

    def fox_s(zb3, kf3, vf3, logf):
        full = jnp.concatenate([cache_fox_logf[0].astype(F32), logf], axis=1)
        sp = -(-(p_len + ls) // LANES) * LANES
        full = jnp.pad(full, ((0, 0), (0, sp - p_len - ls), (0, 0)))
        f_rows = _cumsum_rows(full.swapaxes(1, 2), sp)
        return _fox_sample(zb3, kf3, vf3, cache_fox_k[0].reshape(nbs, p_len, -1),
                           cache_fox_v[0].reshape(nbs, p_len, -1), f_rows, f_rows.swapaxes(1, 2))

    s0_p = jnp.zeros((nbp, H_RET, DK_RET, DV_RET), F32)
    m_p, st_p = _trunk_front(x_prompt, mod_p, lw, 1, tl_p, tm_p, fox_p, s0_p, 0, min(s, 256))
    m_s, st_s = _trunk_front(x_sample, mod_s, lw, bb_s, ls, tm_s, fox_s, state_ret[0], p_len, ls)

    cnt0 = jnp.zeros((8, ne), F32)
    tl2_p = min(s, 256)
    bb2_s = max(1, min(nbs, 256 // ls))
    h1_p, hn_p, e_p, r_p, wt_p, cnt_p = _mix2(m_p, w_out_b, x_prompt, mod_p, g_norm2[0], w_router[0],
                                              b_router[0], cnt0, 1, tl2_p)
    h1_s, hn_s, e_s, r_s, wt_s, cnt_s = _mix2(m_s, w_out_b, x_sample, mod_s, g_norm2[0], w_router[0],
                                              b_router[0], cnt_p, bb2_s, ls)

    counts = cnt_s[0].astype(I32)
    rblk = EXPERT_ROWS
    padded = (counts + rblk - 1) // rblk * rblk
    pad_end = jnp.cumsum(padded)
    pad_start = pad_end - padded
    t_all = tp + ts
    nblk = (t_all * TOP_K + ne * (rblk - 1) + rblk - 1) // rblk
    rows = nblk * rblk
    block_expert = jnp.minimum(jnp.searchsorted(pad_end, jnp.arange(nblk) * rblk, side="right"), ne - 1).astype(I32)
    n_used = (pad_end[-1] // rblk).astype(I32).reshape(1)
    e_all = jnp.concatenate([e_p, e_s], axis=0)[:, :DEST_W]
    r_all = jnp.concatenate([r_p, r_s], axis=0)[:, :DEST_W]
    dest = (pad_start[e_all] + r_all).astype(I32).reshape(-1)
    hn_all = jnp.concatenate([hn_p, hn_s], axis=0)

    buf = _dispatch(hn_all, dest, jnp.zeros((rows, d // 2), I32), min(t_all, 256))
    eo = _expert_ffn(buf, w_exp_gate[0], w_exp_up[0], w_exp_down[0], block_expert, n_used, rblk)
    sh_rows = min(t_all, rblk)
    shared = _expert_ffn(hn_all, w_sh_gate, w_sh_up, w_sh_down, jnp.zeros((t_all // sh_rows,), I32),
                         jnp.full((1,), t_all // sh_rows, I32), sh_rows)

    tlc_p = min(s, 128)
    bbc_s = max(1, min(nbs, 128 // ls))
    y_p = _combine(dest[:tp * DEST_W], wt_p, shared[:tp], h1_p, mod_p, g_final, eo, 1, tlc_p)
    y_s = _combine(dest[tp * DEST_W:], wt_s, shared[tp:], h1_s, mod_s, g_final, eo, bbc_s, ls)

    kf_p, vf_p, logf_p, s_p = st_p
    kf_s, vf_s, logf_s, s_s = st_s
    hd = lambda a: a.reshape(1, a.shape[0], a.shape[1], H_FOX, DH_FOX)
    return (y_p, y_s, hd(kf_p), hd(vf_p), logf_p[None], s_p[None],
            hd(kf_s), hd(vf_s), logf_s[None], s_s[None])
```

```python
import functools
import math

import jax
import jax.numpy as jnp
from jax import lax
from jax.experimental import pallas as pl
from jax.experimental.pallas import tpu as pltpu

F32 = jnp.float32
BF16 = jnp.bfloat16
I32 = jnp.int32

CHUNK = 64
H_RET = 8
DK_RET = 128
DV_RET = 256
H_FOX = 16
DH_FOX = 128
TOP_K = 6
ROUTE_SCALE = 2.5
ROPE_BASE = 10000.0
EPS = 1e-6

LANES = 128
DEST_W = 8
EXPERT_ROWS = 1024
SUB_ROWS = 256
F_TILE = 256
VMEM_LIMIT = 56 * 1024 * 1024


def _params(sem, vmem=None):
    return pltpu.CompilerParams(dimension_semantics=sem, vmem_limit_bytes=vmem)


def _sigmoid(x):
    return 1.0 / (1.0 + jnp.exp(-x))


def _nt_dot(a, b):
    return lax.dot_general(a, b, (((1,), (1,)), ((), ())), preferred_element_type=F32)


def _pack_halves(x):
    n = x.shape[-1] // 2
    hi = pltpu.bitcast(x[:, :n].astype(BF16).astype(F32), I32)
    lo = pltpu.bitcast(x[:, n:].astype(BF16).astype(F32), I32)
    return hi | lax.shift_right_logical(lo, jnp.int32(16))


def _unpack_halves(u):
    hi = pltpu.bitcast(u & jnp.int32(-65536), F32)
    lo = pltpu.bitcast(lax.shift_left(u, jnp.int32(16)), F32)
    return hi, lo


def _ada_kernel(c_ref, w_ref, b_ref, o_ref):
    c = c_ref[...]
    s = (c * _sigmoid(c)).astype(BF16)
    o_ref[...] = jnp.dot(s, w_ref[...].astype(BF16), preferred_element_type=F32) + b_ref[...]


def _adaln(c, w_ada, b_ada):
    nb, d = c.shape
    n = w_ada.shape[1]
    tn = 1024
    return pl.pallas_call(
        _ada_kernel,
        grid=(n // tn,),
        in_specs=[pl.BlockSpec((nb, d), lambda j: (0, 0)),
                  pl.BlockSpec((d, tn), lambda j: (0, j)),
                  pl.BlockSpec((1, tn), lambda j: (0, j))],
        out_specs=pl.BlockSpec((nb, tn), lambda j: (0, j)),
        out_shape=jax.ShapeDtypeStruct((nb, n), F32),
        compiler_params=_params(("arbitrary",), VMEM_LIMIT),
        name="adaln",
    )(c, w_ada, b_ada.reshape(1, n))


def _norm1_kernel(x_ref, g_ref, sh_ref, sc_ref, wf_ref, bf_ref, hn_ref, logf_ref):
    bb, tl, d = x_ref.shape
    x = x_ref[...]
    y = x * lax.rsqrt(jnp.mean(x * x, axis=-1, keepdims=True) + EPS) * g_ref[...]
    hn = (y * (1.0 + sc_ref[...]) + sh_ref[...]).astype(BF16)
    hn_ref[...] = hn
    z = jnp.dot(hn.reshape(bb * tl, d), wf_ref[...], preferred_element_type=F32) + bf_ref[...]
    logf = jnp.minimum(z, 0.0) - jnp.log(1.0 + jnp.exp(-jnp.abs(z)))
    logf_ref[...] = logf[:, :H_FOX].reshape(bb, tl, H_FOX)


def _norm1(x, g, mod, wf, bfg, bb, tl):
    nb, l, d = x.shape
    mod_spec = lambda k: pl.BlockSpec((bb, 1, d), lambda i, j: (i, 0, k))
    return pl.pallas_call(
        _norm1_kernel,
        grid=(nb // bb, l // tl),
        in_specs=[pl.BlockSpec((bb, tl, d), lambda i, j: (i, j, 0)),
                  pl.BlockSpec((1, 1, d), lambda i, j: (0, 0, 0)),
                  mod_spec(0), mod_spec(1),
                  pl.BlockSpec((d, LANES), lambda i, j: (0, 0)),
                  pl.BlockSpec((1, LANES), lambda i, j: (0, 0))],
        out_specs=[pl.BlockSpec((bb, tl, d), lambda i, j: (i, j, 0)),
                   pl.BlockSpec((bb, tl, H_FOX), lambda i, j: (i, j, 0))],
        out_shape=[jax.ShapeDtypeStruct((nb, l, d), BF16),
                   jax.ShapeDtypeStruct((nb, l, H_FOX), F32)],
        compiler_params=_params(("arbitrary", "arbitrary"), VMEM_LIMIT),
        name="norm1",
    )(x, g.reshape(1, 1, d), mod, mod, wf, bfg)


def _mm_kernel(a_ref, w_ref, o_ref):
    o_ref[...] = jnp.dot(a_ref[...], w_ref[...], preferred_element_type=F32).astype(o_ref.dtype)


def _matmul(a, w, out_dtype, tm, tn, name):
    m, k = a.shape
    n = w.shape[1]
    return pl.pallas_call(
        _mm_kernel,
        grid=(m // tm, n // tn),
        in_specs=[pl.BlockSpec((tm, k), lambda i, j: (i, 0)),
                  pl.BlockSpec((k, tn), lambda i, j: (0, j))],
        out_specs=pl.BlockSpec((tm, tn), lambda i, j: (i, j)),
        out_shape=jax.ShapeDtypeStruct((m, n), out_dtype),
        compiler_params=_params(("arbitrary", "arbitrary"), VMEM_LIMIT),
        name=name,
    )(a, w)


def _mm_heads_kernel(a_ref, w_ref, o_ref, ob_ref):
    bb, tl, nh, dh = o_ref.shape
    res = jnp.dot(a_ref[...], w_ref[...], preferred_element_type=F32)
    ob_ref[...] = res.astype(BF16)
    for hh in range(nh):
        o_ref[:, :, hh, :] = res[:, hh * dh:(hh + 1) * dh].reshape(bb, tl, dh)


def _matmul_heads(a, w, nb, l, bb, tl, name):
    m, k = a.shape
    n = w.shape[1]
    nj = l // tl
    return pl.pallas_call(
        _mm_heads_kernel,
        grid=(nb // bb, nj),
        in_specs=[pl.BlockSpec((bb * tl, k), lambda i, j: (i * nj + j, 0)),
                  pl.BlockSpec((k, n), lambda i, j: (0, 0))],
        out_specs=[pl.BlockSpec((bb, tl, H_FOX, DH_FOX), lambda i, j: (i, j, 0, 0)),
                   pl.BlockSpec((bb * tl, n), lambda i, j: (i * nj + j, 0))],
        out_shape=[jax.ShapeDtypeStruct((nb, l, H_FOX, DH_FOX), F32),
                   jax.ShapeDtypeStruct((m, n), BF16)],
        compiler_params=_params(("arbitrary", "arbitrary"), VMEM_LIMIT),
        name=name,
    )(a, w)


def _cumsum_kernel(x_ref, o_ref, carry_ref):
    tl = x_ref.shape[-1]

    @pl.when(pl.program_id(1) == 0)
    def _():
        carry_ref[...] = jnp.zeros_like(carry_ref)

    r = lax.broadcasted_iota(I32, (tl, tl), 0)
    c = lax.broadcasted_iota(I32, (tl, tl), 1)
    tri = (r <= c).astype(F32)
    y = jnp.dot(x_ref[0], tri, precision=lax.Precision.HIGHEST, preferred_element_type=F32)
    y = y + carry_ref[:, 0:1]
    o_ref[0] = y
    carry_ref[...] = jnp.broadcast_to(y[:, tl - 1:tl], carry_ref.shape)


def _cumsum_rows(x, tl):
    nb, h, s = x.shape
    return pl.pallas_call(
        _cumsum_kernel,
        grid=(nb, s // tl),
        in_specs=[pl.BlockSpec((1, h, tl), lambda b, t: (b, 0, t))],
        out_specs=pl.BlockSpec((1, h, tl), lambda b, t: (b, 0, t)),
        out_shape=jax.ShapeDtypeStruct((nb, h, s), F32),
        scratch_shapes=[pltpu.VMEM((h, LANES), F32)],
        compiler_params=_params(("arbitrary", "arbitrary"), VMEM_LIMIT),
        name="cumsum_logf",
    )(x)


def _ret_kernel(q_ref, k_ref, v_ref, g_ref, cos_ref, sin_ref, dmat_ref, qdec_ref, kdec_ref, sdec_ref,
                s0_ref, gn_ref, o_ref, sout_ref, s_scr):
    t = pl.program_id(2)

    @pl.when(t == 0)
    def _():
        s_scr[...] = s0_ref[0, 0]

    cos = cos_ref[...]
    sin = sin_ref[...]
    q = q_ref[0]
    k = k_ref[0]
    half = q.shape[-1] // 2
    qr = q * cos + pltpu.roll(q, half, 1) * sin
    kr = k * cos + pltpu.roll(k, half, 1) * sin
    vb = v_ref[0]
    state = s_scr[...]
    s = _nt_dot(qr.astype(BF16), kr.astype(BF16)) * dmat_ref[0]
    o = jnp.dot(s.astype(BF16), vb, preferred_element_type=F32)
    o = o + jnp.dot((qr * qdec_ref[0]).astype(BF16), state.astype(BF16), preferred_element_type=F32)
    kd_t = (kr * kdec_ref[0]).T.astype(BF16)
    s_scr[...] = state * sdec_ref[0] + jnp.dot(kd_t, vb, preferred_element_type=F32)

    @pl.when(t == pl.num_programs(2) - 1)
    def _():
        sout_ref[0, 0] = s_scr[...]

    mu = jnp.mean(o, axis=-1, keepdims=True)
    oc = o - mu
    var = jnp.mean(oc * oc, axis=-1, keepdims=True)
    on = oc * lax.rsqrt(var + EPS) * gn_ref[...]
    g = g_ref[0].astype(F32)
    o_ref[0] = (on * (g * _sigmoid(g))).astype(BF16)


def _retention(zqk, zb, s0, g_ret_gn, pos0, tb):
    nb, l, _ = zqk.shape
    nt = l // tb
    half = DK_RET // 2
    inv = 1.0 / (ROPE_BASE ** (jnp.arange(half, dtype=F32) * 2.0 / DK_RET))
    ang = (pos0 + jnp.arange(l)).astype(F32)[:, None] * inv[None, :]
    cos = jnp.concatenate([jnp.cos(ang), jnp.cos(ang)], axis=-1)
    sin = jnp.concatenate([-jnp.sin(ang), jnp.sin(ang)], axis=-1)
    lg = jnp.log1p(-jnp.exp2(-5.0 - jnp.arange(H_RET, dtype=F32)))
    i = jnp.arange(tb)
    d = (i[:, None] - i[None, :]).astype(F32)
    ci, cj = i[:, None] // CHUNK, i[None, :] // CHUNK
    expo = jnp.where(ci == cj, jnp.abs(d), d)
    dmat = jnp.where((cj <= ci)[None], jnp.exp(expo[None] * lg[:, None, None]), 0.0)
    fi = i.astype(F32)
    qdec = jnp.broadcast_to(jnp.exp((fi[None, :] + 1.0) * lg[:, None])[:, :, None], (H_RET, tb, DK_RET))
    kdec = jnp.broadcast_to(jnp.exp((tb - 1.0 - fi[None, :]) * lg[:, None])[:, :, None], (H_RET, tb, DK_RET))
    sdec = jnp.broadcast_to(jnp.exp(tb * lg)[:, None, None], (H_RET, 1, DV_RET))
    gr_off = (H_RET * DV_RET) // DV_RET
    return pl.pallas_call(
        _ret_kernel,
        grid=(nb, H_RET, nt),
        in_specs=[pl.BlockSpec((1, tb, DK_RET), lambda b, h, t: (b, t, h)),
                  pl.BlockSpec((1, tb, DK_RET), lambda b, h, t: (b, t, H_RET + h)),
                  pl.BlockSpec((1, tb, DV_RET), lambda b, h, t: (b, t, h)),
                  pl.BlockSpec((1, tb, DV_RET), lambda b, h, t: (b, t, gr_off + h)),
                  pl.BlockSpec((tb, DK_RET), lambda b, h, t: (t, 0)),
                  pl.BlockSpec((tb, DK_RET), lambda b, h, t: (t, 0)),
                  pl.BlockSpec((1, tb, tb), lambda b, h, t: (h, 0, 0)),
                  pl.BlockSpec((1, tb, DK_RET), lambda b, h, t: (h, 0, 0)),
                  pl.BlockSpec((1, tb, DK_RET), lambda b, h, t: (h, 0, 0)),
                  pl.BlockSpec((1, 1, DV_RET), lambda b, h, t: (h, 0, 0)),
                  pl.BlockSpec((1, 1, DK_RET, DV_RET), lambda b, h, t: (b, h, 0, 0)),
                  pl.BlockSpec((1, DV_RET), lambda b, h, t: (0, h))],
        out_specs=[pl.BlockSpec((1, tb, DV_RET), lambda b, h, t: (b, t, h)),
                   pl.BlockSpec((1, 1, DK_RET, DV_RET), lambda b, h, t: (b, h, 0, 0))],
        out_shape=[jax.ShapeDtypeStruct((nb, l, H_RET * DV_RET), BF16),
                   jax.ShapeDtypeStruct((nb, H_RET, DK_RET, DV_RET), F32)],
        scratch_shapes=[pltpu.VMEM((DK_RET, DV_RET), F32)],
        compiler_params=_params(("arbitrary", "arbitrary", "arbitrary"), VMEM_LIMIT),
        name="retention",
    )(zqk, zqk, zb, zb, cos, sin, dmat, qdec, kdec, sdec, s0, g_ret_gn.reshape(1, -1))


def _fox_prompt_kernel(q_ref, k_ref, v_ref, fk_ref, ftok_ref, o_ref, m_scr, l_scr, acc_scr, fq_scr, *, heads, tk):
    g = pl.program_id(1)
    qi = pl.program_id(2)
    tq = q_ref.shape[1]
    ft = ftok_ref[0]
    lane = lax.broadcasted_iota(I32, ft.shape, 1)
    for hh in range(heads):
        m_scr[hh] = jnp.full((tq, 1), -jnp.inf, F32)
        l_scr[hh] = jnp.zeros((tq, 1), F32)
        acc_scr[hh] = jnp.zeros((tq, DH_FOX), F32)
        fq_scr[hh] = jnp.sum(jnp.where(lane == g * heads + hh, ft, 0.0), axis=-1, keepdims=True)

    def step(hh, ki, masked):
        sl = slice(hh * DH_FOX, (hh + 1) * DH_FOX)
        rows = pl.ds(pl.multiple_of(ki * tk, tk), tk)
        s = _nt_dot(q_ref[0, :, sl], k_ref[0, rows, sl])
        s = s + fq_scr[hh] - fk_ref[0, hh, pl.ds(ki, 1), :]
        if masked:
            row = lax.broadcasted_iota(I32, (tq, tk), 0)
            col = lax.broadcasted_iota(I32, (tq, tk), 1)
            s = jnp.where(col <= row, s, -jnp.inf)
        m_prev = m_scr[hh]
        m_new = jnp.maximum(m_prev, jnp.max(s, axis=-1, keepdims=True))
        alpha = jnp.exp(m_prev - m_new)
        p = jnp.exp(s - m_new)
        l_scr[hh] = alpha * l_scr[hh] + jnp.sum(p, axis=-1, keepdims=True)
        acc_scr[hh] = alpha * acc_scr[hh] + jnp.dot(p.astype(BF16), v_ref[0, rows, sl],
                                                    preferred_element_type=F32)
        m_scr[hh] = m_new

    def body(ki, carry):
        for hh in range(heads):
            step(hh, ki, False)
        return carry

    lax.fori_loop(0, qi, body, 0)
    for hh in range(heads):
        step(hh, qi, True)
        o_ref[0, :, hh * DH_FOX:(hh + 1) * DH_FOX] = (acc_scr[hh] / l_scr[hh]).astype(BF16)


def _fox_prompt(zb, kb, vb, f_rows, f_tok, tq, heads=2):
    nb, s, hd = kb.shape
    nq = s // tq
    w = heads * DH_FOX
    q_off = (2 * hd) // w
    kv_spec = pl.BlockSpec((1, s, w), lambda b, g, qi: (b, 0, g))
    return pl.pallas_call(
        functools.partial(_fox_prompt_kernel, heads=heads, tk=tq),
        grid=(nb, H_FOX // heads, nq),
        in_specs=[pl.BlockSpec((1, tq, w), lambda b, g, qi: (b, qi, q_off + g)),
                  kv_spec, kv_spec,
                  pl.BlockSpec((1, heads, nq, tq), lambda b, g, qi: (b, g, 0, 0)),
                  pl.BlockSpec((1, tq, H_FOX), lambda b, g, qi: (b, qi, 0))],
        out_specs=pl.BlockSpec((1, tq, w), lambda b, g, qi: (b, qi, g)),
        out_shape=jax.ShapeDtypeStruct((nb, s, hd), BF16),
        scratch_shapes=[pltpu.VMEM((heads, tq, 1), F32), pltpu.VMEM((heads, tq, 1), F32),
                        pltpu.VMEM((heads, tq, DH_FOX), F32), pltpu.VMEM((heads, tq, 1), F32)],
        compiler_params=_params(("arbitrary",) * 3, VMEM_LIMIT),
        name="fox_prompt",
    )(zb, kb, vb, f_rows.reshape(nb, H_FOX, nq, tq), f_tok)


def _fox_sample_kernel(q_ref, ck_ref, cv_ref, kn_ref, vn_ref, fk_ref, ftok_ref, o_ref, *, heads):
    g = pl.program_id(1)
    l = q_ref.shape[1]
    p_len = ck_ref.shape[2]
    ft = ftok_ref[0]
    lane = lax.broadcasted_iota(I32, ft.shape, 1)
    row = lax.broadcasted_iota(I32, (l, l), 0)
    col = lax.broadcasted_iota(I32, (l, l), 1)
    for hh in range(heads):
        hg = g * heads + hh
        sl = slice(hh * DH_FOX, (hh + 1) * DH_FOX)
        fk = fk_ref[0, pl.ds(hg, 1), :]
        fq = jnp.sum(jnp.where(lane == hg, ft, 0.0), axis=-1, keepdims=True)
        qh = q_ref[0, :, sl]
        s1 = _nt_dot(qh, ck_ref[0, 0, :, hh, :].astype(BF16)) + fq - fk[:, :p_len]
        s2 = _nt_dot(qh, kn_ref[0, :, sl]) + fq - fk[:, p_len:p_len + l]
        s2 = jnp.where(col <= row, s2, -jnp.inf)
        m = jnp.maximum(jnp.max(s1, axis=-1, keepdims=True), jnp.max(s2, axis=-1, keepdims=True))
        p1 = jnp.exp(s1 - m)
        p2 = jnp.exp(s2 - m)
        den = jnp.sum(p1, axis=-1, keepdims=True) + jnp.sum(p2, axis=-1, keepdims=True)
        o = jnp.dot(p1.astype(BF16), cv_ref[0, 0, :, hh, :].astype(BF16), preferred_element_type=F32)
        o = o + jnp.dot(p2.astype(BF16), vn_ref[0, :, sl], preferred_element_type=F32)
        o_ref[0, :, sl] = (o / den).astype(BF16)


def _fox_sample(zb, kb, vb, cache_k, cache_v, f_rows, f_tok, heads=8):
    nb, l, hd = kb.shape
    p_len = cache_k.shape[2]
    sp = f_rows.shape[-1]
    w = heads * DH_FOX
    q_off = (2 * hd) // w
    assert p_len % l == 0
    new_spec = pl.BlockSpec((1, l, w), lambda b, g: (b, 0, g))
    cache_spec = pl.BlockSpec((1, 1, p_len, heads, DH_FOX), lambda b, g: (0, b, 0, g, 0))
    return pl.pallas_call(
        functools.partial(_fox_sample_kernel, heads=heads),
        grid=(nb, H_FOX // heads),
        in_specs=[pl.BlockSpec((1, l, w), lambda b, g: (b, 0, q_off + g)),
                  cache_spec, cache_spec, new_spec, new_spec,
                  pl.BlockSpec((1, H_FOX, sp), lambda b, g: (b, 0, 0)),
                  pl.BlockSpec((1, l, H_FOX), lambda b, g: (b, p_len // l, 0))],
        out_specs=new_spec,
        out_shape=jax.ShapeDtypeStruct((nb, l, hd), BF16),
        compiler_params=_params(("arbitrary", "arbitrary"), VMEM_LIMIT),
        name="fox_sample",
    )(zb, cache_k, cache_v, kb, vb, f_rows, f_tok)


def _mix1_kernel(or_ref, of_ref, w1_ref, w2_ref, gr_ref, gf_ref, o_ref):
    a = jnp.dot(or_ref[...], w1_ref[...], preferred_element_type=F32)
    b = jnp.dot(of_ref[...], w2_ref[...], preferred_element_type=F32)
    gr = _sigmoid(gr_ref[...].astype(F32))
    gf = _sigmoid(gf_ref[...].astype(F32))
    o_ref[...] = (gr * a + gf * b).astype(BF16)


def _mix1(o_r, o_f, w1, w2, zb, tm, tn):
    m, d = o_r.shape
    gr_off = (3 * d) // tn
    gf_off = (4 * d) // tn
    return pl.pallas_call(
        _mix1_kernel,
        grid=(m // tm, d // tn),
        in_specs=[pl.BlockSpec((tm, d), lambda i, j: (i, 0)),
                  pl.BlockSpec((tm, d), lambda i, j: (i, 0)),
                  pl.BlockSpec((d, tn), lambda i, j: (0, j)),
                  pl.BlockSpec((d, tn), lambda i, j: (0, j)),
                  pl.BlockSpec((tm, tn), lambda i, j: (i, gr_off + j)),
                  pl.BlockSpec((tm, tn), lambda i, j: (i, gf_off + j))],
        out_specs=pl.BlockSpec((tm, tn), lambda i, j: (i, j)),
        out_shape=jax.ShapeDtypeStruct((m, d), BF16),
        compiler_params=_params(("arbitrary", "arbitrary"), VMEM_LIMIT),
        name="mix_gates",
    )(o_r, o_f, w1, w2, zb, zb)


def _mix2_kernel(m_ref, w_ref, x_ref, gt_ref, sh_ref, sc_ref, g_ref, wr_ref, br_ref, cnt0_ref,
                 h1_ref, hn_ref, eidx_ref, rank_ref, wts_ref, cnt_ref):
    bb, tl, d = x_ref.shape
    tm = bb * tl
    ne = wr_ref.shape[1]

    @pl.when((pl.program_id(0) == 0) & (pl.program_id(1) == 0))
    def _():
        cnt_ref[...] = cnt0_ref[...]

    mix = jnp.dot(m_ref[...].reshape(tm, d), w_ref[...], preferred_element_type=F32)
    h1 = x_ref[...] + gt_ref[...] * mix.reshape(bb, tl, d)
    h1_ref[...] = h1
    y = h1 * lax.rsqrt(jnp.mean(h1 * h1, axis=-1, keepdims=True) + EPS) * g_ref[...]
    hn = (y * (1.0 + sc_ref[...]) + sh_ref[...]).reshape(tm, d)
    hn_ref[...] = _pack_halves(hn)

    logits = jnp.dot(hn, wr_ref[...], precision=lax.Precision.HIGHEST, preferred_element_type=F32)
    scores = _sigmoid(logits)
    cur = scores + br_ref[...]
    lane = lax.broadcasted_iota(I32, (tm, ne), 1).astype(F32)
    slot = lax.broadcasted_iota(I32, (tm, LANES), 1)
    r = lax.broadcasted_iota(I32, (tm, tm), 0)
    c = lax.broadcasted_iota(I32, (tm, tm), 1)
    lower = (c < r).astype(BF16)
    picks = []
    mask = jnp.zeros((tm, ne), F32)
    for _ in range(TOP_K):
        mx = jnp.max(cur, axis=-1, keepdims=True)
        idx = jnp.min(jnp.where(cur == mx, lane, float(ne)), axis=-1, keepdims=True)
        pick = lane == idx
        picks.append((idx, pick))
        mask = jnp.where(pick, 1.0, mask)
        cur = jnp.where(pick, -jnp.inf, cur)
    rank = jnp.dot(lower, mask.astype(BF16), preferred_element_type=F32) + cnt_ref[0:1, :]
    cnt_ref[...] = cnt_ref[...] + jnp.sum(mask, axis=0, keepdims=True)
    wsum = jnp.sum(mask * scores, axis=-1, keepdims=True)
    eidx = jnp.zeros((tm, LANES), I32)
    rnk = jnp.zeros((tm, LANES), I32)
    wts = jnp.zeros((tm, LANES), F32)
    for kk, (idx, pick) in enumerate(picks):
        sc_k = jnp.sum(jnp.where(pick, scores, 0.0), axis=-1, keepdims=True)
        rk_k = jnp.sum(jnp.where(pick, rank, 0.0), axis=-1, keepdims=True)
        eidx = jnp.where(slot == kk, idx.astype(I32), eidx)
        rnk = jnp.where(slot == kk, rk_k.astype(I32), rnk)
        wts = jnp.where(slot == kk, sc_k / wsum * ROUTE_SCALE, wts)
    eidx_ref[...] = eidx
    rank_ref[...] = rnk
    wts_ref[...] = wts


def _mix2(m3, w_out, x, mod, g2, w_router, b_router, cnt0, bb, tl):
    nb, l, d = x.shape
    ne = w_router.shape[1]
    t = nb * l
    tm = bb * tl
    nj = l // tl
    mod_spec = lambda k: pl.BlockSpec((bb, 1, d), lambda i, j: (i, 0, k))
    tok_spec = lambda w: pl.BlockSpec((tm, w), lambda i, j: (i * nj + j, 0))
    blk3 = pl.BlockSpec((bb, tl, d), lambda i, j: (i, j, 0))
    return pl.pallas_call(
        _mix2_kernel,
        grid=(nb // bb, nj),
        in_specs=[blk3,
                  pl.BlockSpec((d, d), lambda i, j: (0, 0)),
                  blk3,
                  mod_spec(2), mod_spec(3), mod_spec(4),
                  pl.BlockSpec((1, 1, d), lambda i, j: (0, 0, 0)),
                  pl.BlockSpec((d, ne), lambda i, j: (0, 0)),
                  pl.BlockSpec((1, ne), lambda i, j: (0, 0)),
                  pl.BlockSpec((8, ne), lambda i, j: (0, 0))],
        out_specs=[blk3, tok_spec(d // 2), tok_spec(LANES), tok_spec(LANES), tok_spec(LANES),
                   pl.BlockSpec((8, ne), lambda i, j: (0, 0))],
        out_shape=[jax.ShapeDtypeStruct((nb, l, d), F32),
                   jax.ShapeDtypeStruct((t, d // 2), I32),
                   jax.ShapeDtypeStruct((t, LANES), I32),
                   jax.ShapeDtypeStruct((t, LANES), I32),
                   jax.ShapeDtypeStruct((t, LANES), F32),
                   jax.ShapeDtypeStruct((8, ne), F32)],
        compiler_params=_params(("arbitrary", "arbitrary"), VMEM_LIMIT),
        name="outproj_norm2_router",
    )(m3, w_out, x, mod, mod, mod, g2.reshape(1, 1, d), w_router, b_router.reshape(1, ne), cnt0)


def _dispatch_kernel(zs_ref, zn_ref, dest_ref, xp_ref, xs_ref, buf_ref, zero_scr, sem, *, np_blocks):
    i = pl.program_id(0)
    tb = xp_ref.shape[0]

    def scatter(x_ref):
        def row_copy(r, kk):
            d = dest_ref[r * DEST_W + kk]
            return pltpu.make_async_copy(x_ref.at[pl.ds(r, 1), :], buf_ref.at[pl.ds(d, 1), :], sem)

        def issue(r, carry):
            for kk in range(TOP_K):
                row_copy(r, kk).start()
            return carry

        def drain(r, carry):
            for kk in range(TOP_K):
                row_copy(r, kk).wait()
            return carry

        lax.fori_loop(0, tb, issue, 0)
        lax.fori_loop(0, tb, drain, 0)

    @pl.when(i < np_blocks)
    def _():
        scatter(xp_ref)

    @pl.when(i >= np_blocks)
    def _():
        scatter(xs_ref)

    @pl.when(i == pl.num_programs(0) - 1)
    def _():
        zero_scr[...] = jnp.zeros_like(zero_scr)

        def per_expert(e, carry):
            start = zs_ref[e]

            def zero_copy(r):
                return pltpu.make_async_copy(zero_scr.at[pl.ds(0, 1), :], buf_ref.at[pl.ds(start + r, 1), :], sem)

            def issue(r, c):
                zero_copy(r).start()
                return c

            def drain(r, c):
                zero_copy(r).wait()
                return c

            lax.fori_loop(0, zn_ref[e], issue, 0)
            lax.fori_loop(0, zn_ref[e], drain, 0)
            return carry

        lax.fori_loop(0, zs_ref.shape[0], per_expert, 0)


def _dispatch(xp, xs, dest, zero_start, zero_n, rows, tb):
    tp, w = xp.shape
    ts = xs.shape[0]
    npb = tp // tb
    grid_spec = pltpu.PrefetchScalarGridSpec(
        num_scalar_prefetch=2,
        grid=(npb + ts // tb,),
        in_specs=[pl.BlockSpec((tb * DEST_W,), lambda i, zs, zn: (i,), memory_space=pltpu.SMEM),
                  pl.BlockSpec((tb, w), lambda i, zs, zn: (jnp.minimum(i, npb - 1), 0)),
                  pl.BlockSpec((tb, w), lambda i, zs, zn: (jnp.maximum(i - npb, 0), 0))],
        out_specs=pl.BlockSpec(memory_space=pl.ANY),
        scratch_shapes=[pltpu.VMEM((8, w), I32), pltpu.SemaphoreType.DMA(())],
    )
    return pl.pallas_call(
        functools.partial(_dispatch_kernel, np_blocks=npb),
        grid_spec=grid_spec,
        out_shape=jax.ShapeDtypeStruct((rows, w), I32),
        compiler_params=_params(("arbitrary",), VMEM_LIMIT),
        name="moe_dispatch",
    )(zero_start, zero_n, dest, xp, xs)


def _expert_kernel(be_ref, nv_ref, nu_ref, x_ref, wg_ref, wu_ref, wd_ref, wgt_ref, wut_ref, wdt_ref, o_ref,
                   acc_scr, wgb_scr, wub_scr, wdb_scr, *, n_main):
    del be_ref, nu_ref
    i = pl.program_id(0)
    j = pl.program_id(1)
    half = x_ref.shape[1]
    nv = nv_ref[i]
    n_big = nv // 2
    big = 2 * SUB_ROWS

    def for_chunks(fn):
        def body(c, carry):
            fn(pl.multiple_of(c * big, big), big)
            return carry

        lax.fori_loop(0, n_big, body, 0)

        @pl.when(nv % 2 == 1)
        def _():
            fn(pl.multiple_of(n_big * big, SUB_ROWS), SUB_ROWS)

    @pl.when(nv > 0)
    def _():
        @pl.when(j == 0)
        def _():
            def init(r0, rows):
                acc_scr[pl.ds(r0, rows), :] = jnp.zeros((rows, 2 * half), F32)

            for_chunks(init)

        def ffn(wg, wu, wd, width):
            wgb_scr[:, :width] = wg[0].astype(BF16)
            wub_scr[:, :width] = wu[0].astype(BF16)
            wdb_scr[:width, :] = wd[0].astype(BF16)

            def chunk(r0, rows):
                hi, lo = _unpack_halves(x_ref[pl.ds(r0, rows), :])
                xb = jnp.concatenate([hi.astype(BF16), lo.astype(BF16)], axis=1)
                gate = jnp.dot(xb, wgb_scr[:, :width], preferred_element_type=F32)
                up = jnp.dot(xb, wub_scr[:, :width], preferred_element_type=F32)
                hid = (gate * _sigmoid(gate) * up).astype(BF16)
                acc_scr[pl.ds(r0, rows), :] += jnp.dot(hid, wdb_scr[:width, :], preferred_element_type=F32)

            for_chunks(chunk)

        @pl.when(j < n_main)
        def _():
            ffn(wg_ref, wu_ref, wd_ref, wg_ref.shape[2])

        @pl.when(j == n_main)
        def _():
            ffn(wgt_ref, wut_ref, wdt_ref, wgt_ref.shape[2])
            o_ref[...] = jnp.zeros_like(o_ref)

            def fin(r0, rows):
                o_ref[pl.ds(r0, rows), :] = _pack_halves(acc_scr[pl.ds(r0, rows), :])

            for_chunks(fin)

    @pl.when((nv == 0) & (j == n_main))
    def _():
        o_ref[...] = jnp.zeros_like(o_ref)


def _expert_ffn(xp, wg, wu, wd, block_expert, n_valid, n_used, rb):
    rows, half = xp.shape
    d = 2 * half
    f = wg.shape[2]
    n_main = f // F_TILE
    tail = f - n_main * F_TILE
    assert tail > 0 and tail % LANES == 0 and (n_main * F_TILE) % tail == 0
    tail_blk = (n_main * F_TILE) // tail
    nblk = rows // rb

    def main_col(i, j, be, nv, nu):
        return (be[i], 0, jnp.minimum(j, n_main - 1))

    def main_row(i, j, be, nv, nu):
        return (be[i], jnp.minimum(j, n_main - 1), 0)

    grid_spec = pltpu.PrefetchScalarGridSpec(
        num_scalar_prefetch=3,
        grid=(nblk, n_main + 1),
        in_specs=[pl.BlockSpec((rb, half), lambda i, j, be, nv, nu: (jnp.minimum(i, nu[0] - 1), 0)),
                  pl.BlockSpec((1, d, F_TILE), main_col),
                  pl.BlockSpec((1, d, F_TILE), main_col),
                  pl.BlockSpec((1, F_TILE, d), main_row),
                  pl.BlockSpec((1, d, tail), lambda i, j, be, nv, nu: (be[i], 0, tail_blk)),
                  pl.BlockSpec((1, d, tail), lambda i, j, be, nv, nu: (be[i], 0, tail_blk)),
                  pl.BlockSpec((1, tail, d), lambda i, j, be, nv, nu: (be[i], tail_blk, 0))],
        out_specs=pl.BlockSpec((rb, half), lambda i, j, be, nv, nu: (i, 0)),
        scratch_shapes=[pltpu.VMEM((rb, d), F32), pltpu.VMEM((d, F_TILE), BF16),
                        pltpu.VMEM((d, F_TILE), BF16), pltpu.VMEM((F_TILE, d), BF16)],
    )
    return pl.pallas_call(
        functools.partial(_expert_kernel, n_main=n_main),
        grid_spec=grid_spec,
        out_shape=jax.ShapeDtypeStruct((rows, half), I32),
        compiler_params=_params(("arbitrary", "arbitrary"), VMEM_LIMIT),
        name="expert_ffn",
    )(block_expert, n_valid, n_used, xp, wg, wu, wd, wg, wu, wd)


def _combine_kernel(dcur_ref, dnext_ref, wts_ref, sh_ref, h1_ref, gt_ref, g_ref, eo_ref, y_ref,
                    gbuf, sems):
    bb, tl, d = h1_ref.shape
    tb = bb * tl
    half = d // 2
    nj = pl.num_programs(1)
    step = pl.program_id(0) * nj + pl.program_id(1)
    nsteps = pl.num_programs(0) * nj
    slot = step % 2

    def row_copy(dref, sl, r, kk):
        dd = dref[r * DEST_W + kk]
        return pltpu.make_async_copy(eo_ref.at[pl.ds(dd, 1), :], gbuf.at[sl, kk, pl.ds(r, 1), :],
                                     sems.at[sl])

    def issue(dref, sl):
        def body(r, carry):
            for kk in range(TOP_K):
                row_copy(dref, sl, r, kk).start()
            return carry
        lax.fori_loop(0, tb, body, 0)

    @pl.when(step == 0)
    def _():
        issue(dcur_ref, 0)

    @pl.when(step + 1 < nsteps)
    def _():
        issue(dnext_ref, 1 - slot)

    def drain(r, carry):
        for kk in range(TOP_K):
            row_copy(dcur_ref, slot, r, kk).wait()
        return carry
    lax.fori_loop(0, tb, drain, 0)

    wts = wts_ref[...]
    s_hi, s_lo = _unpack_halves(sh_ref[...])
    for kk in range(TOP_K):
        hi, lo = _unpack_halves(gbuf[slot, kk])
        wk = wts[:, kk:kk + 1]
        s_hi = s_hi + wk * hi
        s_lo = s_lo + wk * lo
    h2_hi = h1_ref[:, :, :half] + gt_ref[:, :, :half] * s_hi.reshape(bb, tl, half)
    h2_lo = h1_ref[:, :, half:] + gt_ref[:, :, half:] * s_lo.reshape(bb, tl, half)
    ms = (jnp.sum(h2_hi * h2_hi, axis=-1, keepdims=True)
          + jnp.sum(h2_lo * h2_lo, axis=-1, keepdims=True)) / d
    inv = lax.rsqrt(ms + EPS)
    y_ref[:, :, :half] = h2_hi * inv * g_ref[:, :, :half]
    y_ref[:, :, half:] = h2_lo * inv * g_ref[:, :, half:]


def _combine(dest, wts, shared, h1, mod, g_final, eo, bb, tl):
    nb, l, d = h1.shape
    tb = bb * tl
    nj = l // tl
    nsteps = (nb // bb) * nj
    lin = lambda i, j: i * nj + j
    return pl.pallas_call(
        _combine_kernel,
        grid=(nb // bb, nj),
        in_specs=[pl.BlockSpec((tb * DEST_W,), lambda i, j: (lin(i, j),), memory_space=pltpu.SMEM),
                  pl.BlockSpec((tb * DEST_W,), lambda i, j: (jnp.minimum(lin(i, j) + 1, nsteps - 1),),
                               memory_space=pltpu.SMEM),
                  pl.BlockSpec((tb, LANES), lambda i, j: (lin(i, j), 0)),
                  pl.BlockSpec((tb, d // 2), lambda i, j: (lin(i, j), 0)),
                  pl.BlockSpec((bb, tl, d), lambda i, j: (i, j, 0)),
                  pl.BlockSpec((bb, 1, d), lambda i, j: (i, 0, 5)),
                  pl.BlockSpec((1, 1, d), lambda i, j: (0, 0, 0)),
                  pl.BlockSpec(memory_space=pl.ANY)],
        out_specs=pl.BlockSpec((bb, tl, d), lambda i, j: (i, j, 0)),
        out_shape=jax.ShapeDtypeStruct((nb, l, d), F32),
        scratch_shapes=[pltpu.VMEM((2, TOP_K, tb, d // 2), I32), pltpu.SemaphoreType.DMA((2,))],
        compiler_params=_params(("arbitrary", "arbitrary"), VMEM_LIMIT),
        name="moe_combine",
    )(dest, dest, wts, shared, h1, mod, g_final.reshape(1, 1, d), eo)


def _prep_w_in(w_in):
    d = w_in.shape[0]
    qr_w, vr_w, qf_w = H_RET * DK_RET, H_RET * DV_RET, H_FOX * DH_FOX
    sizes = (qr_w, qr_w, vr_w, vr_w, qf_w, qf_w, qf_w, H_FOX, d, d)
    offs = [0]
    for s in sizes:
        offs.append(offs[-1] + s)
    part = lambda n: w_in[:, offs[n]:offs[n + 1]]
    w_qk = jnp.concatenate([part(0), part(1) * (DK_RET ** -0.5)], axis=1).astype(BF16)
    w_b = jnp.concatenate([part(2), part(3), part(4) * (DH_FOX ** -0.5), part(8), part(9)], axis=1).astype(BF16)
    w_kf = part(5).astype(BF16)
    w_vf = part(6).astype(BF16)
    w_fl = jnp.pad(part(7), ((0, 0), (0, LANES - H_FOX))).astype(BF16)
    return w_qk, w_b, w_kf, w_vf, w_fl


def _trunk_front(x, mod, lw, bb, tl, tm, fox, s0, pos0, ret_tb):
    nb, l, d = x.shape
    t = nb * l
    hn, logf = _norm1(x, lw["g_norm1"], mod, lw["w_fl"], lw["b_fl"], bb, tl)
    hn2d = hn.reshape(t, d)
    zqk = _matmul(hn2d, lw["w_qk"], F32, tm, 512, "proj_qk_ret")
    kf, kb = _matmul_heads(hn2d, lw["w_kf"], nb, l, bb, tl, "proj_k_fox")
    vf, vb = _matmul_heads(hn2d, lw["w_vf"], nb, l, bb, tl, "proj_v_fox")
    zb = _matmul(hn2d, lw["w_b"], BF16, tm, 512, "proj_rest")
    zb3 = zb.reshape(nb, l, -1)
    o_r, s_new = _retention(zqk.reshape(nb, l, -1), zb3, s0, lw["g_ret_gn"], pos0, ret_tb)
    o_f = fox(zb3, kb.reshape(nb, l, -1), vb.reshape(nb, l, -1), logf)
    m = _mix1(o_r.reshape(t, d), o_f.reshape(t, d), lw["w_ret_out"], lw["w_fox_out"], zb, tm, 512)
    return m.reshape(nb, l, d), (kf, vf, logf, s_new)


def kernel(x_prompt, x_sample, c_prompt, c_sample, cache_fox_k, cache_fox_v, cache_fox_logf, state_ret, w_ada, b_ada, g_norm1, w_in, b_forget, g_ret_gn, w_ret_out, w_fox_out, w_out, g_norm2, w_router, b_router, w_exp_gate, w_exp_up, w_exp_down, w_sh_gate, w_sh_up, w_sh_down, g_final):
    depth = w_in.shape[0]
    assert depth == 1
    nbp, s, d = x_prompt.shape
    nbs, ls, _ = x_sample.shape
    p_len = cache_fox_k.shape[2]
    ne = w_router.shape[-1]
    tp, ts = nbp * s, nbs * ls

    nb_all = nbp + nbs
    nb_pad = -(-nb_all // 8) * 8
    c_all = jnp.pad(jnp.concatenate([c_prompt, c_sample], axis=0), ((0, nb_pad - nb_all), (0, 0)))
    mod_all = _adaln(c_all, w_ada[0], b_ada[0])
    mod_p = mod_all[:nbp].reshape(nbp, 1, 6 * d)
    mod_s = mod_all[nbp:nb_all].reshape(nbs, 1, 6 * d)

    w_qk, w_b, w_kf, w_vf, w_fl = _prep_w_in(w_in[0])
    lw = dict(g_norm1=g_norm1[0], w_fl=w_fl,
              b_fl=jnp.pad(b_forget[0], (0, LANES - H_FOX)).reshape(1, LANES),
              w_qk=w_qk, w_b=w_b, w_kf=w_kf, w_vf=w_vf, g_ret_gn=g_ret_gn[0],
              w_ret_out=w_ret_out[0].astype(BF16), w_fox_out=w_fox_out[0].astype(BF16))
    w_out_b = w_out[0].astype(BF16)

    tl_p = min(s, 512)
    tm_p = min(tp, 1024)
    bb_s = max(1, min(nbs, 512 // ls))
    tm_s = min(ts, 1024)

    def fox_p(zb3, kb3, vb3, logf):
        f_rows = _cumsum_rows(logf.swapaxes(1, 2), min(s, 512))
        return _fox_prompt(zb3, kb3, vb3, f_rows, f_rows.swapaxes(1, 2), min(s, 512))

    def fox_s(zb3, kb3, vb3, logf):
        full = jnp.concatenate([cache_fox_logf[0].astype(F32), logf], axis=1)
        sp = -(-(p_len + ls) // LANES) * LANES
        full = jnp.pad(full, ((0, 0), (0, sp - p_len - ls), (0, 0)))
        f_rows = _cumsum_rows(full.swapaxes(1, 2), sp)
        return _fox_sample(zb3, kb3, vb3, cache_fox_k, cache_fox_v, f_rows, f_rows.swapaxes(1, 2))

    s0_p = jnp.zeros((nbp, H_RET, DK_RET, DV_RET), F32)
    m_p, st_p = _trunk_front(x_prompt, mod_p, lw, 1, tl_p, tm_p, fox_p, s0_p, 0, min(s, 256))
    m_s, st_s = _trunk_front(x_sample, mod_s, lw, bb_s, ls, tm_s, fox_s, state_ret[0], p_len, ls)

    cnt0 = jnp.zeros((8, ne), F32)
    tl2_p = min(s, 256)
    bb2_s = max(1, min(nbs, 256 // ls))
    h1_p, hn_p, e_p, r_p, wt_p, cnt_p = _mix2(m_p, w_out_b, x_prompt, mod_p, g_norm2[0], w_router[0],
                                              b_router[0], cnt0, 1, tl2_p)
    h1_s, hn_s, e_s, r_s, wt_s, cnt_s = _mix2(m_s, w_out_b, x_sample, mod_s, g_norm2[0], w_router[0],
                                              b_router[0], cnt_p, bb2_s, ls)

    counts = cnt_s[0].astype(I32)
    rblk = EXPERT_ROWS
    padded = (counts + rblk - 1) // rblk * rblk
    pad_end = jnp.cumsum(padded)
    pad_start = pad_end - padded
    t_all = tp + ts
    nblk = (t_all * TOP_K + ne * (rblk - 1) + rblk - 1) // rblk
    rows = nblk * rblk
    blk_start = jnp.arange(nblk, dtype=I32) * rblk
    block_expert = jnp.minimum(jnp.searchsorted(pad_end, blk_start, side="right"), ne - 1).astype(I32)
    n_used = (pad_end[-1] // rblk).astype(I32).reshape(1)
    valid_rows = jnp.clip(pad_start[block_expert] + counts[block_expert] - blk_start, 0, rblk)
    valid_rows = jnp.where(blk_start < pad_end[-1], valid_rows, 0)
    n_valid = ((valid_rows + SUB_ROWS - 1) // SUB_ROWS).astype(I32)
    zero_start = (pad_start + counts).astype(I32)
    zero_n = ((-counts) % SUB_ROWS).astype(I32)
    e_all = jnp.concatenate([e_p, e_s], axis=0)[:, :DEST_W]
    r_all = jnp.concatenate([r_p, r_s], axis=0)[:, :DEST_W]
    dest = (pad_start[e_all] + r_all).astype(I32).reshape(-1)

    buf = _dispatch(hn_p, hn_s, dest, zero_start, zero_n, rows, min(tp, ts, 256))
    eo = _expert_ffn(buf, w_exp_gate[0], w_exp_up[0], w_exp_down[0], block_expert, n_valid, n_used, rblk)

    def shared_ffn(hn):
        rb = min(hn.shape[0], rblk)
        nb_ = hn.shape[0] // rb
        return _expert_ffn(hn, w_sh_gate, w_sh_up, w_sh_down, jnp.zeros((nb_,), I32),
                           jnp.full((nb_,), rb // SUB_ROWS, I32), jnp.full((1,), nb_, I32), rb)

    tlc_p = min(s, 128)
    bbc_s = max(1, min(nbs, 128 // ls))
    y_p = _combine(dest[:tp * DEST_W], wt_p, shared_ffn(hn_p), h1_p, mod_p, g_final, eo, 1, tlc_p)
    y_s = _combine(dest[tp * DEST_W:], wt_s, shared_ffn(hn_s), h1_s, mod_s, g_final, eo, bbc_s, ls)

    kf_p, vf_p, logf_p, s_p = st_p
    kf_s, vf_s, logf_s, s_s = st_s
    return (y_p, y_s, kf_p[None], vf_p[None], logf_p[None], s_p[None],
            kf_s[None], vf_s[None], logf_s[None], s_s[None])
```

```python
import functools
import math

import jax
import jax.numpy as jnp
from jax import lax
from jax.experimental import pallas as pl
from jax.experimental.pallas import tpu as pltpu

F32 = jnp.float32
BF16 = jnp.bfloat16
I32 = jnp.int32

CHUNK = 64
H_RET = 8
DK_RET = 128
DV_RET = 256
H_FOX = 16
DH_FOX = 128
TOP_K = 6
ROUTE_SCALE = 2.5
ROPE_BASE = 10000.0
EPS = 1e-6
LOG2E = math.log2(math.e)

LANES = 128
DEST_W = 8
EXPERT_ROWS = 1024
SUB_ROWS = 256
F_TILE = 256
VMEM_LIMIT = 56 * 1024 * 1024


def _params(sem, vmem=None):
    return pltpu.CompilerParams(dimension_semantics=sem, vmem_limit_bytes=vmem)


def _sigmoid(x):
    return 1.0 / (1.0 + jnp.exp(-x))


def _nt_dot(a, b):
    return lax.dot_general(a, b, (((1,), (1,)), ((), ())), preferred_element_type=F32)


def _pack_halves(x):
    n = x.shape[-1] // 2
    hi = pltpu.bitcast(x[:, :n].astype(BF16).astype(F32), I32)
    lo = pltpu.bitcast(x[:, n:].astype(BF16).astype(F32), I32)
    return hi | lax.shift_right_logical(lo, jnp.int32(16))


def _unpack_halves(u):
    hi = pltpu.bitcast(u & jnp.int32(-65536), F32)
    lo = pltpu.bitcast(lax.shift_left(u, jnp.int32(16)), F32)
    return hi, lo


def _ada_kernel(c_ref, w_ref, b_ref, o_ref):
    c = c_ref[...]
    s = (c * _sigmoid(c)).astype(BF16)
    o_ref[...] = jnp.dot(s, w_ref[...].astype(BF16), preferred_element_type=F32) + b_ref[...]


def _adaln(c, w_ada, b_ada):
    nb, d = c.shape
    n = w_ada.shape[1]
    tn = 1024
    return pl.pallas_call(
        _ada_kernel,
        grid=(n // tn,),
        in_specs=[pl.BlockSpec((nb, d), lambda j: (0, 0)),
                  pl.BlockSpec((d, tn), lambda j: (0, j)),
                  pl.BlockSpec((1, tn), lambda j: (0, j))],
        out_specs=pl.BlockSpec((nb, tn), lambda j: (0, j)),
        out_shape=jax.ShapeDtypeStruct((nb, n), F32),
        compiler_params=_params(("arbitrary",), VMEM_LIMIT),
        name="adaln",
    )(c, w_ada, b_ada.reshape(1, n))


def _norm1_kernel(x_ref, g_ref, sh_ref, sc_ref, wf_ref, bf_ref, hn_ref, logf_ref):
    bb, tl, d = x_ref.shape
    x = x_ref[...]
    y = x * lax.rsqrt(jnp.mean(x * x, axis=-1, keepdims=True) + EPS) * g_ref[...]
    hn = (y * (1.0 + sc_ref[...]) + sh_ref[...]).astype(BF16)
    hn_ref[...] = hn
    z = jnp.dot(hn.reshape(bb * tl, d), wf_ref[...], preferred_element_type=F32) + bf_ref[...]
    logf = jnp.minimum(z, 0.0) - jnp.log(1.0 + jnp.exp(-jnp.abs(z)))
    logf_ref[...] = logf[:, :H_FOX].reshape(bb, tl, H_FOX)


def _norm1(x, g, mod, wf, bfg, bb, tl):
    nb, l, d = x.shape
    mod_spec = lambda k: pl.BlockSpec((bb, 1, d), lambda i, j: (i, 0, k))
    return pl.pallas_call(
        _norm1_kernel,
        grid=(nb // bb, l // tl),
        in_specs=[pl.BlockSpec((bb, tl, d), lambda i, j: (i, j, 0)),
                  pl.BlockSpec((1, 1, d), lambda i, j: (0, 0, 0)),
                  mod_spec(0), mod_spec(1),
                  pl.BlockSpec((d, LANES), lambda i, j: (0, 0)),
                  pl.BlockSpec((1, LANES), lambda i, j: (0, 0))],
        out_specs=[pl.BlockSpec((bb, tl, d), lambda i, j: (i, j, 0)),
                   pl.BlockSpec((bb, tl, H_FOX), lambda i, j: (i, j, 0))],
        out_shape=[jax.ShapeDtypeStruct((nb, l, d), BF16),
                   jax.ShapeDtypeStruct((nb, l, H_FOX), F32)],
        compiler_params=_params(("arbitrary", "arbitrary"), VMEM_LIMIT),
        name="norm1",
    )(x, g.reshape(1, 1, d), mod, mod, wf, bfg)


def _mm_kernel(a_ref, w_ref, o_ref):
    o_ref[...] = jnp.dot(a_ref[...], w_ref[...], preferred_element_type=F32).astype(o_ref.dtype)


def _matmul(a, w, out_dtype, tm, tn, name):
    m, k = a.shape
    n = w.shape[1]
    return pl.pallas_call(
        _mm_kernel,
        grid=(m // tm, n // tn),
        in_specs=[pl.BlockSpec((tm, k), lambda i, j: (i, 0)),
                  pl.BlockSpec((k, tn), lambda i, j: (0, j))],
        out_specs=pl.BlockSpec((tm, tn), lambda i, j: (i, j)),
        out_shape=jax.ShapeDtypeStruct((m, n), out_dtype),
        compiler_params=_params(("arbitrary", "arbitrary"), VMEM_LIMIT),
        name=name,
    )(a, w)


def _mm_heads_kernel(a_ref, w_ref, o_ref, ob_ref):
    bb, tl, nh, dh = o_ref.shape
    res = jnp.dot(a_ref[...], w_ref[...], preferred_element_type=F32)
    ob_ref[...] = res.astype(BF16)
    for hh in range(nh):
        o_ref[:, :, hh, :] = res[:, hh * dh:(hh + 1) * dh].reshape(bb, tl, dh)


def _matmul_heads(a, w, nb, l, bb, tl, name):
    m, k = a.shape
    n = w.shape[1]
    nj = l // tl
    return pl.pallas_call(
        _mm_heads_kernel,
        grid=(nb // bb, nj),
        in_specs=[pl.BlockSpec((bb * tl, k), lambda i, j: (i * nj + j, 0)),
                  pl.BlockSpec((k, n), lambda i, j: (0, 0))],
        out_specs=[pl.BlockSpec((bb, tl, H_FOX, DH_FOX), lambda i, j: (i, j, 0, 0)),
                   pl.BlockSpec((bb * tl, n), lambda i, j: (i * nj + j, 0))],
        out_shape=[jax.ShapeDtypeStruct((nb, l, H_FOX, DH_FOX), F32),
                   jax.ShapeDtypeStruct((m, n), BF16)],
        compiler_params=_params(("arbitrary", "arbitrary"), VMEM_LIMIT),
        name=name,
    )(a, w)


def _cumsum_kernel(x_ref, o_ref, carry_ref, *, scale):
    tl = x_ref.shape[-1]

    @pl.when(pl.program_id(1) == 0)
    def _():
        carry_ref[...] = jnp.zeros_like(carry_ref)

    r = lax.broadcasted_iota(I32, (tl, tl), 0)
    c = lax.broadcasted_iota(I32, (tl, tl), 1)
    tri = (r <= c).astype(F32)
    y = jnp.dot(x_ref[0], tri, precision=lax.Precision.HIGHEST, preferred_element_type=F32)
    y = y + carry_ref[:, 0:1]
    o_ref[0] = y * scale
    carry_ref[...] = jnp.broadcast_to(y[:, tl - 1:tl], carry_ref.shape)


def _cumsum_rows(x, tl, scale):
    nb, h, s = x.shape
    return pl.pallas_call(
        functools.partial(_cumsum_kernel, scale=scale),
        grid=(nb, s // tl),
        in_specs=[pl.BlockSpec((1, h, tl), lambda b, t: (b, 0, t))],
        out_specs=pl.BlockSpec((1, h, tl), lambda b, t: (b, 0, t)),
        out_shape=jax.ShapeDtypeStruct((nb, h, s), F32),
        scratch_shapes=[pltpu.VMEM((h, LANES), F32)],
        compiler_params=_params(("arbitrary", "arbitrary"), VMEM_LIMIT),
        name="cumsum_logf",
    )(x)


def _ret_kernel(q_ref, k_ref, v_ref, g_ref, cos_ref, sin_ref, dmat_ref, qdec_ref, kdec_ref, sdec_ref,
                s0_ref, gn_ref, o_ref, sout_ref, s_scr, *, heads):
    t = pl.program_id(2)
    hs = range(heads)

    @pl.when(t == 0)
    def _():
        s_scr[...] = s0_ref[0]

    cos = cos_ref[...]
    sin = sin_ref[...]
    half = DK_RET // 2
    ks = [slice(hh * DK_RET, (hh + 1) * DK_RET) for hh in hs]
    vs = [slice(hh * DV_RET, (hh + 1) * DV_RET) for hh in hs]
    q = [q_ref[0, :, ks[hh]] for hh in hs]
    k = [k_ref[0, :, ks[hh]] for hh in hs]
    qr = [q[hh] * cos + pltpu.roll(q[hh], half, 1) * sin for hh in hs]
    kr = [k[hh] * cos + pltpu.roll(k[hh], half, 1) * sin for hh in hs]
    vb = [v_ref[0, :, vs[hh]] for hh in hs]
    state = [s_scr[hh] for hh in hs]
    s = [_nt_dot(qr[hh].astype(BF16), kr[hh].astype(BF16)) * dmat_ref[hh] for hh in hs]
    o = [jnp.dot(s[hh].astype(BF16), vb[hh], preferred_element_type=F32) for hh in hs]
    o = [o[hh] + jnp.dot((qr[hh] * qdec_ref[hh]).astype(BF16), state[hh].astype(BF16),
                         preferred_element_type=F32) for hh in hs]
    kd_t = [(kr[hh] * kdec_ref[hh]).T.astype(BF16) for hh in hs]
    s_new = [state[hh] * sdec_ref[hh] + jnp.dot(kd_t[hh], vb[hh], preferred_element_type=F32) for hh in hs]
    for hh in hs:
        s_scr[hh] = s_new[hh]

    @pl.when(t == pl.num_programs(2) - 1)
    def _():
        sout_ref[0] = s_scr[...]

    mu = [jnp.mean(o[hh], axis=-1, keepdims=True) for hh in hs]
    oc = [o[hh] - mu[hh] for hh in hs]
    var = [jnp.mean(oc[hh] * oc[hh], axis=-1, keepdims=True) for hh in hs]
    on = [oc[hh] * lax.rsqrt(var[hh] + EPS) * gn_ref[:, vs[hh]] for hh in hs]
    g = [g_ref[0, :, vs[hh]].astype(F32) for hh in hs]
    for hh in hs:
        o_ref[0, :, vs[hh]] = (on[hh] * (g[hh] * _sigmoid(g[hh]))).astype(BF16)


def _retention(zqk, zb, s0, g_ret_gn, pos0, tb, heads=4):
    nb, l, _ = zqk.shape
    nt = l // tb
    half = DK_RET // 2
    inv = 1.0 / (ROPE_BASE ** (jnp.arange(half, dtype=F32) * 2.0 / DK_RET))
    ang = (pos0 + jnp.arange(l)).astype(F32)[:, None] * inv[None, :]
    cos = jnp.concatenate([jnp.cos(ang), jnp.cos(ang)], axis=-1)
    sin = jnp.concatenate([-jnp.sin(ang), jnp.sin(ang)], axis=-1)
    lg = jnp.log1p(-jnp.exp2(-5.0 - jnp.arange(H_RET, dtype=F32)))
    i = jnp.arange(tb)
    d = (i[:, None] - i[None, :]).astype(F32)
    ci, cj = i[:, None] // CHUNK, i[None, :] // CHUNK
    expo = jnp.where(ci == cj, jnp.abs(d), d)
    dmat = jnp.where((cj <= ci)[None], jnp.exp(expo[None] * lg[:, None, None]), 0.0)
    fi = i.astype(F32)
    qdec = jnp.broadcast_to(jnp.exp((fi[None, :] + 1.0) * lg[:, None])[:, :, None], (H_RET, tb, DK_RET))
    kdec = jnp.broadcast_to(jnp.exp((tb - 1.0 - fi[None, :]) * lg[:, None])[:, :, None], (H_RET, tb, DK_RET))
    sdec = jnp.broadcast_to(jnp.exp(tb * lg)[:, None, None], (H_RET, 1, DV_RET))
    ng = H_RET // heads
    kw, vw = heads * DK_RET, heads * DV_RET
    head_spec = lambda shape: pl.BlockSpec((heads,) + shape, lambda b, g, t: (g, 0, 0))
    return pl.pallas_call(
        functools.partial(_ret_kernel, heads=heads),
        grid=(nb, ng, nt),
        in_specs=[pl.BlockSpec((1, tb, kw), lambda b, g, t: (b, t, g)),
                  pl.BlockSpec((1, tb, kw), lambda b, g, t: (b, t, ng + g)),
                  pl.BlockSpec((1, tb, vw), lambda b, g, t: (b, t, g)),
                  pl.BlockSpec((1, tb, vw), lambda b, g, t: (b, t, ng + g)),
                  pl.BlockSpec((tb, DK_RET), lambda b, g, t: (t, 0)),
                  pl.BlockSpec((tb, DK_RET), lambda b, g, t: (t, 0)),
                  head_spec((tb, tb)), head_spec((tb, DK_RET)), head_spec((tb, DK_RET)), head_spec((1, DV_RET)),
                  pl.BlockSpec((1, heads, DK_RET, DV_RET), lambda b, g, t: (b, g, 0, 0)),
                  pl.BlockSpec((1, vw), lambda b, g, t: (0, g))],
        out_specs=[pl.BlockSpec((1, tb, vw), lambda b, g, t: (b, t, g)),
                   pl.BlockSpec((1, heads, DK_RET, DV_RET), lambda b, g, t: (b, g, 0, 0))],
        out_shape=[jax.ShapeDtypeStruct((nb, l, H_RET * DV_RET), BF16),
                   jax.ShapeDtypeStruct((nb, H_RET, DK_RET, DV_RET), F32)],
        scratch_shapes=[pltpu.VMEM((heads, DK_RET, DV_RET), F32)],
        compiler_params=_params(("arbitrary", "arbitrary", "arbitrary"), VMEM_LIMIT),
        name="retention",
    )(zqk, zqk, zb, zb, cos, sin, dmat, qdec, kdec, sdec, s0, g_ret_gn.reshape(1, -1))


def _fox_prompt_kernel(q_ref, k_ref, v_ref, fk_ref, ftok_ref, o_ref, m_scr, l_scr, acc_scr, fq_scr, *, heads, tk):
    g = pl.program_id(1)
    qi = pl.program_id(2)
    tq = q_ref.shape[1]
    nck = tk // LANES
    hs = range(heads)
    sls = [slice(hh * DH_FOX, (hh + 1) * DH_FOX) for hh in hs]
    ft = ftok_ref[0]
    lane = lax.broadcasted_iota(I32, ft.shape, 1)
    for hh in hs:
        m_scr[hh] = jnp.full((tq, LANES), -jnp.inf, F32)
        l_scr[hh] = jnp.zeros((tq, LANES), F32)
        acc_scr[hh] = jnp.zeros((tq, DH_FOX), F32)
        fq = jnp.sum(jnp.where(lane == g * heads + hh, ft, 0.0), axis=-1, keepdims=True)
        fq_scr[hh] = jnp.broadcast_to(fq, (tq, LANES))
    ones = jnp.ones((tk, LANES), BF16)

    def steps(ki, masked):
        rows = pl.ds(pl.multiple_of(ki * tk, tk), tk)
        m_prev = [m_scr[hh] for hh in hs]
        t = [_nt_dot(q_ref[0, :, sls[hh]], k_ref[0, rows, sls[hh]]) for hh in hs]
        t = [t[hh] - fk_ref[0, hh, pl.ds(ki, 1), :] for hh in hs]
        if masked:
            row = lax.broadcasted_iota(I32, (tq, tk), 0)
            col = lax.broadcasted_iota(I32, (tq, tk), 1)
            t = [jnp.where(col <= row, t[hh], -jnp.inf) for hh in hs]
        m_new = [jnp.maximum(m_prev[hh], jnp.max(t[hh], axis=-1, keepdims=True) + fq_scr[hh]) for hh in hs]
        shift = [m_new[hh] - fq_scr[hh] for hh in hs]
        p = [jnp.concatenate([jnp.exp2(t[hh][:, c * LANES:(c + 1) * LANES] - shift[hh]) for c in range(nck)],
                             axis=1).astype(BF16) for hh in hs]
        alpha = [jnp.exp2(m_prev[hh] - m_new[hh]) for hh in hs]
        pv = [jnp.dot(p[hh], jnp.concatenate([v_ref[0, rows, sls[hh]], ones], axis=1),
                      preferred_element_type=F32) for hh in hs]
        for hh in hs:
            l_scr[hh] = alpha[hh] * l_scr[hh] + pv[hh][:, DH_FOX:]
            acc_scr[hh] = alpha[hh] * acc_scr[hh] + pv[hh][:, :DH_FOX]
            m_scr[hh] = m_new[hh]

    def body(ki, carry):
        steps(ki, False)
        return carry

    lax.fori_loop(0, qi, body, 0)
    steps(qi, True)
    for hh in hs:
        o_ref[0, :, sls[hh]] = (acc_scr[hh] / l_scr[hh]).astype(BF16)


def _fox_prompt(zb, kb, vb, f_rows, f_tok, tq, heads=2):
    nb, s, hd = kb.shape
    nq = s // tq
    w = heads * DH_FOX
    q_off = (2 * hd) // w
    kv_spec = pl.BlockSpec((1, s, w), lambda b, g, qi: (b, 0, g))
    return pl.pallas_call(
        functools.partial(_fox_prompt_kernel, heads=heads, tk=tq),
        grid=(nb, H_FOX // heads, nq),
        in_specs=[pl.BlockSpec((1, tq, w), lambda b, g, qi: (b, qi, q_off + g)),
                  kv_spec, kv_spec,
                  pl.BlockSpec((1, heads, nq, tq), lambda b, g, qi: (b, g, 0, 0)),
                  pl.BlockSpec((1, tq, H_FOX), lambda b, g, qi: (b, qi, 0))],
        out_specs=pl.BlockSpec((1, tq, w), lambda b, g, qi: (b, qi, g)),
        out_shape=jax.ShapeDtypeStruct((nb, s, hd), BF16),
        scratch_shapes=[pltpu.VMEM((heads, tq, LANES), F32), pltpu.VMEM((heads, tq, LANES), F32),
                        pltpu.VMEM((heads, tq, DH_FOX), F32), pltpu.VMEM((heads, tq, LANES), F32)],
        compiler_params=_params(("arbitrary",) * 3, VMEM_LIMIT),
        name="fox_prompt",
    )(zb, kb, vb, f_rows.reshape(nb, H_FOX, nq, tq), f_tok)


def _fox_sample_kernel(q_ref, ck_ref, cv_ref, kn_ref, vn_ref, fk_ref, ftok_ref, o_ref, *, heads):
    g = pl.program_id(1)
    l = q_ref.shape[1]
    p_len = ck_ref.shape[2]
    ft = ftok_ref[0]
    lane = lax.broadcasted_iota(I32, ft.shape, 1)
    row = lax.broadcasted_iota(I32, (l, l), 0)
    col = lax.broadcasted_iota(I32, (l, l), 1)
    for hh in range(heads):
        hg = g * heads + hh
        sl = slice(hh * DH_FOX, (hh + 1) * DH_FOX)
        fk = fk_ref[0, pl.ds(hg, 1), :]
        fq = jnp.sum(jnp.where(lane == hg, ft, 0.0), axis=-1, keepdims=True)
        qh = q_ref[0, :, sl]
        s1 = _nt_dot(qh, ck_ref[0, 0, :, hh, :].astype(BF16)) + fq - fk[:, :p_len]
        s2 = _nt_dot(qh, kn_ref[0, :, sl]) + fq - fk[:, p_len:p_len + l]
        s2 = jnp.where(col <= row, s2, -jnp.inf)
        m = jnp.maximum(jnp.max(s1, axis=-1, keepdims=True), jnp.max(s2, axis=-1, keepdims=True))
        p1 = jnp.exp2(s1 - m)
        p2 = jnp.exp2(s2 - m)
        den = jnp.sum(p1, axis=-1, keepdims=True) + jnp.sum(p2, axis=-1, keepdims=True)
        o = jnp.dot(p1.astype(BF16), cv_ref[0, 0, :, hh, :].astype(BF16), preferred_element_type=F32)
        o = o + jnp.dot(p2.astype(BF16), vn_ref[0, :, sl], preferred_element_type=F32)
        o_ref[0, :, sl] = (o / den).astype(BF16)


def _fox_sample(zb, kb, vb, cache_k, cache_v, f_rows, f_tok, heads=8):
    nb, l, hd = kb.shape
    p_len = cache_k.shape[2]
    sp = f_rows.shape[-1]
    w = heads * DH_FOX
    q_off = (2 * hd) // w
    assert p_len % l == 0
    new_spec = pl.BlockSpec((1, l, w), lambda b, g: (b, 0, g))
    cache_spec = pl.BlockSpec((1, 1, p_len, heads, DH_FOX), lambda b, g: (0, b, 0, g, 0))
    return pl.pallas_call(
        functools.partial(_fox_sample_kernel, heads=heads),
        grid=(nb, H_FOX // heads),
        in_specs=[pl.BlockSpec((1, l, w), lambda b, g: (b, 0, q_off + g)),
                  cache_spec, cache_spec, new_spec, new_spec,
                  pl.BlockSpec((1, H_FOX, sp), lambda b, g: (b, 0, 0)),
                  pl.BlockSpec((1, l, H_FOX), lambda b, g: (b, p_len // l, 0))],
        out_specs=new_spec,
        out_shape=jax.ShapeDtypeStruct((nb, l, hd), BF16),
        compiler_params=_params(("arbitrary", "arbitrary"), VMEM_LIMIT),
        name="fox_sample",
    )(zb, cache_k, cache_v, kb, vb, f_rows, f_tok)


def _mix1_kernel(or_ref, of_ref, w1_ref, w2_ref, gr_ref, gf_ref, o_ref):
    a = jnp.dot(or_ref[...], w1_ref[...], preferred_element_type=F32)
    b = jnp.dot(of_ref[...], w2_ref[...], preferred_element_type=F32)
    gr = _sigmoid(gr_ref[...].astype(F32))
    gf = _sigmoid(gf_ref[...].astype(F32))
    o_ref[...] = (gr * a + gf * b).astype(BF16)


def _mix1(o_r, o_f, w1, w2, zb, tm, tn):
    m, d = o_r.shape
    gr_off = (3 * d) // tn
    gf_off = (4 * d) // tn
    return pl.pallas_call(
        _mix1_kernel,
        grid=(m // tm, d // tn),
        in_specs=[pl.BlockSpec((tm, d), lambda i, j: (i, 0)),
                  pl.BlockSpec((tm, d), lambda i, j: (i, 0)),
                  pl.BlockSpec((d, tn), lambda i, j: (0, j)),
                  pl.BlockSpec((d, tn), lambda i, j: (0, j)),
                  pl.BlockSpec((tm, tn), lambda i, j: (i, gr_off + j)),
                  pl.BlockSpec((tm, tn), lambda i, j: (i, gf_off + j))],
        out_specs=pl.BlockSpec((tm, tn), lambda i, j: (i, j)),
        out_shape=jax.ShapeDtypeStruct((m, d), BF16),
        compiler_params=_params(("arbitrary", "arbitrary"), VMEM_LIMIT),
        name="mix_gates",
    )(o_r, o_f, w1, w2, zb, zb)


def _mix2_kernel(m_ref, w_ref, x_ref, gt_ref, sh_ref, sc_ref, g_ref, wr_ref, br_ref, cnt0_ref,
                 h1_ref, hn_ref, eidx_ref, rank_ref, wts_ref, cnt_ref):
    bb, tl, d = x_ref.shape
    tm = bb * tl
    ne = wr_ref.shape[1]

    @pl.when((pl.program_id(0) == 0) & (pl.program_id(1) == 0))
    def _():
        cnt_ref[...] = cnt0_ref[...]

    mix = jnp.dot(m_ref[...].reshape(tm, d), w_ref[...], preferred_element_type=F32)
    h1 = x_ref[...] + gt_ref[...] * mix.reshape(bb, tl, d)
    h1_ref[...] = h1
    y = h1 * lax.rsqrt(jnp.mean(h1 * h1, axis=-1, keepdims=True) + EPS) * g_ref[...]
    hn = (y * (1.0 + sc_ref[...]) + sh_ref[...]).reshape(tm, d)
    hn_ref[...] = _pack_halves(hn)

    logits = jnp.dot(hn, wr_ref[...], precision=lax.Precision.HIGHEST, preferred_element_type=F32)
    scores = _sigmoid(logits)
    cur = scores + br_ref[...]
    lane = lax.broadcasted_iota(I32, (tm, ne), 1).astype(F32)
    slot = lax.broadcasted_iota(I32, (tm, LANES), 1)
    r = lax.broadcasted_iota(I32, (tm, tm), 0)
    c = lax.broadcasted_iota(I32, (tm, tm), 1)
    lower = (c < r).astype(BF16)
    picks = []
    mask = jnp.zeros((tm, ne), F32)
    for _ in range(TOP_K):
        mx = jnp.max(cur, axis=-1, keepdims=True)
        idx = jnp.min(jnp.where(cur == mx, lane, float(ne)), axis=-1, keepdims=True)
        pick = lane == idx
        picks.append((idx, pick))
        mask = jnp.where(pick, 1.0, mask)
        cur = jnp.where(pick, -jnp.inf, cur)
    rank = jnp.dot(lower, mask.astype(BF16), preferred_element_type=F32) + cnt_ref[0:1, :]
    cnt_ref[...] = cnt_ref[...] + jnp.sum(mask, axis=0, keepdims=True)
    wsum = jnp.sum(mask * scores, axis=-1, keepdims=True)
    eidx = jnp.zeros((tm, LANES), I32)
    rnk = jnp.zeros((tm, LANES), I32)
    wts = jnp.zeros((tm, LANES), F32)
    for kk, (idx, pick) in enumerate(picks):
        sc_k = jnp.sum(jnp.where(pick, scores, 0.0), axis=-1, keepdims=True)
        rk_k = jnp.sum(jnp.where(pick, rank, 0.0), axis=-1, keepdims=True)
        eidx = jnp.where(slot == kk, idx.astype(I32), eidx)
        rnk = jnp.where(slot == kk, rk_k.astype(I32), rnk)
        wts = jnp.where(slot == kk, sc_k / wsum * ROUTE_SCALE, wts)
    eidx_ref[...] = eidx
    rank_ref[...] = rnk
    wts_ref[...] = wts


def _mix2(m3, w_out, x, mod, g2, w_router, b_router, cnt0, bb, tl):
    nb, l, d = x.shape
    ne = w_router.shape[1]
    t = nb * l
    tm = bb * tl
    nj = l // tl
    mod_spec = lambda k: pl.BlockSpec((bb, 1, d), lambda i, j: (i, 0, k))
    tok_spec = lambda w: pl.BlockSpec((tm, w), lambda i, j: (i * nj + j, 0))
    blk3 = pl.BlockSpec((bb, tl, d), lambda i, j: (i, j, 0))
    return pl.pallas_call(
        _mix2_kernel,
        grid=(nb // bb, nj),
        in_specs=[blk3,
                  pl.BlockSpec((d, d), lambda i, j: (0, 0)),
                  blk3,
                  mod_spec(2), mod_spec(3), mod_spec(4),
                  pl.BlockSpec((1, 1, d), lambda i, j: (0, 0, 0)),
                  pl.BlockSpec((d, ne), lambda i, j: (0, 0)),
                  pl.BlockSpec((1, ne), lambda i, j: (0, 0)),
                  pl.BlockSpec((8, ne), lambda i, j: (0, 0))],
        out_specs=[blk3, tok_spec(d // 2), tok_spec(LANES), tok_spec(LANES), tok_spec(LANES),
                   pl.BlockSpec((8, ne), lambda i, j: (0, 0))],
        out_shape=[jax.ShapeDtypeStruct((nb, l, d), F32),
                   jax.ShapeDtypeStruct((t, d // 2), I32),
                   jax.ShapeDtypeStruct((t, LANES), I32),
                   jax.ShapeDtypeStruct((t, LANES), I32),
                   jax.ShapeDtypeStruct((t, LANES), F32),
                   jax.ShapeDtypeStruct((8, ne), F32)],
        compiler_params=_params(("arbitrary", "arbitrary"), VMEM_LIMIT),
        name="outproj_norm2_router",
    )(m3, w_out, x, mod, mod, mod, g2.reshape(1, 1, d), w_router, b_router.reshape(1, ne), cnt0)


def _dispatch_kernel(zs_ref, zn_ref, dest_ref, xp_ref, xs_ref, buf_ref, zero_scr, sem, *, np_blocks):
    i = pl.program_id(0)
    tb = xp_ref.shape[0]

    def scatter(x_ref):
        def row_copy(r, kk):
            d = dest_ref[r * DEST_W + kk]
            return pltpu.make_async_copy(x_ref.at[pl.ds(r, 1), :], buf_ref.at[pl.ds(d, 1), :], sem)

        def issue(r, carry):
            for kk in range(TOP_K):
                row_copy(r, kk).start()
            return carry

        def drain(r, carry):
            for kk in range(TOP_K):
                row_copy(r, kk).wait()
            return carry

        lax.fori_loop(0, tb, issue, 0)
        lax.fori_loop(0, tb, drain, 0)

    @pl.when(i < np_blocks)
    def _():
        scatter(xp_ref)

    @pl.when(i >= np_blocks)
    def _():
        scatter(xs_ref)

    @pl.when(i == pl.num_programs(0) - 1)
    def _():
        zero_scr[...] = jnp.zeros_like(zero_scr)

        def per_expert(e, carry):
            start = zs_ref[e]

            def zero_copy(r):
                return pltpu.make_async_copy(zero_scr.at[pl.ds(0, 1), :], buf_ref.at[pl.ds(start + r, 1), :], sem)

            def issue(r, c):
                zero_copy(r).start()
                return c

            def drain(r, c):
                zero_copy(r).wait()
                return c

            lax.fori_loop(0, zn_ref[e], issue, 0)
            lax.fori_loop(0, zn_ref[e], drain, 0)
            return carry

        lax.fori_loop(0, zs_ref.shape[0], per_expert, 0)


def _dispatch(xp, xs, dest, zero_start, zero_n, rows, tb):
    tp, w = xp.shape
    ts = xs.shape[0]
    npb = tp // tb
    grid_spec = pltpu.PrefetchScalarGridSpec(
        num_scalar_prefetch=2,
        grid=(npb + ts // tb,),
        in_specs=[pl.BlockSpec((tb * DEST_W,), lambda i, zs, zn: (i,), memory_space=pltpu.SMEM),
                  pl.BlockSpec((tb, w), lambda i, zs, zn: (jnp.minimum(i, npb - 1), 0)),
                  pl.BlockSpec((tb, w), lambda i, zs, zn: (jnp.maximum(i - npb, 0), 0))],
        out_specs=pl.BlockSpec(memory_space=pl.ANY),
        scratch_shapes=[pltpu.VMEM((8, w), I32), pltpu.SemaphoreType.DMA(())],
    )
    return pl.pallas_call(
        functools.partial(_dispatch_kernel, np_blocks=npb),
        grid_spec=grid_spec,
        out_shape=jax.ShapeDtypeStruct((rows, w), I32),
        compiler_params=_params(("arbitrary",), VMEM_LIMIT),
        name="moe_dispatch",
    )(zero_start, zero_n, dest, xp, xs)


def _expert_kernel(be_ref, nv_ref, nu_ref, x_ref, wg_ref, wu_ref, wgt_ref, wut_ref, wda_ref, wdb_ref, o_ref,
                   xb_scr, hid_scr, wgb_scr, wub_scr, wdn_scr, *, n_main):
    del be_ref, nu_ref
    i = pl.program_id(0)
    j = pl.program_id(1)
    half = x_ref.shape[1]
    ft = wg_ref.shape[2]
    tail = wgt_ref.shape[2]
    ct = wda_ref.shape[2]
    nv = nv_ref[i]
    n_big = nv // 2
    big = 2 * SUB_ROWS

    def for_chunks(fn):
        def body(c, carry):
            fn(pl.multiple_of(c * big, big), big)
            return carry

        lax.fori_loop(0, n_big, body, 0)

        @pl.when(nv % 2 == 1)
        def _():
            fn(pl.multiple_of(n_big * big, SUB_ROWS), SUB_ROWS)

    @pl.when(nv > 0)
    def _():
        @pl.when(j == 0)
        def _():
            def unpack(r0, rows):
                hi, lo = _unpack_halves(x_ref[pl.ds(r0, rows), :])
                xb_scr[pl.ds(r0, rows), :half] = hi.astype(BF16)
                xb_scr[pl.ds(r0, rows), half:] = lo.astype(BF16)

            for_chunks(unpack)

        def up_phase(wg, wu, width):
            wgb_scr[:, :width] = wg[0].astype(BF16)
            wub_scr[:, :width] = wu[0].astype(BF16)

            def chunk(r0, rows):
                xb = xb_scr[pl.ds(r0, rows), :]
                gate = jnp.dot(xb, wgb_scr[:, :width], preferred_element_type=F32)
                up = jnp.dot(xb, wub_scr[:, :width], preferred_element_type=F32)
                hid_scr[j, pl.ds(r0, rows), :width] = (gate * _sigmoid(gate) * up).astype(BF16)

            for_chunks(chunk)

        @pl.when(j < n_main)
        def _():
            up_phase(wg_ref, wu_ref, ft)

        @pl.when(j == n_main)
        def _():
            up_phase(wgt_ref, wut_ref, tail)

        @pl.when(j > n_main)
        def _():
            wdn_scr[:, :ct] = wda_ref[0].astype(BF16)
            wdn_scr[:, ct:] = wdb_ref[0].astype(BF16)

            def chunk(r0, rows):
                pieces = [hid_scr[t, pl.ds(r0, rows), :] for t in range(n_main)]
                pieces.append(hid_scr[n_main, pl.ds(r0, rows), :tail])
                hid = jnp.concatenate(pieces, axis=1)
                res = jnp.dot(hid, wdn_scr[...], preferred_element_type=F32)
                o_ref[pl.ds(r0, rows), :] = _pack_halves(res)

            for_chunks(chunk)

    @pl.when(j > n_main)
    def _():
        def body(c, carry):
            r0 = pl.multiple_of(c * SUB_ROWS, SUB_ROWS)
            o_ref[pl.ds(r0, SUB_ROWS), :] = jnp.zeros((SUB_ROWS, o_ref.shape[1]), I32)
            return carry

        lax.fori_loop(nv, o_ref.shape[0] // SUB_ROWS, body, 0)


def _expert_ffn(xp, wg, wu, wd, block_expert, n_valid, n_used, rb):
    rows, half = xp.shape
    d = 2 * half
    f = wg.shape[2]
    n_main = f // F_TILE
    tail = f - n_main * F_TILE
    assert tail > 0 and tail % LANES == 0 and (n_main * F_TILE) % tail == 0
    tail_blk = (n_main * F_TILE) // tail
    ct = F_TILE
    n_out = half // ct
    nblk = rows // rb

    def main_col(i, j, be, nv, nu):
        return (be[i], 0, jnp.minimum(j, n_main - 1))

    def out_col(j):
        return jnp.clip(j - (n_main + 1), 0, n_out - 1)

    grid_spec = pltpu.PrefetchScalarGridSpec(
        num_scalar_prefetch=3,
        grid=(nblk, n_main + 1 + n_out),
        in_specs=[pl.BlockSpec((rb, half), lambda i, j, be, nv, nu: (jnp.minimum(i, nu[0] - 1), 0)),
                  pl.BlockSpec((1, d, F_TILE), main_col),
                  pl.BlockSpec((1, d, F_TILE), main_col),
                  pl.BlockSpec((1, d, tail), lambda i, j, be, nv, nu: (be[i], 0, tail_blk)),
                  pl.BlockSpec((1, d, tail), lambda i, j, be, nv, nu: (be[i], 0, tail_blk)),
                  pl.BlockSpec((1, f, ct), lambda i, j, be, nv, nu: (be[i], 0, out_col(j))),
                  pl.BlockSpec((1, f, ct), lambda i, j, be, nv, nu: (be[i], 0, n_out + out_col(j)))],
        out_specs=pl.BlockSpec((rb, ct), lambda i, j, be, nv, nu: (i, out_col(j))),
        scratch_shapes=[pltpu.VMEM((rb, d), BF16), pltpu.VMEM((n_main + 1, rb, F_TILE), BF16),
                        pltpu.VMEM((d, F_TILE), BF16), pltpu.VMEM((d, F_TILE), BF16),
                        pltpu.VMEM((f, 2 * ct), BF16)],
    )
    return pl.pallas_call(
        functools.partial(_expert_kernel, n_main=n_main),
        grid_spec=grid_spec,
        out_shape=jax.ShapeDtypeStruct((rows, half), I32),
        compiler_params=_params(("arbitrary", "arbitrary"), VMEM_LIMIT),
        name="expert_ffn",
    )(block_expert, n_valid, n_used, xp, wg, wu, wg, wu, wd, wd)


def _combine_kernel(dcur_ref, dnext_ref, wts_ref, sh_ref, h1_ref, gt_ref, g_ref, eo_ref, y_ref,
                    gbuf, sems):
    bb, tl, d = h1_ref.shape
    tb = bb * tl
    half = d // 2
    nj = pl.num_programs(1)
    step = pl.program_id(0) * nj + pl.program_id(1)
    nsteps = pl.num_programs(0) * nj
    slot = step % 2

    def row_copy(dref, sl, r, kk):
        dd = dref[r * DEST_W + kk]
        return pltpu.make_async_copy(eo_ref.at[pl.ds(dd, 1), :], gbuf.at[sl, kk, pl.ds(r, 1), :],
                                     sems.at[sl])

    def issue(dref, sl):
        def body(r, carry):
            for kk in range(TOP_K):
                row_copy(dref, sl, r, kk).start()
            return carry
        lax.fori_loop(0, tb, body, 0)

    @pl.when(step == 0)
    def _():
        issue(dcur_ref, 0)

    @pl.when(step + 1 < nsteps)
    def _():
        issue(dnext_ref, 1 - slot)

    def drain(r, carry):
        for kk in range(TOP_K):
            row_copy(dcur_ref, slot, r, kk).wait()
        return carry
    lax.fori_loop(0, tb, drain, 0)

    wts = wts_ref[...]
    s_hi, s_lo = _unpack_halves(sh_ref[...])
    for kk in range(TOP_K):
        hi, lo = _unpack_halves(gbuf[slot, kk])
        wk = wts[:, kk:kk + 1]
        s_hi = s_hi + wk * hi
        s_lo = s_lo + wk * lo
    h2_hi = h1_ref[:, :, :half] + gt_ref[:, :, :half] * s_hi.reshape(bb, tl, half)
    h2_lo = h1_ref[:, :, half:] + gt_ref[:, :, half:] * s_lo.reshape(bb, tl, half)
    ms = (jnp.sum(h2_hi * h2_hi, axis=-1, keepdims=True)
          + jnp.sum(h2_lo * h2_lo, axis=-1, keepdims=True)) / d
    inv = lax.rsqrt(ms + EPS)
    y_ref[:, :, :half] = h2_hi * inv * g_ref[:, :, :half]
    y_ref[:, :, half:] = h2_lo * inv * g_ref[:, :, half:]


def _combine(dest, wts, shared, h1, mod, g_final, eo, bb, tl):
    nb, l, d = h1.shape
    tb = bb * tl
    nj = l // tl
    nsteps = (nb // bb) * nj
    lin = lambda i, j: i * nj + j
    return pl.pallas_call(
        _combine_kernel,
        grid=(nb // bb, nj),
        in_specs=[pl.BlockSpec((tb * DEST_W,), lambda i, j: (lin(i, j),), memory_space=pltpu.SMEM),
                  pl.BlockSpec((tb * DEST_W,), lambda i, j: (jnp.minimum(lin(i, j) + 1, nsteps - 1),),
                               memory_space=pltpu.SMEM),
                  pl.BlockSpec((tb, LANES), lambda i, j: (lin(i, j), 0)),
                  pl.BlockSpec((tb, d // 2), lambda i, j: (lin(i, j), 0)),
                  pl.BlockSpec((bb, tl, d), lambda i, j: (i, j, 0)),
                  pl.BlockSpec((bb, 1, d), lambda i, j: (i, 0, 5)),
                  pl.BlockSpec((1, 1, d), lambda i, j: (0, 0, 0)),
                  pl.BlockSpec(memory_space=pl.ANY)],
        out_specs=pl.BlockSpec((bb, tl, d), lambda i, j: (i, j, 0)),
        out_shape=jax.ShapeDtypeStruct((nb, l, d), F32),
        scratch_shapes=[pltpu.VMEM((2, TOP_K, tb, d // 2), I32), pltpu.SemaphoreType.DMA((2,))],
        compiler_params=_params(("arbitrary", "arbitrary"), VMEM_LIMIT),
        name="moe_combine",
    )(dest, dest, wts, shared, h1, mod, g_final.reshape(1, 1, d), eo)


def _prep_w_in(w_in):
    d = w_in.shape[0]
    qr_w, vr_w, qf_w = H_RET * DK_RET, H_RET * DV_RET, H_FOX * DH_FOX
    sizes = (qr_w, qr_w, vr_w, vr_w, qf_w, qf_w, qf_w, H_FOX, d, d)
    offs = [0]
    for s in sizes:
        offs.append(offs[-1] + s)
    part = lambda n: w_in[:, offs[n]:offs[n + 1]]
    w_qk = jnp.concatenate([part(0), part(1) * (DK_RET ** -0.5)], axis=1).astype(BF16)
    w_b = jnp.concatenate([part(2), part(3), part(4) * (DH_FOX ** -0.5 * LOG2E), part(8), part(9)], axis=1).astype(BF16)
    w_kf = part(5).astype(BF16)
    w_vf = part(6).astype(BF16)
    w_fl = jnp.pad(part(7), ((0, 0), (0, LANES - H_FOX))).astype(BF16)
    return w_qk, w_b, w_kf, w_vf, w_fl


def _trunk_front(x, mod, lw, bb, tl, tm, fox, s0, pos0, ret_tb):
    nb, l, d = x.shape
    t = nb * l
    hn, logf = _norm1(x, lw["g_norm1"], mod, lw["w_fl"], lw["b_fl"], bb, tl)
    hn2d = hn.reshape(t, d)
    zqk = _matmul(hn2d, lw["w_qk"], F32, tm, 512, "proj_qk_ret")
    kf, kb = _matmul_heads(hn2d, lw["w_kf"], nb, l, bb, tl, "proj_k_fox")
    vf, vb = _matmul_heads(hn2d, lw["w_vf"], nb, l, bb, tl, "proj_v_fox")
    zb = _matmul(hn2d, lw["w_b"], BF16, tm, 512, "proj_rest")
    zb3 = zb.reshape(nb, l, -1)
    o_r, s_new = _retention(zqk.reshape(nb, l, -1), zb3, s0, lw["g_ret_gn"], pos0, ret_tb)
    o_f = fox(zb3, kb.reshape(nb, l, -1), vb.reshape(nb, l, -1), logf)
    m = _mix1(o_r.reshape(t, d), o_f.reshape(t, d), lw["w_ret_out"], lw["w_fox_out"], zb, tm, 512)
    return m.reshape(nb, l, d), (kf, vf, logf, s_new)


def kernel(x_prompt, x_sample, c_prompt, c_sample, cache_fox_k, cache_fox_v, cache_fox_logf, state_ret, w_ada, b_ada, g_norm1, w_in, b_forget, g_ret_gn, w_ret_out, w_fox_out, w_out, g_norm2, w_router, b_router, w_exp_gate, w_exp_up, w_exp_down, w_sh_gate, w_sh_up, w_sh_down, g_final):
    depth = w_in.shape[0]
    assert depth == 1
    nbp, s, d = x_prompt.shape
    nbs, ls, _ = x_sample.shape
    p_len = cache_fox_k.shape[2]
    ne = w_router.shape[-1]
    tp, ts = nbp * s, nbs * ls

    nb_all = nbp + nbs
    nb_pad = -(-nb_all // 8) * 8
    c_all = jnp.pad(jnp.concatenate([c_prompt, c_sample], axis=0), ((0, nb_pad - nb_all), (0, 0)))
    mod_all = _adaln(c_all, w_ada[0], b_ada[0])
    mod_p = mod_all[:nbp].reshape(nbp, 1, 6 * d)
    mod_s = mod_all[nbp:nb_all].reshape(nbs, 1, 6 * d)

    w_qk, w_b, w_kf, w_vf, w_fl = _prep_w_in(w_in[0])
    lw = dict(g_norm1=g_norm1[0], w_fl=w_fl,
              b_fl=jnp.pad(b_forget[0], (0, LANES - H_FOX)).reshape(1, LANES),
              w_qk=w_qk, w_b=w_b, w_kf=w_kf, w_vf=w_vf, g_ret_gn=g_ret_gn[0],
              w_ret_out=w_ret_out[0].astype(BF16), w_fox_out=w_fox_out[0].astype(BF16))
    w_out_b = w_out[0].astype(BF16)

    tl_p = min(s, 512)
    tm_p = min(tp, 1024)
    bb_s = max(1, min(nbs, 512 // ls))
    tm_s = min(ts, 1024)

    def fox_p(zb3, kb3, vb3, logf):
        f_rows = _cumsum_rows(logf.swapaxes(1, 2), min(s, 512), LOG2E)
        return _fox_prompt(zb3, kb3, vb3, f_rows, f_rows.swapaxes(1, 2), min(s, 1024))

    def fox_s(zb3, kb3, vb3, logf):
        full = jnp.concatenate([cache_fox_logf[0].astype(F32), logf], axis=1)
        sp = -(-(p_len + ls) // LANES) * LANES
        full = jnp.pad(full, ((0, 0), (0, sp - p_len - ls), (0, 0)))
        f_rows = _cumsum_rows(full.swapaxes(1, 2), sp, LOG2E)
        return _fox_sample(zb3, kb3, vb3, cache_fox_k, cache_fox_v, f_rows, f_rows.swapaxes(1, 2))

    s0_p = jnp.zeros((nbp, H_RET, DK_RET, DV_RET), F32)
    m_p, st_p = _trunk_front(x_prompt, mod_p, lw, 1, tl_p, tm_p, fox_p, s0_p, 0, min(s, 256))
    m_s, st_s = _trunk_front(x_sample, mod_s, lw, bb_s, ls, tm_s, fox_s, state_ret[0], p_len, ls)

    cnt0 = jnp.zeros((8, ne), F32)
    tl2_p = min(s, 256)
    bb2_s = max(1, min(nbs, 256 // ls))
    h1_p, hn_p, e_p, r_p, wt_p, cnt_p = _mix2(m_p, w_out_b, x_prompt, mod_p, g_norm2[0], w_router[0],
                                              b_router[0], cnt0, 1, tl2_p)
    h1_s, hn_s, e_s, r_s, wt_s, cnt_s = _mix2(m_s, w_out_b, x_sample, mod_s, g_norm2[0], w_router[0],
                                              b_router[0], cnt_p, bb2_s, ls)

    counts = cnt_s[0].astype(I32)
    rblk = EXPERT_ROWS
    padded = (counts + rblk - 1) // rblk * rblk
    pad_end = jnp.cumsum(padded)
    pad_start = pad_end - padded
    t_all = tp + ts
    nblk = (t_all * TOP_K + ne * (rblk - 1) + rblk - 1) // rblk
    rows = nblk * rblk
    blk_start = jnp.arange(nblk, dtype=I32) * rblk
    block_expert = jnp.minimum(jnp.searchsorted(pad_end, blk_start, side="right"), ne - 1).astype(I32)
    n_used = (pad_end[-1] // rblk).astype(I32).reshape(1)
    valid_rows = jnp.clip(pad_start[block_expert] + counts[block_expert] - blk_start, 0, rblk)
    valid_rows = jnp.where(blk_start < pad_end[-1], valid_rows, 0)
    n_valid = ((valid_rows + SUB_ROWS - 1) // SUB_ROWS).astype(I32)
    zero_start = (pad_start + counts).astype(I32)
    zero_n = ((-counts) % SUB_ROWS).astype(I32)
    e_all = jnp.concatenate([e_p, e_s], axis=0)[:, :DEST_W]
    r_all = jnp.concatenate([r_p, r_s], axis=0)[:, :DEST_W]
    dest = (pad_start[e_all] + r_all).astype(I32).reshape(-1)

    buf = _dispatch(hn_p, hn_s, dest, zero_start, zero_n, rows, min(tp, ts, 256))
    eo = _expert_ffn(buf, w_exp_gate[0], w_exp_up[0], w_exp_down[0], block_expert, n_valid, n_used, rblk)

    def shared_ffn(hn):
        rb = min(hn.shape[0], rblk)
        nb_ = hn.shape[0] // rb
        return _expert_ffn(hn, w_sh_gate, w_sh_up, w_sh_down, jnp.zeros((nb_,), I32),
                           jnp.full((nb_,), rb // SUB_ROWS, I32), jnp.full((1,), nb_, I32), rb)

    tlc_p = min(s, 128)
    bbc_s = max(1, min(nbs, 128 // ls))
    y_p = _combine(dest[:tp * DEST_W], wt_p, shared_ffn(hn_p), h1_p, mod_p, g_final, eo, 1, tlc_p)
    y_s = _combine(dest[tp * DEST_W:], wt_s, shared_ffn(hn_s), h1_s, mod_s, g_final, eo, bbc_s, ls)

    kf_p, vf_p, logf_p, s_p = st_p
    kf_s, vf_s, logf_s, s_s = st_s
    return (y_p, y_s, kf_p[None], vf_p[None], logf_p[None], s_p[None],
            kf_s[None], vf_s[None], logf_s[None], s_s[None])
```

```python
import functools
import math

import jax
import jax.numpy as jnp
from jax import lax
from jax.experimental import pallas as pl
from jax.experimental.pallas import tpu as pltpu

F32 = jnp.float32
BF16 = jnp.bfloat16
I32 = jnp.int32

CHUNK = 64
H_RET = 8
DK_RET = 128
DV_RET = 256
H_FOX = 16
DH_FOX = 128
TOP_K = 6
ROUTE_SCALE = 2.5
ROPE_BASE = 10000.0
EPS = 1e-6
LOG2E = math.log2(math.e)

LANES = 128
DEST_W = 8
EXPERT_ROWS = 2048
SUB_ROWS = 256
F_TILE = 256
VMEM_LIMIT = 56 * 1024 * 1024


def _params(sem, vmem=None):
    return pltpu.CompilerParams(dimension_semantics=sem, vmem_limit_bytes=vmem)


def _sigmoid(x):
    return 1.0 / (1.0 + jnp.exp(-x))


def _nt_dot(a, b):
    return lax.dot_general(a, b, (((1,), (1,)), ((), ())), preferred_element_type=F32)


def _pack_halves(x):
    n = x.shape[-1] // 2
    hi = pltpu.bitcast(x[:, :n].astype(BF16).astype(F32), I32)
    lo = pltpu.bitcast(x[:, n:].astype(BF16).astype(F32), I32)
    return hi | lax.shift_right_logical(lo, jnp.int32(16))


def _unpack_halves(u):
    hi = pltpu.bitcast(u & jnp.int32(-65536), F32)
    lo = pltpu.bitcast(lax.shift_left(u, jnp.int32(16)), F32)
    return hi, lo


def _ada_kernel(c_ref, w_ref, b_ref, o_ref):
    c = c_ref[...]
    s = (c * _sigmoid(c)).astype(BF16)
    o_ref[...] = jnp.dot(s, w_ref[...].astype(BF16), preferred_element_type=F32) + b_ref[...]


def _adaln(c, w_ada, b_ada):
    nb, d = c.shape
    n = w_ada.shape[1]
    tn = 1024
    return pl.pallas_call(
        _ada_kernel,
        grid=(n // tn,),
        in_specs=[pl.BlockSpec((nb, d), lambda j: (0, 0)),
                  pl.BlockSpec((d, tn), lambda j: (0, j)),
                  pl.BlockSpec((1, tn), lambda j: (0, j))],
        out_specs=pl.BlockSpec((nb, tn), lambda j: (0, j)),
        out_shape=jax.ShapeDtypeStruct((nb, n), F32),
        compiler_params=_params(("arbitrary",), VMEM_LIMIT),
        name="adaln",
    )(c, w_ada, b_ada.reshape(1, n))


def _norm1_kernel(x_ref, g_ref, sh_ref, sc_ref, wf_ref, bf_ref, hn_ref, logf_ref):
    bb, tl, d = x_ref.shape
    x = x_ref[...]
    y = x * lax.rsqrt(jnp.mean(x * x, axis=-1, keepdims=True) + EPS) * g_ref[...]
    hn = (y * (1.0 + sc_ref[...]) + sh_ref[...]).astype(BF16)
    hn_ref[...] = hn
    z = jnp.dot(hn.reshape(bb * tl, d), wf_ref[...], preferred_element_type=F32) + bf_ref[...]
    logf = jnp.minimum(z, 0.0) - jnp.log(1.0 + jnp.exp(-jnp.abs(z)))
    logf_ref[...] = logf[:, :H_FOX].reshape(bb, tl, H_FOX)


def _norm1(x, g, mod, wf, bfg, bb, tl):
    nb, l, d = x.shape
    mod_spec = lambda k: pl.BlockSpec((bb, 1, d), lambda i, j: (i, 0, k))
    return pl.pallas_call(
        _norm1_kernel,
        grid=(nb // bb, l // tl),
        in_specs=[pl.BlockSpec((bb, tl, d), lambda i, j: (i, j, 0)),
                  pl.BlockSpec((1, 1, d), lambda i, j: (0, 0, 0)),
                  mod_spec(0), mod_spec(1),
                  pl.BlockSpec((d, LANES), lambda i, j: (0, 0)),
                  pl.BlockSpec((1, LANES), lambda i, j: (0, 0))],
        out_specs=[pl.BlockSpec((bb, tl, d), lambda i, j: (i, j, 0)),
                   pl.BlockSpec((bb, tl, H_FOX), lambda i, j: (i, j, 0))],
        out_shape=[jax.ShapeDtypeStruct((nb, l, d), BF16),
                   jax.ShapeDtypeStruct((nb, l, H_FOX), F32)],
        compiler_params=_params(("arbitrary", "arbitrary"), VMEM_LIMIT),
        name="norm1",
    )(x, g.reshape(1, 1, d), mod, mod, wf, bfg)


def _mm_kernel(a_ref, w_ref, o_ref):
    o_ref[...] = jnp.dot(a_ref[...], w_ref[...], preferred_element_type=F32).astype(o_ref.dtype)


def _matmul(a, w, out_dtype, tm, tn, name):
    m, k = a.shape
    n = w.shape[1]
    return pl.pallas_call(
        _mm_kernel,
        grid=(m // tm, n // tn),
        in_specs=[pl.BlockSpec((tm, k), lambda i, j: (i, 0)),
                  pl.BlockSpec((k, tn), lambda i, j: (0, j))],
        out_specs=pl.BlockSpec((tm, tn), lambda i, j: (i, j)),
        out_shape=jax.ShapeDtypeStruct((m, n), out_dtype),
        compiler_params=_params(("arbitrary", "arbitrary"), VMEM_LIMIT),
        name=name,
    )(a, w)


def _mm_heads_kernel(a_ref, w_ref, o_ref, ob_ref):
    bb, tl, nh, dh = o_ref.shape
    res = jnp.dot(a_ref[...], w_ref[...], preferred_element_type=F32)
    ob_ref[...] = res.astype(BF16)
    for hh in range(nh):
        o_ref[:, :, hh, :] = res[:, hh * dh:(hh + 1) * dh].reshape(bb, tl, dh)


def _matmul_heads(a, w, nb, l, bb, tl, name):
    m, k = a.shape
    n = w.shape[1]
    nj = l // tl
    return pl.pallas_call(
        _mm_heads_kernel,
        grid=(nb // bb, nj),
        in_specs=[pl.BlockSpec((bb * tl, k), lambda i, j: (i * nj + j, 0)),
                  pl.BlockSpec((k, n), lambda i, j: (0, 0))],
        out_specs=[pl.BlockSpec((bb, tl, H_FOX, DH_FOX), lambda i, j: (i, j, 0, 0)),
                   pl.BlockSpec((bb * tl, n), lambda i, j: (i * nj + j, 0))],
        out_shape=[jax.ShapeDtypeStruct((nb, l, H_FOX, DH_FOX), F32),
                   jax.ShapeDtypeStruct((m, n), BF16)],
        compiler_params=_params(("arbitrary", "arbitrary"), VMEM_LIMIT),
        name=name,
    )(a, w)


def _cumsum_kernel(x_ref, o_ref, carry_ref, *, scale):
    tl = x_ref.shape[-1]

    @pl.when(pl.program_id(1) == 0)
    def _():
        carry_ref[...] = jnp.zeros_like(carry_ref)

    r = lax.broadcasted_iota(I32, (tl, tl), 0)
    c = lax.broadcasted_iota(I32, (tl, tl), 1)
    tri = (r <= c).astype(F32)
    y = jnp.dot(x_ref[0], tri, precision=lax.Precision.HIGHEST, preferred_element_type=F32)
    y = y + carry_ref[:, 0:1]
    o_ref[0] = y * scale
    carry_ref[...] = jnp.broadcast_to(y[:, tl - 1:tl], carry_ref.shape)


def _cumsum_rows(x, tl, scale):
    nb, h, s = x.shape
    return pl.pallas_call(
        functools.partial(_cumsum_kernel, scale=scale),
        grid=(nb, s // tl),
        in_specs=[pl.BlockSpec((1, h, tl), lambda b, t: (b, 0, t))],
        out_specs=pl.BlockSpec((1, h, tl), lambda b, t: (b, 0, t)),
        out_shape=jax.ShapeDtypeStruct((nb, h, s), F32),
        scratch_shapes=[pltpu.VMEM((h, LANES), F32)],
        compiler_params=_params(("arbitrary", "arbitrary"), VMEM_LIMIT),
        name="cumsum_logf",
    )(x)


def _ret_kernel(q_ref, k_ref, v_ref, g_ref, cos_ref, sin_ref, dmat_ref, qdec_ref, kdec_ref, sdec_ref,
                s0_ref, gn_ref, o_ref, sout_ref, s_scr, *, heads):
    t = pl.program_id(2)
    hs = range(heads)

    @pl.when(t == 0)
    def _():
        s_scr[...] = s0_ref[0]

    cos = cos_ref[...]
    sin = sin_ref[...]
    half = DK_RET // 2
    ks = [slice(hh * DK_RET, (hh + 1) * DK_RET) for hh in hs]
    vs = [slice(hh * DV_RET, (hh + 1) * DV_RET) for hh in hs]
    q = [q_ref[0, :, ks[hh]] for hh in hs]
    k = [k_ref[0, :, ks[hh]] for hh in hs]
    qr = [q[hh] * cos + pltpu.roll(q[hh], half, 1) * sin for hh in hs]
    kr = [k[hh] * cos + pltpu.roll(k[hh], half, 1) * sin for hh in hs]
    vb = [v_ref[0, :, vs[hh]] for hh in hs]
    state = [s_scr[hh] for hh in hs]
    s = [_nt_dot(qr[hh].astype(BF16), kr[hh].astype(BF16)) * dmat_ref[hh] for hh in hs]
    o = [jnp.dot(s[hh].astype(BF16), vb[hh], preferred_element_type=F32) for hh in hs]
    o = [o[hh] + jnp.dot((qr[hh] * qdec_ref[hh]).astype(BF16), state[hh].astype(BF16),
                         preferred_element_type=F32) for hh in hs]
    kd_t = [(kr[hh] * kdec_ref[hh]).T.astype(BF16) for hh in hs]
    s_new = [state[hh] * sdec_ref[hh] + jnp.dot(kd_t[hh], vb[hh], preferred_element_type=F32) for hh in hs]
    for hh in hs:
        s_scr[hh] = s_new[hh]

    @pl.when(t == pl.num_programs(2) - 1)
    def _():
        sout_ref[0] = s_scr[...]

    mu = [jnp.mean(o[hh], axis=-1, keepdims=True) for hh in hs]
    oc = [o[hh] - mu[hh] for hh in hs]
    var = [jnp.mean(oc[hh] * oc[hh], axis=-1, keepdims=True) for hh in hs]
    on = [oc[hh] * lax.rsqrt(var[hh] + EPS) * gn_ref[:, vs[hh]] for hh in hs]
    g = [g_ref[0, :, vs[hh]].astype(F32) for hh in hs]
    for hh in hs:
        o_ref[0, :, vs[hh]] = (on[hh] * (g[hh] * _sigmoid(g[hh]))).astype(BF16)


def _retention(zqk, zb, s0, g_ret_gn, pos0, tb, heads=4):
    nb, l, _ = zqk.shape
    nt = l // tb
    half = DK_RET // 2
    inv = 1.0 / (ROPE_BASE ** (jnp.arange(half, dtype=F32) * 2.0 / DK_RET))
    ang = (pos0 + jnp.arange(l)).astype(F32)[:, None] * inv[None, :]
    cos = jnp.concatenate([jnp.cos(ang), jnp.cos(ang)], axis=-1)
    sin = jnp.concatenate([-jnp.sin(ang), jnp.sin(ang)], axis=-1)
    lg = jnp.log1p(-jnp.exp2(-5.0 - jnp.arange(H_RET, dtype=F32)))
    i = jnp.arange(tb)
    d = (i[:, None] - i[None, :]).astype(F32)
    ci, cj = i[:, None] // CHUNK, i[None, :] // CHUNK
    expo = jnp.where(ci == cj, jnp.abs(d), d)
    dmat = jnp.where((cj <= ci)[None], jnp.exp(expo[None] * lg[:, None, None]), 0.0)
    fi = i.astype(F32)
    qdec = jnp.broadcast_to(jnp.exp((fi[None, :] + 1.0) * lg[:, None])[:, :, None], (H_RET, tb, DK_RET))
    kdec = jnp.broadcast_to(jnp.exp((tb - 1.0 - fi[None, :]) * lg[:, None])[:, :, None], (H_RET, tb, DK_RET))
    sdec = jnp.broadcast_to(jnp.exp(tb * lg)[:, None, None], (H_RET, 1, DV_RET))
    ng = H_RET // heads
    kw, vw = heads * DK_RET, heads * DV_RET
    head_spec = lambda shape: pl.BlockSpec((heads,) + shape, lambda b, g, t: (g, 0, 0))
    return pl.pallas_call(
        functools.partial(_ret_kernel, heads=heads),
        grid=(nb, ng, nt),
        in_specs=[pl.BlockSpec((1, tb, kw), lambda b, g, t: (b, t, g)),
                  pl.BlockSpec((1, tb, kw), lambda b, g, t: (b, t, ng + g)),
                  pl.BlockSpec((1, tb, vw), lambda b, g, t: (b, t, g)),
                  pl.BlockSpec((1, tb, vw), lambda b, g, t: (b, t, ng + g)),
                  pl.BlockSpec((tb, DK_RET), lambda b, g, t: (t, 0)),
                  pl.BlockSpec((tb, DK_RET), lambda b, g, t: (t, 0)),
                  head_spec((tb, tb)), head_spec((tb, DK_RET)), head_spec((tb, DK_RET)), head_spec((1, DV_RET)),
                  pl.BlockSpec((1, heads, DK_RET, DV_RET), lambda b, g, t: (b, g, 0, 0)),
                  pl.BlockSpec((1, vw), lambda b, g, t: (0, g))],
        out_specs=[pl.BlockSpec((1, tb, vw), lambda b, g, t: (b, t, g)),
                   pl.BlockSpec((1, heads, DK_RET, DV_RET), lambda b, g, t: (b, g, 0, 0))],
        out_shape=[jax.ShapeDtypeStruct((nb, l, H_RET * DV_RET), BF16),
                   jax.ShapeDtypeStruct((nb, H_RET, DK_RET, DV_RET), F32)],
        scratch_shapes=[pltpu.VMEM((heads, DK_RET, DV_RET), F32)],
        compiler_params=_params(("arbitrary", "arbitrary", "arbitrary"), VMEM_LIMIT),
        name="retention",
    )(zqk, zqk, zb, zb, cos, sin, dmat, qdec, kdec, sdec, s0, g_ret_gn.reshape(1, -1))


def _fox_prompt_kernel(q_ref, k_ref, v_ref, fk_ref, ftok_ref, o_ref, m_scr, l_scr, acc_scr, fq_scr, *, heads, tk):
    g = pl.program_id(1)
    qi = pl.program_id(2)
    tq = q_ref.shape[1]
    nck = tk // LANES
    hs = range(heads)
    sls = [slice(hh * DH_FOX, (hh + 1) * DH_FOX) for hh in hs]
    ft = ftok_ref[0]
    lane = lax.broadcasted_iota(I32, ft.shape, 1)
    for hh in hs:
        m_scr[hh] = jnp.full((tq, LANES), -jnp.inf, F32)
        l_scr[hh] = jnp.zeros((tq, LANES), F32)
        acc_scr[hh] = jnp.zeros((tq, DH_FOX), F32)
        fq = jnp.sum(jnp.where(lane == g * heads + hh, ft, 0.0), axis=-1, keepdims=True)
        fq_scr[hh] = jnp.broadcast_to(fq, (tq, LANES))
    ones = jnp.ones((tk, LANES), BF16)

    def steps(ki, masked):
        rows = pl.ds(pl.multiple_of(ki * tk, tk), tk)
        m_prev = [m_scr[hh] for hh in hs]
        t = [_nt_dot(q_ref[0, :, sls[hh]], k_ref[0, rows, sls[hh]]) for hh in hs]
        t = [t[hh] - fk_ref[0, hh, pl.ds(ki, 1), :] for hh in hs]
        if masked:
            row = lax.broadcasted_iota(I32, (tq, tk), 0)
            col = lax.broadcasted_iota(I32, (tq, tk), 1)
            t = [jnp.where(col <= row, t[hh], -jnp.inf) for hh in hs]
        m_new = [jnp.maximum(m_prev[hh], jnp.max(t[hh], axis=-1, keepdims=True) + fq_scr[hh]) for hh in hs]
        shift = [m_new[hh] - fq_scr[hh] for hh in hs]
        p = [jnp.concatenate([jnp.exp2(t[hh][:, c * LANES:(c + 1) * LANES] - shift[hh]) for c in range(nck)],
                             axis=1).astype(BF16) for hh in hs]
        alpha = [jnp.exp2(m_prev[hh] - m_new[hh]) for hh in hs]
        pv = [jnp.dot(p[hh], jnp.concatenate([v_ref[0, rows, sls[hh]], ones], axis=1),
                      preferred_element_type=F32) for hh in hs]
        for hh in hs:
            l_scr[hh] = alpha[hh] * l_scr[hh] + pv[hh][:, DH_FOX:]
            acc_scr[hh] = alpha[hh] * acc_scr[hh] + pv[hh][:, :DH_FOX]
            m_scr[hh] = m_new[hh]

    def body(ki, carry):
        steps(ki, False)
        return carry

    lax.fori_loop(0, qi, body, 0)
    steps(qi, True)
    for hh in hs:
        o_ref[0, :, sls[hh]] = (acc_scr[hh] / l_scr[hh]).astype(BF16)


def _fox_prompt(zb, kb, vb, f_rows, f_tok, tq, heads=2):
    nb, s, hd = kb.shape
    nq = s // tq
    w = heads * DH_FOX
    q_off = (2 * hd) // w
    kv_spec = pl.BlockSpec((1, s, w), lambda b, g, qi: (b, 0, g))
    return pl.pallas_call(
        functools.partial(_fox_prompt_kernel, heads=heads, tk=tq),
        grid=(nb, H_FOX // heads, nq),
        in_specs=[pl.BlockSpec((1, tq, w), lambda b, g, qi: (b, qi, q_off + g)),
                  kv_spec, kv_spec,
                  pl.BlockSpec((1, heads, nq, tq), lambda b, g, qi: (b, g, 0, 0)),
                  pl.BlockSpec((1, tq, H_FOX), lambda b, g, qi: (b, qi, 0))],
        out_specs=pl.BlockSpec((1, tq, w), lambda b, g, qi: (b, qi, g)),
        out_shape=jax.ShapeDtypeStruct((nb, s, hd), BF16),
        scratch_shapes=[pltpu.VMEM((heads, tq, LANES), F32), pltpu.VMEM((heads, tq, LANES), F32),
                        pltpu.VMEM((heads, tq, DH_FOX), F32), pltpu.VMEM((heads, tq, LANES), F32)],
        compiler_params=_params(("arbitrary",) * 3, VMEM_LIMIT),
        name="fox_prompt",
    )(zb, kb, vb, f_rows.reshape(nb, H_FOX, nq, tq), f_tok)


def _fox_sample_kernel(q_ref, ck_ref, cv_ref, kn_ref, vn_ref, fk_ref, ftok_ref, o_ref, *, heads):
    g = pl.program_id(1)
    l = q_ref.shape[1]
    p_len = ck_ref.shape[1]
    ft = ftok_ref[0]
    lane = lax.broadcasted_iota(I32, ft.shape, 1)
    row = lax.broadcasted_iota(I32, (l, l), 0)
    col = lax.broadcasted_iota(I32, (l, l), 1)
    for hh in range(heads):
        hg = g * heads + hh
        sl = slice(hh * DH_FOX, (hh + 1) * DH_FOX)
        fk = fk_ref[0, pl.ds(hg, 1), :]
        fq = jnp.sum(jnp.where(lane == hg, ft, 0.0), axis=-1, keepdims=True)
        qh = q_ref[0, :, sl]
        s1 = _nt_dot(qh, ck_ref[0, :, sl]) + fq - fk[:, :p_len]
        s2 = _nt_dot(qh, kn_ref[0, :, sl]) + fq - fk[:, p_len:p_len + l]
        s2 = jnp.where(col <= row, s2, -jnp.inf)
        m = jnp.maximum(jnp.max(s1, axis=-1, keepdims=True), jnp.max(s2, axis=-1, keepdims=True))
        p1 = jnp.exp2(s1 - m)
        p2 = jnp.exp2(s2 - m)
        den = jnp.sum(p1, axis=-1, keepdims=True) + jnp.sum(p2, axis=-1, keepdims=True)
        o = jnp.dot(p1.astype(BF16), cv_ref[0, :, sl], preferred_element_type=F32)
        o = o + jnp.dot(p2.astype(BF16), vn_ref[0, :, sl], preferred_element_type=F32)
        o_ref[0, :, sl] = (o / den).astype(BF16)


def _fox_sample(zb, kb, vb, cache_k, cache_v, f_rows, f_tok, heads=8):
    nb, l, hd = kb.shape
    p_len = cache_k.shape[1]
    sp = f_rows.shape[-1]
    w = heads * DH_FOX
    q_off = (2 * hd) // w
    assert p_len % l == 0
    new_spec = pl.BlockSpec((1, l, w), lambda b, g: (b, 0, g))
    cache_spec = pl.BlockSpec((1, p_len, w), lambda b, g: (b, 0, g))
    return pl.pallas_call(
        functools.partial(_fox_sample_kernel, heads=heads),
        grid=(nb, H_FOX // heads),
        in_specs=[pl.BlockSpec((1, l, w), lambda b, g: (b, 0, q_off + g)),
                  cache_spec, cache_spec, new_spec, new_spec,
                  pl.BlockSpec((1, H_FOX, sp), lambda b, g: (b, 0, 0)),
                  pl.BlockSpec((1, l, H_FOX), lambda b, g: (b, p_len // l, 0))],
        out_specs=new_spec,
        out_shape=jax.ShapeDtypeStruct((nb, l, hd), BF16),
        compiler_params=_params(("arbitrary", "arbitrary"), VMEM_LIMIT),
        name="fox_sample",
    )(zb, cache_k, cache_v, kb, vb, f_rows, f_tok)


def _mix1_kernel(or_ref, of_ref, w1_ref, w2_ref, gr_ref, gf_ref, o_ref):
    a = jnp.dot(or_ref[...], w1_ref[...], preferred_element_type=F32)
    b = jnp.dot(of_ref[...], w2_ref[...], preferred_element_type=F32)
    gr = _sigmoid(gr_ref[...].astype(F32))
    gf = _sigmoid(gf_ref[...].astype(F32))
    o_ref[...] = (gr * a + gf * b).astype(BF16)


def _mix1(o_r, o_f, w1, w2, zb, tm, tn):
    m, d = o_r.shape
    gr_off = (3 * d) // tn
    gf_off = (4 * d) // tn
    return pl.pallas_call(
        _mix1_kernel,
        grid=(m // tm, d // tn),
        in_specs=[pl.BlockSpec((tm, d), lambda i, j: (i, 0)),
                  pl.BlockSpec((tm, d), lambda i, j: (i, 0)),
                  pl.BlockSpec((d, tn), lambda i, j: (0, j)),
                  pl.BlockSpec((d, tn), lambda i, j: (0, j)),
                  pl.BlockSpec((tm, tn), lambda i, j: (i, gr_off + j)),
                  pl.BlockSpec((tm, tn), lambda i, j: (i, gf_off + j))],
        out_specs=pl.BlockSpec((tm, tn), lambda i, j: (i, j)),
        out_shape=jax.ShapeDtypeStruct((m, d), BF16),
        compiler_params=_params(("arbitrary", "arbitrary"), VMEM_LIMIT),
        name="mix_gates",
    )(o_r, o_f, w1, w2, zb, zb)


def _mix2_kernel(m_ref, w_ref, x_ref, gt_ref, sh_ref, sc_ref, g_ref, wr_ref, br_ref, cnt0_ref,
                 h1_ref, hn_ref, eidx_ref, rank_ref, wts_ref, cnt_ref):
    bb, tl, d = x_ref.shape
    tm = bb * tl
    ne = wr_ref.shape[1]

    @pl.when((pl.program_id(0) == 0) & (pl.program_id(1) == 0))
    def _():
        cnt_ref[...] = cnt0_ref[...]

    mix = jnp.dot(m_ref[...].reshape(tm, d), w_ref[...], preferred_element_type=F32)
    h1 = x_ref[...] + gt_ref[...] * mix.reshape(bb, tl, d)
    h1_ref[...] = h1
    y = h1 * lax.rsqrt(jnp.mean(h1 * h1, axis=-1, keepdims=True) + EPS) * g_ref[...]
    hn = (y * (1.0 + sc_ref[...]) + sh_ref[...]).reshape(tm, d)
    hn_ref[...] = _pack_halves(hn)

    logits = jnp.dot(hn, wr_ref[...], precision=lax.Precision.HIGHEST, preferred_element_type=F32)
    scores = _sigmoid(logits)
    cur = scores + br_ref[...]
    lane = lax.broadcasted_iota(I32, (tm, ne), 1).astype(F32)
    slot = lax.broadcasted_iota(I32, (tm, LANES), 1)
    r = lax.broadcasted_iota(I32, (tm, tm), 0)
    c = lax.broadcasted_iota(I32, (tm, tm), 1)
    lower = (c < r).astype(BF16)
    picks = []
    mask = jnp.zeros((tm, ne), F32)
    for _ in range(TOP_K):
        mx = jnp.max(cur, axis=-1, keepdims=True)
        idx = jnp.min(jnp.where(cur == mx, lane, float(ne)), axis=-1, keepdims=True)
        pick = lane == idx
        picks.append((idx, pick))
        mask = jnp.where(pick, 1.0, mask)
        cur = jnp.where(pick, -jnp.inf, cur)
    rank = jnp.dot(lower, mask.astype(BF16), preferred_element_type=F32) + cnt_ref[0:1, :]
    cnt_ref[...] = cnt_ref[...] + jnp.sum(mask, axis=0, keepdims=True)
    wsum = jnp.sum(mask * scores, axis=-1, keepdims=True)
    eidx = jnp.zeros((tm, LANES), I32)
    rnk = jnp.zeros((tm, LANES), I32)
    wts = jnp.zeros((tm, LANES), F32)
    for kk, (idx, pick) in enumerate(picks):
        sc_k = jnp.sum(jnp.where(pick, scores, 0.0), axis=-1, keepdims=True)
        rk_k = jnp.sum(jnp.where(pick, rank, 0.0), axis=-1, keepdims=True)
        eidx = jnp.where(slot == kk, idx.astype(I32), eidx)
        rnk = jnp.where(slot == kk, rk_k.astype(I32), rnk)
        wts = jnp.where(slot == kk, sc_k / wsum * ROUTE_SCALE, wts)
    eidx_ref[...] = eidx
    rank_ref[...] = rnk
    wts_ref[...] = wts


def _mix2(m3, w_out, x, mod, g2, w_router, b_router, cnt0, bb, tl):
    nb, l, d = x.shape
    ne = w_router.shape[1]
    t = nb * l
    tm = bb * tl
    nj = l // tl
    mod_spec = lambda k: pl.BlockSpec((bb, 1, d), lambda i, j: (i, 0, k))
    tok_spec = lambda w: pl.BlockSpec((tm, w), lambda i, j: (i * nj + j, 0))
    blk3 = pl.BlockSpec((bb, tl, d), lambda i, j: (i, j, 0))
    return pl.pallas_call(
        _mix2_kernel,
        grid=(nb // bb, nj),
        in_specs=[blk3,
                  pl.BlockSpec((d, d), lambda i, j: (0, 0)),
                  blk3,
                  mod_spec(2), mod_spec(3), mod_spec(4),
                  pl.BlockSpec((1, 1, d), lambda i, j: (0, 0, 0)),
                  pl.BlockSpec((d, ne), lambda i, j: (0, 0)),
                  pl.BlockSpec((1, ne), lambda i, j: (0, 0)),
                  pl.BlockSpec((8, ne), lambda i, j: (0, 0))],
        out_specs=[blk3, tok_spec(d // 2), tok_spec(LANES), tok_spec(LANES), tok_spec(LANES),
                   pl.BlockSpec((8, ne), lambda i, j: (0, 0))],
        out_shape=[jax.ShapeDtypeStruct((nb, l, d), F32),
                   jax.ShapeDtypeStruct((t, d // 2), I32),
                   jax.ShapeDtypeStruct((t, LANES), I32),
                   jax.ShapeDtypeStruct((t, LANES), I32),
                   jax.ShapeDtypeStruct((t, LANES), F32),
                   jax.ShapeDtypeStruct((8, ne), F32)],
        compiler_params=_params(("arbitrary", "arbitrary"), VMEM_LIMIT),
        name="outproj_norm2_router",
    )(m3, w_out, x, mod, mod, mod, g2.reshape(1, 1, d), w_router, b_router.reshape(1, ne), cnt0)


def _dispatch_kernel(zs_ref, zn_ref, dest_ref, xp_ref, xs_ref, buf_ref, zero_scr, sem, *, np_blocks):
    i = pl.program_id(0)
    tb = xp_ref.shape[0]

    def scatter(x_ref):
        def row_copy(r, kk):
            d = dest_ref[r * DEST_W + kk]
            return pltpu.make_async_copy(x_ref.at[pl.ds(r, 1), :], buf_ref.at[pl.ds(d, 1), :], sem)

        def issue(r, carry):
            for kk in range(TOP_K):
                row_copy(r, kk).start()
            return carry

        def drain(r, carry):
            for kk in range(TOP_K):
                row_copy(r, kk).wait()
            return carry

        lax.fori_loop(0, tb, issue, 0)
        lax.fori_loop(0, tb, drain, 0)

    @pl.when(i < np_blocks)
    def _():
        scatter(xp_ref)

    @pl.when(i >= np_blocks)
    def _():
        scatter(xs_ref)

    @pl.when(i == pl.num_programs(0) - 1)
    def _():
        zero_scr[...] = jnp.zeros_like(zero_scr)

        def per_expert(e, carry):
            start = zs_ref[e]

            def zero_copy(r):
                return pltpu.make_async_copy(zero_scr.at[pl.ds(0, 1), :], buf_ref.at[pl.ds(start + r, 1), :], sem)

            def issue(r, c):
                zero_copy(r).start()
                return c

            def drain(r, c):
                zero_copy(r).wait()
                return c

            lax.fori_loop(0, zn_ref[e], issue, 0)
            lax.fori_loop(0, zn_ref[e], drain, 0)
            return carry

        lax.fori_loop(0, zs_ref.shape[0], per_expert, 0)


def _dispatch(xp, xs, dest, zero_start, zero_n, rows, tb):
    tp, w = xp.shape
    ts = xs.shape[0]
    npb = tp // tb
    grid_spec = pltpu.PrefetchScalarGridSpec(
        num_scalar_prefetch=2,
        grid=(npb + ts // tb,),
        in_specs=[pl.BlockSpec((tb * DEST_W,), lambda i, zs, zn: (i,), memory_space=pltpu.SMEM),
                  pl.BlockSpec((tb, w), lambda i, zs, zn: (jnp.minimum(i, npb - 1), 0)),
                  pl.BlockSpec((tb, w), lambda i, zs, zn: (jnp.maximum(i - npb, 0), 0))],
        out_specs=pl.BlockSpec(memory_space=pl.ANY),
        scratch_shapes=[pltpu.VMEM((8, w), I32), pltpu.SemaphoreType.DMA(())],
    )
    return pl.pallas_call(
        functools.partial(_dispatch_kernel, np_blocks=npb),
        grid_spec=grid_spec,
        out_shape=jax.ShapeDtypeStruct((rows, w), I32),
        compiler_params=_params(("arbitrary",), VMEM_LIMIT),
        name="moe_dispatch",
    )(zero_start, zero_n, dest, xp, xs)


def _expert_kernel(be_ref, nv_ref, nu_ref, x_ref, wg_ref, wu_ref, wgt_ref, wut_ref, wda_ref, wdb_ref, o_ref,
                   xb_scr, hid_scr, wgb_scr, wub_scr, wdn_scr, *, n_main):
    del be_ref, nu_ref
    i = pl.program_id(0)
    j = pl.program_id(1)
    half = x_ref.shape[1]
    ft = wg_ref.shape[2]
    tail = wgt_ref.shape[2]
    ct = wda_ref.shape[2]
    nv = nv_ref[i]
    n_big = nv // 2
    big = 2 * SUB_ROWS

    def for_chunks(fn):
        def body(c, carry):
            fn(pl.multiple_of(c * big, big), big)
            return carry

        lax.fori_loop(0, n_big, body, 0)

        @pl.when(nv % 2 == 1)
        def _():
            fn(pl.multiple_of(n_big * big, SUB_ROWS), SUB_ROWS)

    @pl.when(nv > 0)
    def _():
        @pl.when(j == 0)
        def _():
            def unpack(r0, rows):
                hi, lo = _unpack_halves(x_ref[pl.ds(r0, rows), :])
                xb_scr[pl.ds(r0, rows), :half] = hi.astype(BF16)
                xb_scr[pl.ds(r0, rows), half:] = lo.astype(BF16)

            for_chunks(unpack)

        def up_phase(wg, wu, width):
            wgb_scr[:, :width] = wg[0].astype(BF16)
            wub_scr[:, :width] = wu[0].astype(BF16)

            def chunk(r0, rows):
                xb = xb_scr[pl.ds(r0, rows), :]
                gate = jnp.dot(xb, wgb_scr[:, :width], preferred_element_type=F32)
                up = jnp.dot(xb, wub_scr[:, :width], preferred_element_type=F32)
                hid_scr[j, pl.ds(r0, rows), :width] = (gate * _sigmoid(gate) * up).astype(BF16)

            for_chunks(chunk)

        @pl.when(j < n_main)
        def _():
            up_phase(wg_ref, wu_ref, ft)

        @pl.when(j == n_main)
        def _():
            up_phase(wgt_ref, wut_ref, tail)

        @pl.when(j > n_main)
        def _():
            wdn_scr[:, :ct] = wda_ref[0].astype(BF16)
            wdn_scr[:, ct:] = wdb_ref[0].astype(BF16)

            def chunk(r0, rows):
                pieces = [hid_scr[t, pl.ds(r0, rows), :] for t in range(n_main)]
                pieces.append(hid_scr[n_main, pl.ds(r0, rows), :tail])
                hid = jnp.concatenate(pieces, axis=1)
                res = jnp.dot(hid, wdn_scr[...], preferred_element_type=F32)
                o_ref[pl.ds(r0, rows), :] = _pack_halves(res)

            for_chunks(chunk)

    @pl.when(j > n_main)
    def _():
        def body(c, carry):
            r0 = pl.multiple_of(c * SUB_ROWS, SUB_ROWS)
            o_ref[pl.ds(r0, SUB_ROWS), :] = jnp.zeros((SUB_ROWS, o_ref.shape[1]), I32)
            return carry

        lax.fori_loop(nv, o_ref.shape[0] // SUB_ROWS, body, 0)


def _expert_ffn(xp, wg, wu, wd, block_expert, n_valid, n_used, rb):
    rows, half = xp.shape
    d = 2 * half
    f = wg.shape[2]
    n_main = f // F_TILE
    tail = f - n_main * F_TILE
    assert tail > 0 and tail % LANES == 0 and (n_main * F_TILE) % tail == 0
    tail_blk = (n_main * F_TILE) // tail
    ct = F_TILE
    n_out = half // ct
    nblk = rows // rb

    def main_col(i, j, be, nv, nu):
        return (be[i], 0, jnp.minimum(j, n_main - 1))

    def out_col(j):
        return jnp.clip(j - (n_main + 1), 0, n_out - 1)

    grid_spec = pltpu.PrefetchScalarGridSpec(
        num_scalar_prefetch=3,
        grid=(nblk, n_main + 1 + n_out),
        in_specs=[pl.BlockSpec((rb, half), lambda i, j, be, nv, nu: (jnp.minimum(i, nu[0] - 1), 0),
                               pipeline_mode=pl.Buffered(1)),
                  pl.BlockSpec((1, d, F_TILE), main_col),
                  pl.BlockSpec((1, d, F_TILE), main_col),
                  pl.BlockSpec((1, d, tail), lambda i, j, be, nv, nu: (be[i], 0, tail_blk)),
                  pl.BlockSpec((1, d, tail), lambda i, j, be, nv, nu: (be[i], 0, tail_blk)),
                  pl.BlockSpec((1, f, ct), lambda i, j, be, nv, nu: (be[i], 0, out_col(j))),
                  pl.BlockSpec((1, f, ct), lambda i, j, be, nv, nu: (be[i], 0, n_out + out_col(j)))],
        out_specs=pl.BlockSpec((rb, ct), lambda i, j, be, nv, nu: (i, out_col(j))),
        scratch_shapes=[pltpu.VMEM((rb, d), BF16), pltpu.VMEM((n_main + 1, rb, F_TILE), BF16),
                        pltpu.VMEM((d, F_TILE), BF16), pltpu.VMEM((d, F_TILE), BF16),
                        pltpu.VMEM((f, 2 * ct), BF16)],
    )
    return pl.pallas_call(
        functools.partial(_expert_kernel, n_main=n_main),
        grid_spec=grid_spec,
        out_shape=jax.ShapeDtypeStruct((rows, half), I32),
        compiler_params=_params(("arbitrary", "arbitrary"), VMEM_LIMIT),
        name="expert_ffn",
    )(block_expert, n_valid, n_used, xp, wg, wu, wg, wu, wd, wd)


def _combine_kernel(dcur_ref, dnext_ref, wts_ref, sh_ref, h1_ref, gt_ref, g_ref, eo_ref, y_ref,
                    gbuf, sems):
    bb, tl, d = h1_ref.shape
    tb = bb * tl
    half = d // 2
    nj = pl.num_programs(1)
    step = pl.program_id(0) * nj + pl.program_id(1)
    nsteps = pl.num_programs(0) * nj
    slot = step % 2

    def row_copy(dref, sl, r, kk):
        dd = dref[r * DEST_W + kk]
        return pltpu.make_async_copy(eo_ref.at[pl.ds(dd, 1), :], gbuf.at[sl, kk, pl.ds(r, 1), :],
                                     sems.at[sl])

    def issue(dref, sl):
        def body(r, carry):
            for kk in range(TOP_K):
                row_copy(dref, sl, r, kk).start()
            return carry
        lax.fori_loop(0, tb, body, 0)

    @pl.when(step == 0)
    def _():
        issue(dcur_ref, 0)

    @pl.when(step + 1 < nsteps)
    def _():
        issue(dnext_ref, 1 - slot)

    def drain(r, carry):
        for kk in range(TOP_K):
            row_copy(dcur_ref, slot, r, kk).wait()
        return carry
    lax.fori_loop(0, tb, drain, 0)

    wts = wts_ref[...]
    s_hi, s_lo = _unpack_halves(sh_ref[...])
    for kk in range(TOP_K):
        hi, lo = _unpack_halves(gbuf[slot, kk])
        wk = wts[:, kk:kk + 1]
        s_hi = s_hi + wk * hi
        s_lo = s_lo + wk * lo
    h2_hi = h1_ref[:, :, :half] + gt_ref[:, :, :half] * s_hi.reshape(bb, tl, half)
    h2_lo = h1_ref[:, :, half:] + gt_ref[:, :, half:] * s_lo.reshape(bb, tl, half)
    ms = (jnp.sum(h2_hi * h2_hi, axis=-1, keepdims=True)
          + jnp.sum(h2_lo * h2_lo, axis=-1, keepdims=True)) / d
    inv = lax.rsqrt(ms + EPS)
    y_ref[:, :, :half] = h2_hi * inv * g_ref[:, :, :half]
    y_ref[:, :, half:] = h2_lo * inv * g_ref[:, :, half:]


def _combine(dest, wts, shared, h1, mod, g_final, eo, bb, tl):
    nb, l, d = h1.shape
    tb = bb * tl
    nj = l // tl
    nsteps = (nb // bb) * nj
    lin = lambda i, j: i * nj + j
    return pl.pallas_call(
        _combine_kernel,
        grid=(nb // bb, nj),
        in_specs=[pl.BlockSpec((tb * DEST_W,), lambda i, j: (lin(i, j),), memory_space=pltpu.SMEM),
                  pl.BlockSpec((tb * DEST_W,), lambda i, j: (jnp.minimum(lin(i, j) + 1, nsteps - 1),),
                               memory_space=pltpu.SMEM),
                  pl.BlockSpec((tb, LANES), lambda i, j: (lin(i, j), 0)),
                  pl.BlockSpec((tb, d // 2), lambda i, j: (lin(i, j), 0)),
                  pl.BlockSpec((bb, tl, d), lambda i, j: (i, j, 0)),
                  pl.BlockSpec((bb, 1, d), lambda i, j: (i, 0, 5)),
                  pl.BlockSpec((1, 1, d), lambda i, j: (0, 0, 0)),
                  pl.BlockSpec(memory_space=pl.ANY)],
        out_specs=pl.BlockSpec((bb, tl, d), lambda i, j: (i, j, 0)),
        out_shape=jax.ShapeDtypeStruct((nb, l, d), F32),
        scratch_shapes=[pltpu.VMEM((2, TOP_K, tb, d // 2), I32), pltpu.SemaphoreType.DMA((2,))],
        compiler_params=_params(("arbitrary", "arbitrary"), VMEM_LIMIT),
        name="moe_combine",
    )(dest, dest, wts, shared, h1, mod, g_final.reshape(1, 1, d), eo)


def _prep_w_in(w_in):
    d = w_in.shape[0]
    qr_w, vr_w, qf_w = H_RET * DK_RET, H_RET * DV_RET, H_FOX * DH_FOX
    sizes = (qr_w, qr_w, vr_w, vr_w, qf_w, qf_w, qf_w, H_FOX, d, d)
    offs = [0]
    for s in sizes:
        offs.append(offs[-1] + s)
    part = lambda n: w_in[:, offs[n]:offs[n + 1]]
    w_qk = jnp.concatenate([part(0), part(1) * (DK_RET ** -0.5)], axis=1).astype(BF16)
    w_b = jnp.concatenate([part(2), part(3), part(4) * (DH_FOX ** -0.5 * LOG2E), part(8), part(9)], axis=1).astype(BF16)
    w_kf = part(5).astype(BF16)
    w_vf = part(6).astype(BF16)
    w_fl = jnp.pad(part(7), ((0, 0), (0, LANES - H_FOX))).astype(BF16)
    return w_qk, w_b, w_kf, w_vf, w_fl


def _trunk_front(x, mod, lw, bb, tl, tm, fox, s0, pos0, ret_tb):
    nb, l, d = x.shape
    t = nb * l
    hn, logf = _norm1(x, lw["g_norm1"], mod, lw["w_fl"], lw["b_fl"], bb, tl)
    hn2d = hn.reshape(t, d)
    zqk = _matmul(hn2d, lw["w_qk"], F32, tm, 512, "proj_qk_ret")
    kf, kb = _matmul_heads(hn2d, lw["w_kf"], nb, l, bb, tl, "proj_k_fox")
    vf, vb = _matmul_heads(hn2d, lw["w_vf"], nb, l, bb, tl, "proj_v_fox")
    zb = _matmul(hn2d, lw["w_b"], BF16, tm, 512, "proj_rest")
    zb3 = zb.reshape(nb, l, -1)
    o_r, s_new = _retention(zqk.reshape(nb, l, -1), zb3, s0, lw["g_ret_gn"], pos0, ret_tb)
    o_f = fox(zb3, kb.reshape(nb, l, -1), vb.reshape(nb, l, -1), logf)
    m = _mix1(o_r.reshape(t, d), o_f.reshape(t, d), lw["w_ret_out"], lw["w_fox_out"], zb, tm, 512)
    return m.reshape(nb, l, d), (kf, vf, logf, s_new)


def kernel(x_prompt, x_sample, c_prompt, c_sample, cache_fox_k, cache_fox_v, cache_fox_logf, state_ret, w_ada, b_ada, g_norm1, w_in, b_forget, g_ret_gn, w_ret_out, w_fox_out, w_out, g_norm2, w_router, b_router, w_exp_gate, w_exp_up, w_exp_down, w_sh_gate, w_sh_up, w_sh_down, g_final):
    depth = w_in.shape[0]
    assert depth == 1
    nbp, s, d = x_prompt.shape
    nbs, ls, _ = x_sample.shape
    p_len = cache_fox_k.shape[2]
    ne = w_router.shape[-1]
    tp, ts = nbp * s, nbs * ls

    nb_all = nbp + nbs
    nb_pad = -(-nb_all // 8) * 8
    c_all = jnp.pad(jnp.concatenate([c_prompt, c_sample], axis=0), ((0, nb_pad - nb_all), (0, 0)))
    mod_all = _adaln(c_all, w_ada[0], b_ada[0])
    mod_p = mod_all[:nbp].reshape(nbp, 1, 6 * d)
    mod_s = mod_all[nbp:nb_all].reshape(nbs, 1, 6 * d)

    w_qk, w_b, w_kf, w_vf, w_fl = _prep_w_in(w_in[0])
    lw = dict(g_norm1=g_norm1[0], w_fl=w_fl,
              b_fl=jnp.pad(b_forget[0], (0, LANES - H_FOX)).reshape(1, LANES),
              w_qk=w_qk, w_b=w_b, w_kf=w_kf, w_vf=w_vf, g_ret_gn=g_ret_gn[0],
              w_ret_out=w_ret_out[0].astype(BF16), w_fox_out=w_fox_out[0].astype(BF16))
    w_out_b = w_out[0].astype(BF16)

    tl_p = min(s, 512)
    tm_p = min(tp, 1024)
    bb_s = max(1, min(nbs, 512 // ls))
    tm_s = min(ts, 1024)

    def fox_p(zb3, kb3, vb3, logf):
        f_rows = _cumsum_rows(logf.swapaxes(1, 2), min(s, 512), LOG2E)
        return _fox_prompt(zb3, kb3, vb3, f_rows, f_rows.swapaxes(1, 2), min(s, 1024))

    def fox_s(zb3, kb3, vb3, logf):
        full = jnp.concatenate([cache_fox_logf[0].astype(F32), logf], axis=1)
        sp = -(-(p_len + ls) // LANES) * LANES
        full = jnp.pad(full, ((0, 0), (0, sp - p_len - ls), (0, 0)))
        f_rows = _cumsum_rows(full.swapaxes(1, 2), sp, LOG2E)
        to_rows = lambda c: c[0].astype(BF16).reshape(nbs, p_len, -1)
        return _fox_sample(zb3, kb3, vb3, to_rows(cache_fox_k), to_rows(cache_fox_v), f_rows, f_rows.swapaxes(1, 2))

    s0_p = jnp.zeros((nbp, H_RET, DK_RET, DV_RET), F32)
    m_p, st_p = _trunk_front(x_prompt, mod_p, lw, 1, tl_p, tm_p, fox_p, s0_p, 0, min(s, 256))
    m_s, st_s = _trunk_front(x_sample, mod_s, lw, bb_s, ls, tm_s, fox_s, state_ret[0], p_len, ls)

    cnt0 = jnp.zeros((8, ne), F32)
    tl2_p = min(s, 256)
    bb2_s = max(1, min(nbs, 256 // ls))
    h1_p, hn_p, e_p, r_p, wt_p, cnt_p = _mix2(m_p, w_out_b, x_prompt, mod_p, g_norm2[0], w_router[0],
                                              b_router[0], cnt0, 1, tl2_p)
    h1_s, hn_s, e_s, r_s, wt_s, cnt_s = _mix2(m_s, w_out_b, x_sample, mod_s, g_norm2[0], w_router[0],
                                              b_router[0], cnt_p, bb2_s, ls)

    counts = cnt_s[0].astype(I32)
    rblk = EXPERT_ROWS
    padded = (counts + rblk - 1) // rblk * rblk
    pad_end = jnp.cumsum(padded)
    pad_start = pad_end - padded
    t_all = tp + ts
    nblk = (t_all * TOP_K + ne * (rblk - 1) + rblk - 1) // rblk
    rows = nblk * rblk
    blk_start = jnp.arange(nblk, dtype=I32) * rblk
    block_expert = jnp.minimum(jnp.searchsorted(pad_end, blk_start, side="right"), ne - 1).astype(I32)
    n_used = (pad_end[-1] // rblk).astype(I32).reshape(1)
    valid_rows = jnp.clip(pad_start[block_expert] + counts[block_expert] - blk_start, 0, rblk)
    valid_rows = jnp.where(blk_start < pad_end[-1], valid_rows, 0)
    n_valid = ((valid_rows + SUB_ROWS - 1) // SUB_ROWS).astype(I32)
    zero_start = (pad_start + counts).astype(I32)
    zero_n = ((-counts) % SUB_ROWS).astype(I32)
    e_all = jnp.concatenate([e_p, e_s], axis=0)[:, :DEST_W]
    r_all = jnp.concatenate([r_p, r_s], axis=0)[:, :DEST_W]
    dest = (pad_start[e_all] + r_all).astype(I32).reshape(-1)

    buf = _dispatch(hn_p, hn_s, dest, zero_start, zero_n, rows, min(tp, ts, 256))
    eo = _expert_ffn(buf, w_exp_gate[0], w_exp_up[0], w_exp_down[0], block_expert, n_valid, n_used, rblk)

    def shared_ffn(hn):
        rb = min(hn.shape[0], rblk)
        nb_ = hn.shape[0] // rb
        return _expert_ffn(hn, w_sh_gate, w_sh_up, w_sh_down, jnp.zeros((nb_,), I32),
                           jnp.full((nb_,), rb // SUB_ROWS, I32), jnp.full((1,), nb_, I32), rb)

    tlc_p = min(s, 128)
    bbc_s = max(1, min(nbs, 128 // ls))
    y_p = _combine(dest[:tp * DEST_W], wt_p, shared_ffn(hn_p), h1_p, mod_p, g_final, eo, 1, tlc_p)
    y_s = _combine(dest[tp * DEST_W:], wt_s, shared_ffn(hn_s), h1_s, mod_s, g_final, eo, bbc_s, ls)

    kf_p, vf_p, logf_p, s_p = st_p
    kf_s, vf_s, logf_s, s_s = st_s
    return (y_p, y_s, kf_p[None], vf_p[None], logf_p[None], s_p[None],
            kf_s[None], vf_s[None], logf_s[None], s_s[None])
```

```python
import functools
import math

import jax
import jax.numpy as jnp
from jax import lax
from jax.experimental import pallas as pl
from jax.experimental.pallas import tpu as pltpu

F32 = jnp.float32
BF16 = jnp.bfloat16
I32 = jnp.int32

CHUNK = 64
H_RET = 8
DK_RET = 128
DV_RET = 256
H_FOX = 16
DH_FOX = 128
TOP_K = 6
ROUTE_SCALE = 2.5
ROPE_BASE = 10000.0
EPS = 1e-6
LOG2E = math.log2(math.e)

LANES = 128
DEST_W = 8
EXPERT_ROWS = 2048
SUB_ROWS = 256
F_TILE = 256
VMEM_LIMIT = 56 * 1024 * 1024


def _params(sem, vmem=None):
    return pltpu.CompilerParams(dimension_semantics=sem, vmem_limit_bytes=vmem)


def _sigmoid(x):
    return 1.0 / (1.0 + jnp.exp(-x))


def _nt_dot(a, b):
    return lax.dot_general(a, b, (((1,), (1,)), ((), ())), preferred_element_type=F32)


def _pack_halves(x):
    n = x.shape[-1] // 2
    hi = pltpu.bitcast(x[:, :n].astype(BF16).astype(F32), I32)
    lo = pltpu.bitcast(x[:, n:].astype(BF16).astype(F32), I32)
    return hi | lax.shift_right_logical(lo, jnp.int32(16))


def _unpack_halves(u):
    hi = pltpu.bitcast(u & jnp.int32(-65536), F32)
    lo = pltpu.bitcast(lax.shift_left(u, jnp.int32(16)), F32)
    return hi, lo


def _ada_kernel(c_ref, w_ref, b_ref, o_ref):
    c = c_ref[...]
    s = (c * _sigmoid(c)).astype(BF16)
    o_ref[...] = jnp.dot(s, w_ref[...].astype(BF16), preferred_element_type=F32) + b_ref[...]


def _adaln(c, w_ada, b_ada):
    nb, d = c.shape
    n = w_ada.shape[1]
    tn = 1024
    return pl.pallas_call(
        _ada_kernel,
        grid=(n // tn,),
        in_specs=[pl.BlockSpec((nb, d), lambda j: (0, 0)),
                  pl.BlockSpec((d, tn), lambda j: (0, j)),
                  pl.BlockSpec((1, tn), lambda j: (0, j))],
        out_specs=pl.BlockSpec((nb, tn), lambda j: (0, j)),
        out_shape=jax.ShapeDtypeStruct((nb, n), F32),
        compiler_params=_params(("arbitrary",), VMEM_LIMIT),
        name="adaln",
    )(c, w_ada, b_ada.reshape(1, n))


def _norm1_kernel(x_ref, g_ref, sh_ref, sc_ref, wf_ref, bf_ref, hn_ref, logf_ref):
    bb, tl, d = x_ref.shape
    x = x_ref[...]
    y = x * lax.rsqrt(jnp.mean(x * x, axis=-1, keepdims=True) + EPS) * g_ref[...]
    hn = (y * (1.0 + sc_ref[...]) + sh_ref[...]).astype(BF16)
    hn_ref[...] = hn
    z = jnp.dot(hn.reshape(bb * tl, d), wf_ref[...], preferred_element_type=F32) + bf_ref[...]
    logf = jnp.minimum(z, 0.0) - jnp.log(1.0 + jnp.exp(-jnp.abs(z)))
    logf_ref[...] = logf[:, :H_FOX].reshape(bb, tl, H_FOX)


def _norm1(x, g, mod, wf, bfg, bb, tl):
    nb, l, d = x.shape
    mod_spec = lambda k: pl.BlockSpec((bb, 1, d), lambda i, j: (i, 0, k))
    return pl.pallas_call(
        _norm1_kernel,
        grid=(nb // bb, l // tl),
        in_specs=[pl.BlockSpec((bb, tl, d), lambda i, j: (i, j, 0)),
                  pl.BlockSpec((1, 1, d), lambda i, j: (0, 0, 0)),
                  mod_spec(0), mod_spec(1),
                  pl.BlockSpec((d, LANES), lambda i, j: (0, 0)),
                  pl.BlockSpec((1, LANES), lambda i, j: (0, 0))],
        out_specs=[pl.BlockSpec((bb, tl, d), lambda i, j: (i, j, 0)),
                   pl.BlockSpec((bb, tl, H_FOX), lambda i, j: (i, j, 0))],
        out_shape=[jax.ShapeDtypeStruct((nb, l, d), BF16),
                   jax.ShapeDtypeStruct((nb, l, H_FOX), F32)],
        compiler_params=_params(("arbitrary", "arbitrary"), VMEM_LIMIT),
        name="norm1",
    )(x, g.reshape(1, 1, d), mod, mod, wf, bfg)


def _mm_kernel(a_ref, w_ref, o_ref):
    o_ref[...] = jnp.dot(a_ref[...], w_ref[...], preferred_element_type=F32).astype(o_ref.dtype)


def _matmul(a, w, out_dtype, tm, tn, name):
    m, k = a.shape
    n = w.shape[1]
    return pl.pallas_call(
        _mm_kernel,
        grid=(m // tm, n // tn),
        in_specs=[pl.BlockSpec((tm, k), lambda i, j: (i, 0)),
                  pl.BlockSpec((k, tn), lambda i, j: (0, j))],
        out_specs=pl.BlockSpec((tm, tn), lambda i, j: (i, j)),
        out_shape=jax.ShapeDtypeStruct((m, n), out_dtype),
        compiler_params=_params(("arbitrary", "arbitrary"), VMEM_LIMIT),
        name=name,
    )(a, w)


def _mm_heads_kernel(a_ref, w_ref, o_ref, ob_ref):
    bb, tl, nh, dh = o_ref.shape
    res = jnp.dot(a_ref[...], w_ref[...], preferred_element_type=F32)
    ob_ref[...] = res.astype(BF16)
    for hh in range(nh):
        o_ref[:, :, hh, :] = res[:, hh * dh:(hh + 1) * dh].reshape(bb, tl, dh)


def _matmul_heads(a, w, nb, l, bb, tl, name):
    m, k = a.shape
    n = w.shape[1]
    nj = l // tl
    return pl.pallas_call(
        _mm_heads_kernel,
        grid=(nb // bb, nj),
        in_specs=[pl.BlockSpec((bb * tl, k), lambda i, j: (i * nj + j, 0)),
                  pl.BlockSpec((k, n), lambda i, j: (0, 0))],
        out_specs=[pl.BlockSpec((bb, tl, H_FOX, DH_FOX), lambda i, j: (i, j, 0, 0)),
                   pl.BlockSpec((bb * tl, n), lambda i, j: (i * nj + j, 0))],
        out_shape=[jax.ShapeDtypeStruct((nb, l, H_FOX, DH_FOX), F32),
                   jax.ShapeDtypeStruct((m, n), BF16)],
        compiler_params=_params(("arbitrary", "arbitrary"), VMEM_LIMIT),
        name=name,
    )(a, w)


def _cumsum_kernel(x_ref, o_ref, carry_ref, *, scale):
    tl = x_ref.shape[-1]

    @pl.when(pl.program_id(1) == 0)
    def _():
        carry_ref[...] = jnp.zeros_like(carry_ref)

    r = lax.broadcasted_iota(I32, (tl, tl), 0)
    c = lax.broadcasted_iota(I32, (tl, tl), 1)
    tri = (r <= c).astype(F32)
    y = jnp.dot(x_ref[0], tri, precision=lax.Precision.HIGHEST, preferred_element_type=F32)
    y = y + carry_ref[:, 0:1]
    o_ref[0] = y * scale
    carry_ref[...] = jnp.broadcast_to(y[:, tl - 1:tl], carry_ref.shape)


def _cumsum_rows(x, tl, scale):
    nb, h, s = x.shape
    return pl.pallas_call(
        functools.partial(_cumsum_kernel, scale=scale),
        grid=(nb, s // tl),
        in_specs=[pl.BlockSpec((1, h, tl), lambda b, t: (b, 0, t))],
        out_specs=pl.BlockSpec((1, h, tl), lambda b, t: (b, 0, t)),
        out_shape=jax.ShapeDtypeStruct((nb, h, s), F32),
        scratch_shapes=[pltpu.VMEM((h, LANES), F32)],
        compiler_params=_params(("arbitrary", "arbitrary"), VMEM_LIMIT),
        name="cumsum_logf",
    )(x)


def _ret_kernel(q_ref, k_ref, v_ref, g_ref, cos_ref, sin_ref, dmat_ref, qdec_ref, kdec_ref, sdec_ref,
                s0_ref, gn_ref, o_ref, sout_ref, s_scr, *, heads):
    t = pl.program_id(2)
    hs = range(heads)

    @pl.when(t == 0)
    def _():
        s_scr[...] = s0_ref[0]

    cos = cos_ref[...]
    sin = sin_ref[...]
    half = DK_RET // 2
    ks = [slice(hh * DK_RET, (hh + 1) * DK_RET) for hh in hs]
    vs = [slice(hh * DV_RET, (hh + 1) * DV_RET) for hh in hs]
    q = [q_ref[0, :, ks[hh]] for hh in hs]
    k = [k_ref[0, :, ks[hh]] for hh in hs]
    qr = [q[hh] * cos + pltpu.roll(q[hh], half, 1) * sin for hh in hs]
    kr = [k[hh] * cos + pltpu.roll(k[hh], half, 1) * sin for hh in hs]
    vb = [v_ref[0, :, vs[hh]] for hh in hs]
    state = [s_scr[hh] for hh in hs]
    s = [_nt_dot(qr[hh].astype(BF16), kr[hh].astype(BF16)) * dmat_ref[hh] for hh in hs]
    o = [jnp.dot(s[hh].astype(BF16), vb[hh], preferred_element_type=F32) for hh in hs]
    o = [o[hh] + jnp.dot((qr[hh] * qdec_ref[hh]).astype(BF16), state[hh].astype(BF16),
                         preferred_element_type=F32) for hh in hs]
    kd_t = [(kr[hh] * kdec_ref[hh]).T.astype(BF16) for hh in hs]
    s_new = [state[hh] * sdec_ref[hh] + jnp.dot(kd_t[hh], vb[hh], preferred_element_type=F32) for hh in hs]
    for hh in hs:
        s_scr[hh] = s_new[hh]

    @pl.when(t == pl.num_programs(2) - 1)
    def _():
        sout_ref[0] = s_scr[...]

    mu = [jnp.mean(o[hh], axis=-1, keepdims=True) for hh in hs]
    oc = [o[hh] - mu[hh] for hh in hs]
    var = [jnp.mean(oc[hh] * oc[hh], axis=-1, keepdims=True) for hh in hs]
    on = [oc[hh] * lax.rsqrt(var[hh] + EPS) * gn_ref[:, vs[hh]] for hh in hs]
    g = [g_ref[0, :, vs[hh]].astype(F32) for hh in hs]
    for hh in hs:
        o_ref[0, :, vs[hh]] = (on[hh] * (g[hh] * _sigmoid(g[hh]))).astype(BF16)


def _retention(zqk, zb, s0, g_ret_gn, pos0, tb, heads=4):
    nb, l, _ = zqk.shape
    nt = l // tb
    half = DK_RET // 2
    inv = 1.0 / (ROPE_BASE ** (jnp.arange(half, dtype=F32) * 2.0 / DK_RET))
    ang = (pos0 + jnp.arange(l)).astype(F32)[:, None] * inv[None, :]
    cos = jnp.concatenate([jnp.cos(ang), jnp.cos(ang)], axis=-1)
    sin = jnp.concatenate([-jnp.sin(ang), jnp.sin(ang)], axis=-1)
    lg = jnp.log1p(-jnp.exp2(-5.0 - jnp.arange(H_RET, dtype=F32)))
    i = jnp.arange(tb)
    d = (i[:, None] - i[None, :]).astype(F32)
    ci, cj = i[:, None] // CHUNK, i[None, :] // CHUNK
    expo = jnp.where(ci == cj, jnp.abs(d), d)
    dmat = jnp.where((cj <= ci)[None], jnp.exp(expo[None] * lg[:, None, None]), 0.0)
    fi = i.astype(F32)
    qdec = jnp.broadcast_to(jnp.exp((fi[None, :] + 1.0) * lg[:, None])[:, :, None], (H_RET, tb, DK_RET))
    kdec = jnp.broadcast_to(jnp.exp((tb - 1.0 - fi[None, :]) * lg[:, None])[:, :, None], (H_RET, tb, DK_RET))
    sdec = jnp.broadcast_to(jnp.exp(tb * lg)[:, None, None], (H_RET, 1, DV_RET))
    ng = H_RET // heads
    kw, vw = heads * DK_RET, heads * DV_RET
    head_spec = lambda shape: pl.BlockSpec((heads,) + shape, lambda b, g, t: (g, 0, 0))
    return pl.pallas_call(
        functools.partial(_ret_kernel, heads=heads),
        grid=(nb, ng, nt),
        in_specs=[pl.BlockSpec((1, tb, kw), lambda b, g, t: (b, t, g)),
                  pl.BlockSpec((1, tb, kw), lambda b, g, t: (b, t, ng + g)),
                  pl.BlockSpec((1, tb, vw), lambda b, g, t: (b, t, g)),
                  pl.BlockSpec((1, tb, vw), lambda b, g, t: (b, t, ng + g)),
                  pl.BlockSpec((tb, DK_RET), lambda b, g, t: (t, 0)),
                  pl.BlockSpec((tb, DK_RET), lambda b, g, t: (t, 0)),
                  head_spec((tb, tb)), head_spec((tb, DK_RET)), head_spec((tb, DK_RET)), head_spec((1, DV_RET)),
                  pl.BlockSpec((1, heads, DK_RET, DV_RET), lambda b, g, t: (b, g, 0, 0)),
                  pl.BlockSpec((1, vw), lambda b, g, t: (0, g))],
        out_specs=[pl.BlockSpec((1, tb, vw), lambda b, g, t: (b, t, g)),
                   pl.BlockSpec((1, heads, DK_RET, DV_RET), lambda b, g, t: (b, g, 0, 0))],
        out_shape=[jax.ShapeDtypeStruct((nb, l, H_RET * DV_RET), BF16),
                   jax.ShapeDtypeStruct((nb, H_RET, DK_RET, DV_RET), F32)],
        scratch_shapes=[pltpu.VMEM((heads, DK_RET, DV_RET), F32)],
        compiler_params=_params(("arbitrary", "arbitrary", "arbitrary"), VMEM_LIMIT),
        name="retention",
    )(zqk, zqk, zb, zb, cos, sin, dmat, qdec, kdec, sdec, s0, g_ret_gn.reshape(1, -1))


def _fox_prompt_kernel(q_ref, k_ref, v_ref, fk_ref, ftok_ref, o_ref, m_scr, l_scr, acc_scr, fq_scr, *, heads, tk):
    g = pl.program_id(1)
    qi = pl.program_id(2)
    tq = q_ref.shape[1]
    nck = tk // LANES
    hs = range(heads)
    sls = [slice(hh * DH_FOX, (hh + 1) * DH_FOX) for hh in hs]
    ft = ftok_ref[0]
    lane = lax.broadcasted_iota(I32, ft.shape, 1)
    for hh in hs:
        m_scr[hh] = jnp.full((tq, LANES), -jnp.inf, F32)
        l_scr[hh] = jnp.zeros((tq, LANES), F32)
        acc_scr[hh] = jnp.zeros((tq, DH_FOX), F32)
        fq = jnp.sum(jnp.where(lane == g * heads + hh, ft, 0.0), axis=-1, keepdims=True)
        fq_scr[hh] = jnp.broadcast_to(fq, (tq, LANES))
    ones = jnp.ones((tk, LANES), BF16)

    def steps(ki, masked):
        rows = pl.ds(pl.multiple_of(ki * tk, tk), tk)
        m_prev = [m_scr[hh] for hh in hs]
        t = [_nt_dot(q_ref[0, :, sls[hh]], k_ref[0, rows, sls[hh]]) for hh in hs]
        t = [t[hh] - fk_ref[0, hh, pl.ds(ki, 1), :] for hh in hs]
        if masked:
            row = lax.broadcasted_iota(I32, (tq, tk), 0)
            col = lax.broadcasted_iota(I32, (tq, tk), 1)
            t = [jnp.where(col <= row, t[hh], -jnp.inf) for hh in hs]
        m_new = [jnp.maximum(m_prev[hh], jnp.max(t[hh], axis=-1, keepdims=True) + fq_scr[hh]) for hh in hs]
        shift = [m_new[hh] - fq_scr[hh] for hh in hs]
        p = [jnp.concatenate([jnp.exp2(t[hh][:, c * LANES:(c + 1) * LANES] - shift[hh]) for c in range(nck)],
                             axis=1).astype(BF16) for hh in hs]
        alpha = [jnp.exp2(m_prev[hh] - m_new[hh]) for hh in hs]
        pv = [jnp.dot(p[hh], jnp.concatenate([v_ref[0, rows, sls[hh]], ones], axis=1),
                      preferred_element_type=F32) for hh in hs]
        for hh in hs:
            l_scr[hh] = alpha[hh] * l_scr[hh] + pv[hh][:, DH_FOX:]
            acc_scr[hh] = alpha[hh] * acc_scr[hh] + pv[hh][:, :DH_FOX]
            m_scr[hh] = m_new[hh]

    def body(ki, carry):
        steps(ki, False)
        return carry

    lax.fori_loop(0, qi, body, 0)
    steps(qi, True)
    for hh in hs:
        o_ref[0, :, sls[hh]] = (acc_scr[hh] / l_scr[hh]).astype(BF16)


def _fox_prompt(zb, kb, vb, f_rows, f_tok, tq, heads=2):
    nb, s, hd = kb.shape
    nq = s // tq
    w = heads * DH_FOX
    q_off = (2 * hd) // w
    kv_spec = pl.BlockSpec((1, s, w), lambda b, g, qi: (b, 0, g))
    return pl.pallas_call(
        functools.partial(_fox_prompt_kernel, heads=heads, tk=tq),
        grid=(nb, H_FOX // heads, nq),
        in_specs=[pl.BlockSpec((1, tq, w), lambda b, g, qi: (b, qi, q_off + g)),
                  kv_spec, kv_spec,
                  pl.BlockSpec((1, heads, nq, tq), lambda b, g, qi: (b, g, 0, 0)),
                  pl.BlockSpec((1, tq, H_FOX), lambda b, g, qi: (b, qi, 0))],
        out_specs=pl.BlockSpec((1, tq, w), lambda b, g, qi: (b, qi, g)),
        out_shape=jax.ShapeDtypeStruct((nb, s, hd), BF16),
        scratch_shapes=[pltpu.VMEM((heads, tq, LANES), F32), pltpu.VMEM((heads, tq, LANES), F32),
                        pltpu.VMEM((heads, tq, DH_FOX), F32), pltpu.VMEM((heads, tq, LANES), F32)],
        compiler_params=_params(("arbitrary",) * 3, VMEM_LIMIT),
        name="fox_prompt",
    )(zb, kb, vb, f_rows.reshape(nb, H_FOX, nq, tq), f_tok)


def _fox_sample_kernel(q_ref, ck_ref, cv_ref, kn_ref, vn_ref, fk_ref, ftok_ref, o_ref, *, heads):
    g = pl.program_id(1)
    l = q_ref.shape[1]
    p_len = ck_ref.shape[2]
    ft = ftok_ref[0]
    lane = lax.broadcasted_iota(I32, ft.shape, 1)
    row = lax.broadcasted_iota(I32, (l, l), 0)
    col = lax.broadcasted_iota(I32, (l, l), 1)
    for hh in range(heads):
        hg = g * heads + hh
        sl = slice(hh * DH_FOX, (hh + 1) * DH_FOX)
        fk = fk_ref[0, pl.ds(hg, 1), :]
        fq = jnp.sum(jnp.where(lane == hg, ft, 0.0), axis=-1, keepdims=True)
        qh = q_ref[0, :, sl]
        s1 = _nt_dot(qh, ck_ref[0, 0, :, hh, :].astype(BF16)) + fq - fk[:, :p_len]
        s2 = _nt_dot(qh, kn_ref[0, :, sl]) + fq - fk[:, p_len:p_len + l]
        s2 = jnp.where(col <= row, s2, -jnp.inf)
        m = jnp.maximum(jnp.max(s1, axis=-1, keepdims=True), jnp.max(s2, axis=-1, keepdims=True))
        p1 = jnp.exp2(s1 - m)
        p2 = jnp.exp2(s2 - m)
        den = jnp.sum(p1, axis=-1, keepdims=True) + jnp.sum(p2, axis=-1, keepdims=True)
        o = jnp.dot(p1.astype(BF16), cv_ref[0, 0, :, hh, :].astype(BF16), preferred_element_type=F32)
        o = o + jnp.dot(p2.astype(BF16), vn_ref[0, :, sl], preferred_element_type=F32)
        o_ref[0, :, sl] = (o / den).astype(BF16)


def _fox_sample(zb, kb, vb, cache_k, cache_v, f_rows, f_tok, heads=8):
    nb, l, hd = kb.shape
    p_len = cache_k.shape[2]
    sp = f_rows.shape[-1]
    w = heads * DH_FOX
    q_off = (2 * hd) // w
    assert p_len % l == 0
    new_spec = pl.BlockSpec((1, l, w), lambda b, g: (b, 0, g))
    cache_spec = pl.BlockSpec((1, 1, p_len, heads, DH_FOX), lambda b, g: (0, b, 0, g, 0))
    return pl.pallas_call(
        functools.partial(_fox_sample_kernel, heads=heads),
        grid=(nb, H_FOX // heads),
        in_specs=[pl.BlockSpec((1, l, w), lambda b, g: (b, 0, q_off + g)),
                  cache_spec, cache_spec, new_spec, new_spec,
                  pl.BlockSpec((1, H_FOX, sp), lambda b, g: (b, 0, 0)),
                  pl.BlockSpec((1, l, H_FOX), lambda b, g: (b, p_len // l, 0))],
        out_specs=new_spec,
        out_shape=jax.ShapeDtypeStruct((nb, l, hd), BF16),
        compiler_params=_params(("arbitrary", "arbitrary"), VMEM_LIMIT),
        name="fox_sample",
    )(zb, cache_k, cache_v, kb, vb, f_rows, f_tok)


def _mix1_kernel(or_ref, of_ref, w1_ref, w2_ref, gr_ref, gf_ref, o_ref):
    a = jnp.dot(or_ref[...], w1_ref[...], preferred_element_type=F32)
    b = jnp.dot(of_ref[...], w2_ref[...], preferred_element_type=F32)
    gr = _sigmoid(gr_ref[...].astype(F32))
    gf = _sigmoid(gf_ref[...].astype(F32))
    o_ref[...] = (gr * a + gf * b).astype(BF16)


def _mix1(o_r, o_f, w1, w2, zb, tm, tn):
    m, d = o_r.shape
    gr_off = (3 * d) // tn
    gf_off = (4 * d) // tn
    return pl.pallas_call(
        _mix1_kernel,
        grid=(m // tm, d // tn),
        in_specs=[pl.BlockSpec((tm, d), lambda i, j: (i, 0)),
                  pl.BlockSpec((tm, d), lambda i, j: (i, 0)),
                  pl.BlockSpec((d, tn), lambda i, j: (0, j)),
                  pl.BlockSpec((d, tn), lambda i, j: (0, j)),
                  pl.BlockSpec((tm, tn), lambda i, j: (i, gr_off + j)),
                  pl.BlockSpec((tm, tn), lambda i, j: (i, gf_off + j))],
        out_specs=pl.BlockSpec((tm, tn), lambda i, j: (i, j)),
        out_shape=jax.ShapeDtypeStruct((m, d), BF16),
        compiler_params=_params(("arbitrary", "arbitrary"), VMEM_LIMIT),
        name="mix_gates",
    )(o_r, o_f, w1, w2, zb, zb)


def _mix2_kernel(m_ref, w_ref, x_ref, gt_ref, sh_ref, sc_ref, g_ref, wrt_ref, br_ref, cnt0_ref,
                 h1_ref, hn_ref, eidx_ref, rank_ref, wts_ref, cnt_ref):
    bb, tl, d = x_ref.shape
    tm = bb * tl
    ne = wrt_ref.shape[0]

    @pl.when((pl.program_id(0) == 0) & (pl.program_id(1) == 0))
    def _():
        cnt_ref[...] = cnt0_ref[...]

    mix = jnp.dot(m_ref[...].reshape(tm, d), w_ref[...], preferred_element_type=F32)
    h1 = x_ref[...] + gt_ref[...] * mix.reshape(bb, tl, d)
    h1_ref[...] = h1
    y = h1 * lax.rsqrt(jnp.mean(h1 * h1, axis=-1, keepdims=True) + EPS) * g_ref[...]
    hn = (y * (1.0 + sc_ref[...]) + sh_ref[...]).reshape(tm, d)
    hn_ref[...] = _pack_halves(hn)

    logits = lax.dot_general(wrt_ref[...], hn, (((1,), (1,)), ((), ())), precision=lax.Precision.HIGHEST,
                             preferred_element_type=F32)
    scores = _sigmoid(logits)
    cur = scores + br_ref[...]
    eid = lax.broadcasted_iota(I32, (ne, tm), 0).astype(F32)
    slot = lax.broadcasted_iota(I32, (DEST_W, tm), 0)
    r = lax.broadcasted_iota(I32, (tm, tm), 0)
    c = lax.broadcasted_iota(I32, (tm, tm), 1)
    earlier = (r < c).astype(BF16)
    picks = []
    mask = jnp.zeros((ne, tm), F32)
    for _ in range(TOP_K):
        mx = jnp.max(cur, axis=0, keepdims=True)
        idx = jnp.min(jnp.where(cur == mx, eid, float(ne)), axis=0, keepdims=True)
        pick = eid == idx
        picks.append((idx, pick))
        mask = jnp.where(pick, 1.0, mask)
        cur = jnp.where(pick, -jnp.inf, cur)
    rank = jnp.dot(mask.astype(BF16), earlier, preferred_element_type=F32) + cnt_ref[:, 0:1]
    cnt_ref[...] = cnt_ref[...] + jnp.sum(mask, axis=1, keepdims=True)
    wsum = jnp.sum(mask * scores, axis=0, keepdims=True)
    eidx = jnp.zeros((DEST_W, tm), I32)
    rnk = jnp.zeros((DEST_W, tm), I32)
    wts = jnp.zeros((DEST_W, tm), F32)
    for kk, (idx, pick) in enumerate(picks):
        sc_k = jnp.sum(jnp.where(pick, scores, 0.0), axis=0, keepdims=True)
        rk_k = jnp.sum(jnp.where(pick, rank, 0.0), axis=0, keepdims=True)
        eidx = jnp.where(slot == kk, idx.astype(I32), eidx)
        rnk = jnp.where(slot == kk, rk_k.astype(I32), rnk)
        wts = jnp.where(slot == kk, sc_k / wsum * ROUTE_SCALE, wts)
    eidx_ref[...] = eidx
    rank_ref[...] = rnk
    wts_ref[...] = wts


def _mix2(m3, w_out, x, mod, g2, w_router, b_router, cnt0, bb, tl):
    nb, l, d = x.shape
    ne = w_router.shape[1]
    t = nb * l
    tm = bb * tl
    nj = l // tl
    mod_spec = lambda k: pl.BlockSpec((bb, 1, d), lambda i, j: (i, 0, k))
    slot_spec = pl.BlockSpec((DEST_W, tm), lambda i, j: (0, i * nj + j))
    blk3 = pl.BlockSpec((bb, tl, d), lambda i, j: (i, j, 0))
    return pl.pallas_call(
        _mix2_kernel,
        grid=(nb // bb, nj),
        in_specs=[blk3,
                  pl.BlockSpec((d, d), lambda i, j: (0, 0), pipeline_mode=pl.Buffered(1)),
                  blk3,
                  mod_spec(2), mod_spec(3), mod_spec(4),
                  pl.BlockSpec((1, 1, d), lambda i, j: (0, 0, 0)),
                  pl.BlockSpec((ne, d), lambda i, j: (0, 0)),
                  pl.BlockSpec((ne, 1), lambda i, j: (0, 0)),
                  pl.BlockSpec((ne, LANES), lambda i, j: (0, 0))],
        out_specs=[blk3, pl.BlockSpec((tm, d // 2), lambda i, j: (i * nj + j, 0)), slot_spec, slot_spec, slot_spec,
                   pl.BlockSpec((ne, LANES), lambda i, j: (0, 0))],
        out_shape=[jax.ShapeDtypeStruct((nb, l, d), F32),
                   jax.ShapeDtypeStruct((t, d // 2), I32),
                   jax.ShapeDtypeStruct((DEST_W, t), I32),
                   jax.ShapeDtypeStruct((DEST_W, t), I32),
                   jax.ShapeDtypeStruct((DEST_W, t), F32),
                   jax.ShapeDtypeStruct((ne, LANES), F32)],
        compiler_params=_params(("arbitrary", "arbitrary"), VMEM_LIMIT),
        name="outproj_norm2_router",
    )(m3, w_out, x, mod, mod, mod, g2.reshape(1, 1, d), w_router.T, b_router.reshape(ne, 1), cnt0)


def _dispatch_kernel(zs_ref, zn_ref, dest_ref, xp_ref, xs_ref, buf_ref, zero_scr, sem, *, np_blocks):
    i = pl.program_id(0)
    tb = xp_ref.shape[0]

    def scatter(x_ref):
        def row_copy(r, kk):
            d = dest_ref[r * DEST_W + kk]
            return pltpu.make_async_copy(x_ref.at[pl.ds(r, 1), :], buf_ref.at[pl.ds(d, 1), :], sem)

        def issue(r, carry):
            for kk in range(TOP_K):
                row_copy(r, kk).start()
            return carry

        def drain(r, carry):
            for kk in range(TOP_K):
                row_copy(r, kk).wait()
            return carry

        lax.fori_loop(0, tb, issue, 0)
        lax.fori_loop(0, tb, drain, 0)

    @pl.when(i < np_blocks)
    def _():
        scatter(xp_ref)

    @pl.when(i >= np_blocks)
    def _():
        scatter(xs_ref)

    @pl.when(i == pl.num_programs(0) - 1)
    def _():
        zero_scr[...] = jnp.zeros_like(zero_scr)

        def per_expert(e, carry):
            start = zs_ref[e]

            def zero_copy(r):
                return pltpu.make_async_copy(zero_scr.at[pl.ds(0, 1), :], buf_ref.at[pl.ds(start + r, 1), :], sem)

            def issue(r, c):
                zero_copy(r).start()
                return c

            def drain(r, c):
                zero_copy(r).wait()
                return c

            lax.fori_loop(0, zn_ref[e], issue, 0)
            lax.fori_loop(0, zn_ref[e], drain, 0)
            return carry

        lax.fori_loop(0, zs_ref.shape[0], per_expert, 0)


def _dispatch(xp, xs, dest, zero_start, zero_n, rows, tb):
    tp, w = xp.shape
    ts = xs.shape[0]
    npb = tp // tb
    grid_spec = pltpu.PrefetchScalarGridSpec(
        num_scalar_prefetch=2,
        grid=(npb + ts // tb,),
        in_specs=[pl.BlockSpec((tb * DEST_W,), lambda i, zs, zn: (i,), memory_space=pltpu.SMEM),
                  pl.BlockSpec((tb, w), lambda i, zs, zn: (jnp.minimum(i, npb - 1), 0)),
                  pl.BlockSpec((tb, w), lambda i, zs, zn: (jnp.maximum(i - npb, 0), 0))],
        out_specs=pl.BlockSpec(memory_space=pl.ANY),
        scratch_shapes=[pltpu.VMEM((8, w), I32), pltpu.SemaphoreType.DMA(())],
    )
    return pl.pallas_call(
        functools.partial(_dispatch_kernel, np_blocks=npb),
        grid_spec=grid_spec,
        out_shape=jax.ShapeDtypeStruct((rows, w), I32),
        compiler_params=_params(("arbitrary",), VMEM_LIMIT),
        name="moe_dispatch",
    )(zero_start, zero_n, dest, xp, xs)


def _expert_kernel(be_ref, nv_ref, nu_ref, x_ref, wg_ref, wu_ref, wgt_ref, wut_ref, wda_ref, wdb_ref, o_ref,
                   xb_scr, hid_scr, wgb_scr, wub_scr, wdn_scr, *, n_main):
    del be_ref, nu_ref
    i = pl.program_id(0)
    j = pl.program_id(1)
    half = x_ref.shape[1]
    ft = wg_ref.shape[2]
    tail = wgt_ref.shape[2]
    ct = wda_ref.shape[2]
    nv = nv_ref[i]
    big = 4 * SUB_ROWS
    n_big = nv // 4
    rem = nv % 4

    def for_chunks(fn):
        def body(c, carry):
            fn(pl.multiple_of(c * big, big), big)
            return carry

        lax.fori_loop(0, n_big, body, 0)

        @pl.when(rem >= 2)
        def _():
            fn(pl.multiple_of(n_big * big, SUB_ROWS), 2 * SUB_ROWS)

        @pl.when(rem % 2 == 1)
        def _():
            fn(pl.multiple_of((nv - 1) * SUB_ROWS, SUB_ROWS), SUB_ROWS)

    @pl.when(nv > 0)
    def _():
        @pl.when(j == 0)
        def _():
            def unpack(r0, rows):
                hi, lo = _unpack_halves(x_ref[pl.ds(r0, rows), :])
                xb_scr[pl.ds(r0, rows), :half] = hi.astype(BF16)
                xb_scr[pl.ds(r0, rows), half:] = lo.astype(BF16)

            for_chunks(unpack)

        def up_phase(wg, wu, width):
            wgb_scr[:, :width] = wg[0].astype(BF16)
            wub_scr[:, :width] = wu[0].astype(BF16)

            def chunk(r0, rows):
                xb = xb_scr[pl.ds(r0, rows), :]
                gate = jnp.dot(xb, wgb_scr[:, :width], preferred_element_type=F32)
                up = jnp.dot(xb, wub_scr[:, :width], preferred_element_type=F32)
                hid_scr[j, pl.ds(r0, rows), :width] = (gate * _sigmoid(gate) * up).astype(BF16)

            for_chunks(chunk)

        @pl.when(j < n_main)
        def _():
            up_phase(wg_ref, wu_ref, ft)

        @pl.when(j == n_main)
        def _():
            wgb_scr[:, :tail] = wgt_ref[0].astype(BF16)
            wgb_scr[:, tail:2 * tail] = wut_ref[0].astype(BF16)

            def chunk(r0, rows):
                gu = jnp.dot(xb_scr[pl.ds(r0, rows), :], wgb_scr[:, :2 * tail], preferred_element_type=F32)
                gate, up = gu[:, :tail], gu[:, tail:]
                hid_scr[j, pl.ds(r0, rows), :tail] = (gate * _sigmoid(gate) * up).astype(BF16)

            for_chunks(chunk)

        @pl.when(j > n_main)
        def _():
            wdn_scr[:, :ct] = wda_ref[0].astype(BF16)
            wdn_scr[:, ct:] = wdb_ref[0].astype(BF16)

            def chunk(r0, rows):
                pieces = [hid_scr[t, pl.ds(r0, rows), :] for t in range(n_main)]
                pieces.append(hid_scr[n_main, pl.ds(r0, rows), :tail])
                hid = jnp.concatenate(pieces, axis=1)
                res = jnp.dot(hid, wdn_scr[...], preferred_element_type=F32)
                o_ref[pl.ds(r0, rows), :] = _pack_halves(res)

            for_chunks(chunk)

    @pl.when(j > n_main)
    def _():
        def body(c, carry):
            r0 = pl.multiple_of(c * SUB_ROWS, SUB_ROWS)
            o_ref[pl.ds(r0, SUB_ROWS), :] = jnp.zeros((SUB_ROWS, o_ref.shape[1]), I32)
            return carry

        lax.fori_loop(nv, o_ref.shape[0] // SUB_ROWS, body, 0)


def _expert_ffn(xp, wg, wu, wd, block_expert, n_valid, n_used, rb):
    rows, half = xp.shape
    d = 2 * half
    f = wg.shape[2]
    n_main = f // F_TILE
    tail = f - n_main * F_TILE
    assert tail > 0 and tail % LANES == 0 and (n_main * F_TILE) % tail == 0
    tail_blk = (n_main * F_TILE) // tail
    ct = F_TILE
    n_out = half // ct
    nblk = rows // rb

    def main_col(i, j, be, nv, nu):
        return (be[i], 0, jnp.minimum(j, n_main - 1))

    def out_col(j):
        return jnp.clip(j - (n_main + 1), 0, n_out - 1)

    grid_spec = pltpu.PrefetchScalarGridSpec(
        num_scalar_prefetch=3,
        grid=(nblk, n_main + 1 + n_out),
        in_specs=[pl.BlockSpec((rb, half), lambda i, j, be, nv, nu: (jnp.minimum(i, nu[0] - 1), 0),
                               pipeline_mode=pl.Buffered(1)),
                  pl.BlockSpec((1, d, F_TILE), main_col),
                  pl.BlockSpec((1, d, F_TILE), main_col),
                  pl.BlockSpec((1, d, tail), lambda i, j, be, nv, nu: (be[i], 0, tail_blk)),
                  pl.BlockSpec((1, d, tail), lambda i, j, be, nv, nu: (be[i], 0, tail_blk)),
                  pl.BlockSpec((1, f, ct), lambda i, j, be, nv, nu: (be[i], 0, out_col(j))),
                  pl.BlockSpec((1, f, ct), lambda i, j, be, nv, nu: (be[i], 0, n_out + out_col(j)))],
        out_specs=pl.BlockSpec((rb, ct), lambda i, j, be, nv, nu: (i, out_col(j))),
        scratch_shapes=[pltpu.VMEM((rb, d), BF16), pltpu.VMEM((n_main + 1, rb, F_TILE), BF16),
                        pltpu.VMEM((d, F_TILE), BF16), pltpu.VMEM((d, F_TILE), BF16),
                        pltpu.VMEM((f, 2 * ct), BF16)],
    )
    return pl.pallas_call(
        functools.partial(_expert_kernel, n_main=n_main),
        grid_spec=grid_spec,
        out_shape=jax.ShapeDtypeStruct((rows, half), I32),
        compiler_params=_params(("arbitrary", "arbitrary"), VMEM_LIMIT),
        name="expert_ffn",
    )(block_expert, n_valid, n_used, xp, wg, wu, wg, wu, wd, wd)


def _combine_kernel(dcur_ref, dnext_ref, wts_ref, sh_ref, h1_ref, gt_ref, g_ref, eo_ref, y_ref,
                    gbuf, sems):
    bb, tl, d = h1_ref.shape
    tb = bb * tl
    half = d // 2
    nj = pl.num_programs(1)
    step = pl.program_id(0) * nj + pl.program_id(1)
    nsteps = pl.num_programs(0) * nj
    slot = step % 2

    def row_copy(dref, sl, r, kk):
        dd = dref[r * DEST_W + kk]
        return pltpu.make_async_copy(eo_ref.at[pl.ds(dd, 1), :], gbuf.at[sl, kk, pl.ds(r, 1), :],
                                     sems.at[sl])

    def issue(dref, sl):
        def body(r, carry):
            for kk in range(TOP_K):
                row_copy(dref, sl, r, kk).start()
            return carry
        lax.fori_loop(0, tb, body, 0)

    @pl.when(step == 0)
    def _():
        issue(dcur_ref, 0)

    @pl.when(step + 1 < nsteps)
    def _():
        issue(dnext_ref, 1 - slot)

    def drain(r, carry):
        for kk in range(TOP_K):
            row_copy(dcur_ref, slot, r, kk).wait()
        return carry
    lax.fori_loop(0, tb, drain, 0)

    wts = wts_ref[...]
    s_hi, s_lo = _unpack_halves(sh_ref[...])
    for kk in range(TOP_K):
        hi, lo = _unpack_halves(gbuf[slot, kk])
        wk = wts[:, kk:kk + 1]
        s_hi = s_hi + wk * hi
        s_lo = s_lo + wk * lo
    h2_hi = h1_ref[:, :, :half] + gt_ref[:, :, :half] * s_hi.reshape(bb, tl, half)
    h2_lo = h1_ref[:, :, half:] + gt_ref[:, :, half:] * s_lo.reshape(bb, tl, half)
    ms = (jnp.sum(h2_hi * h2_hi, axis=-1, keepdims=True)
          + jnp.sum(h2_lo * h2_lo, axis=-1, keepdims=True)) / d
    inv = lax.rsqrt(ms + EPS)
    y_ref[:, :, :half] = h2_hi * inv * g_ref[:, :, :half]
    y_ref[:, :, half:] = h2_lo * inv * g_ref[:, :, half:]


def _combine(dest, wts, shared, h1, mod, g_final, eo, bb, tl):
    nb, l, d = h1.shape
    tb = bb * tl
    nj = l // tl
    nsteps = (nb // bb) * nj
    lin = lambda i, j: i * nj + j
    return pl.pallas_call(
        _combine_kernel,
        grid=(nb // bb, nj),
        in_specs=[pl.BlockSpec((tb * DEST_W,), lambda i, j: (lin(i, j),), memory_space=pltpu.SMEM),
                  pl.BlockSpec((tb * DEST_W,), lambda i, j: (jnp.minimum(lin(i, j) + 1, nsteps - 1),),
                               memory_space=pltpu.SMEM),
                  pl.BlockSpec((tb, LANES), lambda i, j: (lin(i, j), 0)),
                  pl.BlockSpec((tb, d // 2), lambda i, j: (lin(i, j), 0)),
                  pl.BlockSpec((bb, tl, d), lambda i, j: (i, j, 0)),
                  pl.BlockSpec((bb, 1, d), lambda i, j: (i, 0, 5)),
                  pl.BlockSpec((1, 1, d), lambda i, j: (0, 0, 0)),
                  pl.BlockSpec(memory_space=pl.ANY)],
        out_specs=pl.BlockSpec((bb, tl, d), lambda i, j: (i, j, 0)),
        out_shape=jax.ShapeDtypeStruct((nb, l, d), F32),
        scratch_shapes=[pltpu.VMEM((2, TOP_K, tb, d // 2), I32), pltpu.SemaphoreType.DMA((2,))],
        compiler_params=_params(("arbitrary", "arbitrary"), VMEM_LIMIT),
        name="moe_combine",
    )(dest, dest, wts, shared, h1, mod, g_final.reshape(1, 1, d), eo)


def _prep_w_in(w_in):
    d = w_in.shape[0]
    qr_w, vr_w, qf_w = H_RET * DK_RET, H_RET * DV_RET, H_FOX * DH_FOX
    sizes = (qr_w, qr_w, vr_w, vr_w, qf_w, qf_w, qf_w, H_FOX, d, d)
    offs = [0]
    for s in sizes:
        offs.append(offs[-1] + s)
    part = lambda n: w_in[:, offs[n]:offs[n + 1]]
    w_qk = jnp.concatenate([part(0), part(1) * (DK_RET ** -0.5)], axis=1).astype(BF16)
    w_b = jnp.concatenate([part(2), part(3), part(4) * (DH_FOX ** -0.5 * LOG2E), part(8), part(9)], axis=1).astype(BF16)
    w_kf = part(5).astype(BF16)
    w_vf = part(6).astype(BF16)
    w_fl = jnp.pad(part(7), ((0, 0), (0, LANES - H_FOX))).astype(BF16)
    return w_qk, w_b, w_kf, w_vf, w_fl


def _trunk_front(x, mod, lw, bb, tl, tm, fox, s0, pos0, ret_tb):
    nb, l, d = x.shape
    t = nb * l
    hn, logf = _norm1(x, lw["g_norm1"], mod, lw["w_fl"], lw["b_fl"], bb, tl)
    hn2d = hn.reshape(t, d)
    zqk = _matmul(hn2d, lw["w_qk"], F32, tm, 512, "proj_qk_ret")
    kf, kb = _matmul_heads(hn2d, lw["w_kf"], nb, l, bb, tl, "proj_k_fox")
    vf, vb = _matmul_heads(hn2d, lw["w_vf"], nb, l, bb, tl, "proj_v_fox")
    zb = _matmul(hn2d, lw["w_b"], BF16, tm, 512, "proj_rest")
    zb3 = zb.reshape(nb, l, -1)
    o_r, s_new = _retention(zqk.reshape(nb, l, -1), zb3, s0, lw["g_ret_gn"], pos0, ret_tb)
    o_f = fox(zb3, kb.reshape(nb, l, -1), vb.reshape(nb, l, -1), logf)
    m = _mix1(o_r.reshape(t, d), o_f.reshape(t, d), lw["w_ret_out"], lw["w_fox_out"], zb, tm, 512)
    return m.reshape(nb, l, d), (kf, vf, logf, s_new)


def kernel(x_prompt, x_sample, c_prompt, c_sample, cache_fox_k, cache_fox_v, cache_fox_logf, state_ret, w_ada, b_ada, g_norm1, w_in, b_forget, g_ret_gn, w_ret_out, w_fox_out, w_out, g_norm2, w_router, b_router, w_exp_gate, w_exp_up, w_exp_down, w_sh_gate, w_sh_up, w_sh_down, g_final):
    depth = w_in.shape[0]
    assert depth == 1
    nbp, s, d = x_prompt.shape
    nbs, ls, _ = x_sample.shape
    p_len = cache_fox_k.shape[2]
    ne = w_router.shape[-1]
    tp, ts = nbp * s, nbs * ls

    nb_all = nbp + nbs
    nb_pad = -(-nb_all // 8) * 8
    c_all = jnp.pad(jnp.concatenate([c_prompt, c_sample], axis=0), ((0, nb_pad - nb_all), (0, 0)))
    mod_all = _adaln(c_all, w_ada[0], b_ada[0])
    mod_p = mod_all[:nbp].reshape(nbp, 1, 6 * d)
    mod_s = mod_all[nbp:nb_all].reshape(nbs, 1, 6 * d)

    w_qk, w_b, w_kf, w_vf, w_fl = _prep_w_in(w_in[0])
    lw = dict(g_norm1=g_norm1[0], w_fl=w_fl,
              b_fl=jnp.pad(b_forget[0], (0, LANES - H_FOX)).reshape(1, LANES),
              w_qk=w_qk, w_b=w_b, w_kf=w_kf, w_vf=w_vf, g_ret_gn=g_ret_gn[0],
              w_ret_out=w_ret_out[0].astype(BF16), w_fox_out=w_fox_out[0].astype(BF16))
    w_out_b = w_out[0].astype(BF16)

    tl_p = min(s, 512)
    tm_p = min(tp, 1024)
    bb_s = max(1, min(nbs, 512 // ls))
    tm_s = min(ts, 1024)

    def fox_p(zb3, kb3, vb3, logf):
        f_rows = _cumsum_rows(logf.swapaxes(1, 2), min(s, 512), LOG2E)
        return _fox_prompt(zb3, kb3, vb3, f_rows, f_rows.swapaxes(1, 2), min(s, 1024))

    def fox_s(zb3, kb3, vb3, logf):
        full = jnp.concatenate([cache_fox_logf[0].astype(F32), logf], axis=1)
        sp = -(-(p_len + ls) // LANES) * LANES
        full = jnp.pad(full, ((0, 0), (0, sp - p_len - ls), (0, 0)))
        f_rows = _cumsum_rows(full.swapaxes(1, 2), sp, LOG2E)
        return _fox_sample(zb3, kb3, vb3, cache_fox_k, cache_fox_v, f_rows, f_rows.swapaxes(1, 2))

    s0_p = jnp.zeros((nbp, H_RET, DK_RET, DV_RET), F32)
    m_p, st_p = _trunk_front(x_prompt, mod_p, lw, 1, tl_p, tm_p, fox_p, s0_p, 0, min(s, 256))
    m_s, st_s = _trunk_front(x_sample, mod_s, lw, bb_s, ls, tm_s, fox_s, state_ret[0], p_len, ls)

    cnt0 = jnp.zeros((ne, LANES), F32)
    tl2_p = min(s, 512)
    bb2_s = max(1, min(nbs, 512 // ls))
    h1_p, hn_p, e_p, r_p, wt_p, cnt_p = _mix2(m_p, w_out_b, x_prompt, mod_p, g_norm2[0], w_router[0],
                                              b_router[0], cnt0, 1, tl2_p)
    h1_s, hn_s, e_s, r_s, wt_s, cnt_s = _mix2(m_s, w_out_b, x_sample, mod_s, g_norm2[0], w_router[0],
                                              b_router[0], cnt_p, bb2_s, ls)

    counts = cnt_s[:, 0].astype(I32)
    rblk = EXPERT_ROWS
    padded = (counts + rblk - 1) // rblk * rblk
    pad_end = jnp.cumsum(padded)
    pad_start = pad_end - padded
    t_all = tp + ts
    nblk = (t_all * TOP_K + ne * (rblk - 1) + rblk - 1) // rblk
    rows = nblk * rblk
    blk_start = jnp.arange(nblk, dtype=I32) * rblk
    block_expert = jnp.minimum(jnp.searchsorted(pad_end, blk_start, side="right"), ne - 1).astype(I32)
    n_used = (pad_end[-1] // rblk).astype(I32).reshape(1)
    valid_rows = jnp.clip(pad_start[block_expert] + counts[block_expert] - blk_start, 0, rblk)
    valid_rows = jnp.where(blk_start < pad_end[-1], valid_rows, 0)
    n_valid = ((valid_rows + SUB_ROWS - 1) // SUB_ROWS).astype(I32)
    zero_start = (pad_start + counts).astype(I32)
    zero_n = ((-counts) % SUB_ROWS).astype(I32)
    e_all = jnp.concatenate([e_p, e_s], axis=1)
    r_all = jnp.concatenate([r_p, r_s], axis=1)
    dest = (pad_start[e_all] + r_all).astype(I32).T.reshape(-1)
    tok_major = lambda w: jnp.pad(w.T, ((0, 0), (0, LANES - DEST_W)))

    buf = _dispatch(hn_p, hn_s, dest, zero_start, zero_n, rows, min(tp, ts, 256))
    eo = _expert_ffn(buf, w_exp_gate[0], w_exp_up[0], w_exp_down[0], block_expert, n_valid, n_used, rblk)

    def shared_ffn(hn):
        rb = min(hn.shape[0], rblk)
        nb_ = hn.shape[0] // rb
        return _expert_ffn(hn, w_sh_gate, w_sh_up, w_sh_down, jnp.zeros((nb_,), I32),
                           jnp.full((nb_,), rb // SUB_ROWS, I32), jnp.full((1,), nb_, I32), rb)

    tlc_p = min(s, 128)
    bbc_s = max(1, min(nbs, 128 // ls))
    y_p = _combine(dest[:tp * DEST_W], tok_major(wt_p), shared_ffn(hn_p), h1_p, mod_p, g_final, eo, 1, tlc_p)
    y_s = _combine(dest[tp * DEST_W:], tok_major(wt_s), shared_ffn(hn_s), h1_s, mod_s, g_final, eo, bbc_s, ls)

    kf_p, vf_p, logf_p, s_p = st_p
    kf_s, vf_s, logf_s, s_s = st_s
    return (y_p, y_s, kf_p[None], vf_p[None], logf_p[None], s_p[None],
            kf_s[None], vf_s[None], logf_s[None], s_s[None])
```

```python
import functools
import math

import jax
import jax.numpy as jnp
from jax import lax
from jax.experimental import pallas as pl
from jax.experimental.pallas import tpu as pltpu

F32 = jnp.float32
BF16 = jnp.bfloat16
I32 = jnp.int32

CHUNK = 64
H_RET = 8
DK_RET = 128
DV_RET = 256
H_FOX = 16
DH_FOX = 128
TOP_K = 6
ROUTE_SCALE = 2.5
ROPE_BASE = 10000.0
EPS = 1e-6
LOG2E = math.log2(math.e)

LANES = 128
DEST_W = 8
EXPERT_ROWS = 2048
SUB_ROWS = 256
F_TILE = 256
VMEM_LIMIT = 56 * 1024 * 1024


def _params(sem, vmem=None):
    return pltpu.CompilerParams(dimension_semantics=sem, vmem_limit_bytes=vmem)


def _sigmoid(x):
    return 1.0 / (1.0 + jnp.exp(-x))


def _nt_dot(a, b):
    return lax.dot_general(a, b, (((1,), (1,)), ((), ())), preferred_element_type=F32)


def _pack_halves(x):
    n = x.shape[-1] // 2
    hi = pltpu.bitcast(x[:, :n].astype(BF16).astype(F32), I32)
    lo = pltpu.bitcast(x[:, n:].astype(BF16).astype(F32), I32)
    return hi | lax.shift_right_logical(lo, jnp.int32(16))


def _unpack_halves(u):
    hi = pltpu.bitcast(u & jnp.int32(-65536), F32)
    lo = pltpu.bitcast(lax.shift_left(u, jnp.int32(16)), F32)
    return hi, lo


def _ada_kernel(c_ref, w_ref, b_ref, o_ref):
    c = c_ref[...]
    s = (c * _sigmoid(c)).astype(BF16)
    o_ref[...] = jnp.dot(s, w_ref[...].astype(BF16), preferred_element_type=F32) + b_ref[...]


def _adaln(c, w_ada, b_ada):
    nb, d = c.shape
    n = w_ada.shape[1]
    tn = 1024
    return pl.pallas_call(
        _ada_kernel,
        grid=(n // tn,),
        in_specs=[pl.BlockSpec((nb, d), lambda j: (0, 0)),
                  pl.BlockSpec((d, tn), lambda j: (0, j)),
                  pl.BlockSpec((1, tn), lambda j: (0, j))],
        out_specs=pl.BlockSpec((nb, tn), lambda j: (0, j)),
        out_shape=jax.ShapeDtypeStruct((nb, n), F32),
        compiler_params=_params(("arbitrary",), VMEM_LIMIT),
        name="adaln",
    )(c, w_ada, b_ada.reshape(1, n))


def _norm1_kernel(x_ref, g_ref, sh_ref, sc_ref, wf_ref, bf_ref, hn_ref, logf_ref):
    bb, tl, d = x_ref.shape
    x = x_ref[...]
    y = x * lax.rsqrt(jnp.mean(x * x, axis=-1, keepdims=True) + EPS) * g_ref[...]
    hn = (y * (1.0 + sc_ref[...]) + sh_ref[...]).astype(BF16)
    hn_ref[...] = hn
    z = jnp.dot(hn.reshape(bb * tl, d), wf_ref[...], preferred_element_type=F32) + bf_ref[...]
    logf = jnp.minimum(z, 0.0) - jnp.log(1.0 + jnp.exp(-jnp.abs(z)))
    logf_ref[...] = logf[:, :H_FOX].reshape(bb, tl, H_FOX)


def _norm1(x, g, mod, wf, bfg, bb, tl):
    nb, l, d = x.shape
    mod_spec = lambda k: pl.BlockSpec((bb, 1, d), lambda i, j: (i, 0, k))
    return pl.pallas_call(
        _norm1_kernel,
        grid=(nb // bb, l // tl),
        in_specs=[pl.BlockSpec((bb, tl, d), lambda i, j: (i, j, 0)),
                  pl.BlockSpec((1, 1, d), lambda i, j: (0, 0, 0)),
                  mod_spec(0), mod_spec(1),
                  pl.BlockSpec((d, LANES), lambda i, j: (0, 0)),
                  pl.BlockSpec((1, LANES), lambda i, j: (0, 0))],
        out_specs=[pl.BlockSpec((bb, tl, d), lambda i, j: (i, j, 0)),
                   pl.BlockSpec((bb, tl, H_FOX), lambda i, j: (i, j, 0))],
        out_shape=[jax.ShapeDtypeStruct((nb, l, d), BF16),
                   jax.ShapeDtypeStruct((nb, l, H_FOX), F32)],
        compiler_params=_params(("arbitrary", "arbitrary"), VMEM_LIMIT),
        name="norm1",
    )(x, g.reshape(1, 1, d), mod, mod, wf, bfg)


def _mm_kernel(a_ref, w_ref, o_ref):
    o_ref[...] = jnp.dot(a_ref[...], w_ref[...], preferred_element_type=F32).astype(o_ref.dtype)


def _matmul(a, w, out_dtype, tm, tn, name):
    m, k = a.shape
    n = w.shape[1]
    return pl.pallas_call(
        _mm_kernel,
        grid=(m // tm, n // tn),
        in_specs=[pl.BlockSpec((tm, k), lambda i, j: (i, 0)),
                  pl.BlockSpec((k, tn), lambda i, j: (0, j))],
        out_specs=pl.BlockSpec((tm, tn), lambda i, j: (i, j)),
        out_shape=jax.ShapeDtypeStruct((m, n), out_dtype),
        compiler_params=_params(("arbitrary", "arbitrary"), VMEM_LIMIT),
        name=name,
    )(a, w)


def _mm_heads_kernel(a_ref, w_ref, o_ref, ob_ref):
    bb, tl, nh, dh = o_ref.shape
    res = jnp.dot(a_ref[...], w_ref[...], preferred_element_type=F32)
    ob_ref[...] = res.astype(BF16)
    for hh in range(nh):
        o_ref[:, :, hh, :] = res[:, hh * dh:(hh + 1) * dh].reshape(bb, tl, dh)


def _matmul_heads(a, w, nb, l, bb, tl, name):
    m, k = a.shape
    n = w.shape[1]
    nj = l // tl
    return pl.pallas_call(
        _mm_heads_kernel,
        grid=(nb // bb, nj),
        in_specs=[pl.BlockSpec((bb * tl, k), lambda i, j: (i * nj + j, 0)),
                  pl.BlockSpec((k, n), lambda i, j: (0, 0))],
        out_specs=[pl.BlockSpec((bb, tl, H_FOX, DH_FOX), lambda i, j: (i, j, 0, 0)),
                   pl.BlockSpec((bb * tl, n), lambda i, j: (i * nj + j, 0))],
        out_shape=[jax.ShapeDtypeStruct((nb, l, H_FOX, DH_FOX), F32),
                   jax.ShapeDtypeStruct((m, n), BF16)],
        compiler_params=_params(("arbitrary", "arbitrary"), VMEM_LIMIT),
        name=name,
    )(a, w)


def _cumsum_kernel(x_ref, o_ref, carry_ref, *, scale):
    tl = x_ref.shape[-1]

    @pl.when(pl.program_id(1) == 0)
    def _():
        carry_ref[...] = jnp.zeros_like(carry_ref)

    r = lax.broadcasted_iota(I32, (tl, tl), 0)
    c = lax.broadcasted_iota(I32, (tl, tl), 1)
    tri = (r <= c).astype(F32)
    y = jnp.dot(x_ref[0], tri, precision=lax.Precision.HIGHEST, preferred_element_type=F32)
    y = y + carry_ref[:, 0:1]
    o_ref[0] = y * scale
    carry_ref[...] = jnp.broadcast_to(y[:, tl - 1:tl], carry_ref.shape)


def _cumsum_rows(x, tl, scale):
    nb, h, s = x.shape
    return pl.pallas_call(
        functools.partial(_cumsum_kernel, scale=scale),
        grid=(nb, s // tl),
        in_specs=[pl.BlockSpec((1, h, tl), lambda b, t: (b, 0, t))],
        out_specs=pl.BlockSpec((1, h, tl), lambda b, t: (b, 0, t)),
        out_shape=jax.ShapeDtypeStruct((nb, h, s), F32),
        scratch_shapes=[pltpu.VMEM((h, LANES), F32)],
        compiler_params=_params(("arbitrary", "arbitrary"), VMEM_LIMIT),
        name="cumsum_logf",
    )(x)


def _ret_kernel(q_ref, k_ref, v_ref, g_ref, cos_ref, sin_ref, dmat_ref, qdec_ref, kdec_ref, sdec_ref,
                s0_ref, gn_ref, o_ref, sout_ref, s_scr, *, heads):
    t = pl.program_id(2)
    hs = range(heads)

    @pl.when(t == 0)
    def _():
        s_scr[...] = s0_ref[0]

    cos = cos_ref[...]
    sin = sin_ref[...]
    half = DK_RET // 2
    ks = [slice(hh * DK_RET, (hh + 1) * DK_RET) for hh in hs]
    vs = [slice(hh * DV_RET, (hh + 1) * DV_RET) for hh in hs]
    q = [q_ref[0, :, ks[hh]] for hh in hs]
    k = [k_ref[0, :, ks[hh]] for hh in hs]
    qr = [q[hh] * cos + pltpu.roll(q[hh], half, 1) * sin for hh in hs]
    kr = [k[hh] * cos + pltpu.roll(k[hh], half, 1) * sin for hh in hs]
    vb = [v_ref[0, :, vs[hh]] for hh in hs]
    state = [s_scr[hh] for hh in hs]
    s = [_nt_dot(qr[hh].astype(BF16), kr[hh].astype(BF16)) * dmat_ref[hh] for hh in hs]
    o = [jnp.dot(s[hh].astype(BF16), vb[hh], preferred_element_type=F32) for hh in hs]
    o = [o[hh] + jnp.dot((qr[hh] * qdec_ref[hh]).astype(BF16), state[hh].astype(BF16),
                         preferred_element_type=F32) for hh in hs]
    kd_t = [(kr[hh] * kdec_ref[hh]).T.astype(BF16) for hh in hs]
    s_new = [state[hh] * sdec_ref[hh] + jnp.dot(kd_t[hh], vb[hh], preferred_element_type=F32) for hh in hs]
    for hh in hs:
        s_scr[hh] = s_new[hh]

    @pl.when(t == pl.num_programs(2) - 1)
    def _():
        sout_ref[0] = s_scr[...]

    mu = [jnp.mean(o[hh], axis=-1, keepdims=True) for hh in hs]
    oc = [o[hh] - mu[hh] for hh in hs]
    var = [jnp.mean(oc[hh] * oc[hh], axis=-1, keepdims=True) for hh in hs]
    on = [oc[hh] * lax.rsqrt(var[hh] + EPS) * gn_ref[:, vs[hh]] for hh in hs]
    g = [g_ref[0, :, vs[hh]].astype(F32) for hh in hs]
    for hh in hs:
        o_ref[0, :, vs[hh]] = (on[hh] * (g[hh] * _sigmoid(g[hh]))).astype(BF16)


def _retention(zqk, zb, s0, g_ret_gn, pos0, tb, heads=4):
    nb, l, _ = zqk.shape
    nt = l // tb
    half = DK_RET // 2
    inv = 1.0 / (ROPE_BASE ** (jnp.arange(half, dtype=F32) * 2.0 / DK_RET))
    ang = (pos0 + jnp.arange(l)).astype(F32)[:, None] * inv[None, :]
    cos = jnp.concatenate([jnp.cos(ang), jnp.cos(ang)], axis=-1)
    sin = jnp.concatenate([-jnp.sin(ang), jnp.sin(ang)], axis=-1)
    lg = jnp.log1p(-jnp.exp2(-5.0 - jnp.arange(H_RET, dtype=F32)))
    i = jnp.arange(tb)
    d = (i[:, None] - i[None, :]).astype(F32)
    ci, cj = i[:, None] // CHUNK, i[None, :] // CHUNK
    expo = jnp.where(ci == cj, jnp.abs(d), d)
    dmat = jnp.where((cj <= ci)[None], jnp.exp(expo[None] * lg[:, None, None]), 0.0)
    fi = i.astype(F32)
    qdec = jnp.broadcast_to(jnp.exp((fi[None, :] + 1.0) * lg[:, None])[:, :, None], (H_RET, tb, DK_RET))
    kdec = jnp.broadcast_to(jnp.exp((tb - 1.0 - fi[None, :]) * lg[:, None])[:, :, None], (H_RET, tb, DK_RET))
    sdec = jnp.broadcast_to(jnp.exp(tb * lg)[:, None, None], (H_RET, 1, DV_RET))
    ng = H_RET // heads
    kw, vw = heads * DK_RET, heads * DV_RET
    head_spec = lambda shape: pl.BlockSpec((heads,) + shape, lambda b, g, t: (g, 0, 0))
    return pl.pallas_call(
        functools.partial(_ret_kernel, heads=heads),
        grid=(nb, ng, nt),
        in_specs=[pl.BlockSpec((1, tb, kw), lambda b, g, t: (b, t, g)),
                  pl.BlockSpec((1, tb, kw), lambda b, g, t: (b, t, ng + g)),
                  pl.BlockSpec((1, tb, vw), lambda b, g, t: (b, t, g)),
                  pl.BlockSpec((1, tb, vw), lambda b, g, t: (b, t, ng + g)),
                  pl.BlockSpec((tb, DK_RET), lambda b, g, t: (t, 0)),
                  pl.BlockSpec((tb, DK_RET), lambda b, g, t: (t, 0)),
                  head_spec((tb, tb)), head_spec((tb, DK_RET)), head_spec((tb, DK_RET)), head_spec((1, DV_RET)),
                  pl.BlockSpec((1, heads, DK_RET, DV_RET), lambda b, g, t: (b, g, 0, 0)),
                  pl.BlockSpec((1, vw), lambda b, g, t: (0, g))],
        out_specs=[pl.BlockSpec((1, tb, vw), lambda b, g, t: (b, t, g)),
                   pl.BlockSpec((1, heads, DK_RET, DV_RET), lambda b, g, t: (b, g, 0, 0))],
        out_shape=[jax.ShapeDtypeStruct((nb, l, H_RET * DV_RET), BF16),
                   jax.ShapeDtypeStruct((nb, H_RET, DK_RET, DV_RET), F32)],
        scratch_shapes=[pltpu.VMEM((heads, DK_RET, DV_RET), F32)],
        compiler_params=_params(("arbitrary", "arbitrary", "arbitrary"), VMEM_LIMIT),
        name="retention",
    )(zqk, zqk, zb, zb, cos, sin, dmat, qdec, kdec, sdec, s0, g_ret_gn.reshape(1, -1))


def _fox_prompt_kernel(q_ref, k_ref, v_ref, fk_ref, ftok_ref, o_ref, m_scr, l_scr, acc_scr, fq_scr, *, heads, tk):
    g = pl.program_id(1)
    qi = pl.program_id(2)
    tq = q_ref.shape[1]
    nck = tk // LANES
    hs = range(heads)
    sls = [slice(hh * DH_FOX, (hh + 1) * DH_FOX) for hh in hs]
    ft = ftok_ref[0]
    lane = lax.broadcasted_iota(I32, ft.shape, 1)
    for hh in hs:
        m_scr[hh] = jnp.full((tq, LANES), -jnp.inf, F32)
        l_scr[hh] = jnp.zeros((tq, LANES), F32)
        acc_scr[hh] = jnp.zeros((tq, DH_FOX), F32)
        fq = jnp.sum(jnp.where(lane == g * heads + hh, ft, 0.0), axis=-1, keepdims=True)
        fq_scr[hh] = jnp.broadcast_to(fq, (tq, LANES))
    ones = jnp.ones((tk, LANES), BF16)

    def steps(ki, masked):
        rows = pl.ds(pl.multiple_of(ki * tk, tk), tk)
        m_prev = [m_scr[hh] for hh in hs]
        t = [_nt_dot(q_ref[0, :, sls[hh]], k_ref[0, rows, sls[hh]]) for hh in hs]
        t = [t[hh] - fk_ref[0, hh, pl.ds(ki, 1), :] for hh in hs]
        if masked:
            row = lax.broadcasted_iota(I32, (tq, tk), 0)
            col = lax.broadcasted_iota(I32, (tq, tk), 1)
            t = [jnp.where(col <= row, t[hh], -jnp.inf) for hh in hs]
        m_new = [jnp.maximum(m_prev[hh], jnp.max(t[hh], axis=-1, keepdims=True) + fq_scr[hh]) for hh in hs]
        shift = [m_new[hh] - fq_scr[hh] for hh in hs]
        p = [jnp.concatenate([jnp.exp2(t[hh][:, c * LANES:(c + 1) * LANES] - shift[hh]) for c in range(nck)],
                             axis=1).astype(BF16) for hh in hs]
        alpha = [jnp.exp2(m_prev[hh] - m_new[hh]) for hh in hs]
        pv = [jnp.dot(p[hh], jnp.concatenate([v_ref[0, rows, sls[hh]], ones], axis=1),
                      preferred_element_type=F32) for hh in hs]
        for hh in hs:
            l_scr[hh] = alpha[hh] * l_scr[hh] + pv[hh][:, DH_FOX:]
            acc_scr[hh] = alpha[hh] * acc_scr[hh] + pv[hh][:, :DH_FOX]
            m_scr[hh] = m_new[hh]

    def body(ki, carry):
        steps(ki, False)
        return carry

    lax.fori_loop(0, qi, body, 0)
    steps(qi, True)
    for hh in hs:
        o_ref[0, :, sls[hh]] = (acc_scr[hh] / l_scr[hh]).astype(BF16)


def _fox_prompt(zb, kb, vb, f_rows, f_tok, tq, heads=2):
    nb, s, hd = kb.shape
    nq = s // tq
    w = heads * DH_FOX
    q_off = (2 * hd) // w
    kv_spec = pl.BlockSpec((1, s, w), lambda b, g, qi: (b, 0, g))
    return pl.pallas_call(
        functools.partial(_fox_prompt_kernel, heads=heads, tk=tq),
        grid=(nb, H_FOX // heads, nq),
        in_specs=[pl.BlockSpec((1, tq, w), lambda b, g, qi: (b, qi, q_off + g)),
                  kv_spec, kv_spec,
                  pl.BlockSpec((1, heads, nq, tq), lambda b, g, qi: (b, g, 0, 0)),
                  pl.BlockSpec((1, tq, H_FOX), lambda b, g, qi: (b, qi, 0))],
        out_specs=pl.BlockSpec((1, tq, w), lambda b, g, qi: (b, qi, g)),
        out_shape=jax.ShapeDtypeStruct((nb, s, hd), BF16),
        scratch_shapes=[pltpu.VMEM((heads, tq, LANES), F32), pltpu.VMEM((heads, tq, LANES), F32),
                        pltpu.VMEM((heads, tq, DH_FOX), F32), pltpu.VMEM((heads, tq, LANES), F32)],
        compiler_params=_params(("arbitrary",) * 3, VMEM_LIMIT),
        name="fox_prompt",
    )(zb, kb, vb, f_rows.reshape(nb, H_FOX, nq, tq), f_tok)


def _fox_sample_kernel(q_ref, ck_ref, cv_ref, kn_ref, vn_ref, fk_ref, ftok_ref, o_ref, *, heads):
    g = pl.program_id(1)
    l = q_ref.shape[1]
    p_len = ck_ref.shape[2]
    ft = ftok_ref[0]
    lane = lax.broadcasted_iota(I32, ft.shape, 1)
    row = lax.broadcasted_iota(I32, (l, l), 0)
    col = lax.broadcasted_iota(I32, (l, l), 1)
    for hh in range(heads):
        hg = g * heads + hh
        sl = slice(hh * DH_FOX, (hh + 1) * DH_FOX)
        fk = fk_ref[0, pl.ds(hg, 1), :]
        fq = jnp.sum(jnp.where(lane == hg, ft, 0.0), axis=-1, keepdims=True)
        qh = q_ref[0, :, sl]
        s1 = _nt_dot(qh, ck_ref[0, 0, :, hh, :].astype(BF16)) + fq - fk[:, :p_len]
        s2 = _nt_dot(qh, kn_ref[0, :, sl]) + fq - fk[:, p_len:p_len + l]
        s2 = jnp.where(col <= row, s2, -jnp.inf)
        m = jnp.maximum(jnp.max(s1, axis=-1, keepdims=True), jnp.max(s2, axis=-1, keepdims=True))
        p1 = jnp.exp2(s1 - m)
        p2 = jnp.exp2(s2 - m)
        den = jnp.sum(p1, axis=-1, keepdims=True) + jnp.sum(p2, axis=-1, keepdims=True)
        o = jnp.dot(p1.astype(BF16), cv_ref[0, 0, :, hh, :].astype(BF16), preferred_element_type=F32)
        o = o + jnp.dot(p2.astype(BF16), vn_ref[0, :, sl], preferred_element_type=F32)
        o_ref[0, :, sl] = (o / den).astype(BF16)


def _fox_sample(zb, kb, vb, cache_k, cache_v, f_rows, f_tok, heads=8):
    nb, l, hd = kb.shape
    p_len = cache_k.shape[2]
    sp = f_rows.shape[-1]
    w = heads * DH_FOX
    q_off = (2 * hd) // w
    assert p_len % l == 0
    new_spec = pl.BlockSpec((1, l, w), lambda b, g: (b, 0, g))
    cache_spec = pl.BlockSpec((1, 1, p_len, heads, DH_FOX), lambda b, g: (0, b, 0, g, 0))
    return pl.pallas_call(
        functools.partial(_fox_sample_kernel, heads=heads),
        grid=(nb, H_FOX // heads),
        in_specs=[pl.BlockSpec((1, l, w), lambda b, g: (b, 0, q_off + g)),
                  cache_spec, cache_spec, new_spec, new_spec,
                  pl.BlockSpec((1, H_FOX, sp), lambda b, g: (b, 0, 0)),
                  pl.BlockSpec((1, l, H_FOX), lambda b, g: (b, p_len // l, 0))],
        out_specs=new_spec,
        out_shape=jax.ShapeDtypeStruct((nb, l, hd), BF16),
        compiler_params=_params(("arbitrary", "arbitrary"), VMEM_LIMIT),
        name="fox_sample",
    )(zb, cache_k, cache_v, kb, vb, f_rows, f_tok)


def _mix1_kernel(or_ref, of_ref, w1_ref, w2_ref, gr_ref, gf_ref, o_ref):
    a = jnp.dot(or_ref[...], w1_ref[...], preferred_element_type=F32)
    b = jnp.dot(of_ref[...], w2_ref[...], preferred_element_type=F32)
    gr = _sigmoid(gr_ref[...].astype(F32))
    gf = _sigmoid(gf_ref[...].astype(F32))
    o_ref[...] = (gr * a + gf * b).astype(BF16)


def _mix1(o_r, o_f, w1, w2, zb, tm, tn):
    m, d = o_r.shape
    gr_off = (3 * d) // tn
    gf_off = (4 * d) // tn
    return pl.pallas_call(
        _mix1_kernel,
        grid=(m // tm, d // tn),
        in_specs=[pl.BlockSpec((tm, d), lambda i, j: (i, 0)),
                  pl.BlockSpec((tm, d), lambda i, j: (i, 0)),
                  pl.BlockSpec((d, tn), lambda i, j: (0, j)),
                  pl.BlockSpec((d, tn), lambda i, j: (0, j)),
                  pl.BlockSpec((tm, tn), lambda i, j: (i, gr_off + j)),
                  pl.BlockSpec((tm, tn), lambda i, j: (i, gf_off + j))],
        out_specs=pl.BlockSpec((tm, tn), lambda i, j: (i, j)),
        out_shape=jax.ShapeDtypeStruct((m, d), BF16),
        compiler_params=_params(("arbitrary", "arbitrary"), VMEM_LIMIT),
        name="mix_gates",
    )(o_r, o_f, w1, w2, zb, zb)


def _mix2_kernel(m_ref, w_ref, x_ref, gt_ref, sh_ref, sc_ref, g_ref, wrt_ref, br_ref, cnt0_ref,
                 h1_ref, hn_ref, eidx_ref, rank_ref, wts_ref, cnt_ref):
    bb, tl, d = x_ref.shape
    tm = bb * tl
    ne = wrt_ref.shape[0]

    @pl.when((pl.program_id(0) == 0) & (pl.program_id(1) == 0))
    def _():
        cnt_ref[...] = cnt0_ref[...]

    mix = jnp.dot(m_ref[...].reshape(tm, d), w_ref[...], preferred_element_type=F32)
    h1 = x_ref[...] + gt_ref[...] * mix.reshape(bb, tl, d)
    h1_ref[...] = h1
    y = h1 * lax.rsqrt(jnp.mean(h1 * h1, axis=-1, keepdims=True) + EPS) * g_ref[...]
    hn = (y * (1.0 + sc_ref[...]) + sh_ref[...]).reshape(tm, d)
    hn_ref[...] = _pack_halves(hn)

    logits = lax.dot_general(wrt_ref[...], hn, (((1,), (1,)), ((), ())), precision=lax.Precision.HIGHEST,
                             preferred_element_type=F32)
    scores = _sigmoid(logits)
    cur = scores + br_ref[...]
    eid = lax.broadcasted_iota(I32, (ne, tm), 0).astype(F32)
    slot = lax.broadcasted_iota(I32, (DEST_W, tm), 0)
    r = lax.broadcasted_iota(I32, (tm, tm), 0)
    c = lax.broadcasted_iota(I32, (tm, tm), 1)
    earlier = (r < c).astype(BF16)
    picks = []
    mask = jnp.zeros((ne, tm), F32)
    for _ in range(TOP_K):
        mx = jnp.max(cur, axis=0, keepdims=True)
        idx = jnp.min(jnp.where(cur == mx, eid, float(ne)), axis=0, keepdims=True)
        pick = eid == idx
        picks.append((idx, pick))
        mask = jnp.where(pick, 1.0, mask)
        cur = jnp.where(pick, -jnp.inf, cur)
    rank = jnp.dot(mask.astype(BF16), earlier, preferred_element_type=F32) + cnt_ref[:, 0:1]
    cnt_ref[...] = cnt_ref[...] + jnp.sum(mask, axis=1, keepdims=True)
    wsum = jnp.sum(mask * scores, axis=0, keepdims=True)
    eidx = jnp.zeros((DEST_W, tm), I32)
    rnk = jnp.zeros((DEST_W, tm), I32)
    wts = jnp.zeros((DEST_W, tm), F32)
    for kk, (idx, pick) in enumerate(picks):
        sc_k = jnp.sum(jnp.where(pick, scores, 0.0), axis=0, keepdims=True)
        rk_k = jnp.sum(jnp.where(pick, rank, 0.0), axis=0, keepdims=True)
        eidx = jnp.where(slot == kk, idx.astype(I32), eidx)
        rnk = jnp.where(slot == kk, rk_k.astype(I32), rnk)
        wts = jnp.where(slot == kk, sc_k / wsum * ROUTE_SCALE, wts)
    eidx_ref[...] = eidx
    rank_ref[...] = rnk
    wts_ref[...] = wts


def _mix2(m3, w_out, x, mod, g2, w_router, b_router, cnt0, bb, tl):
    nb, l, d = x.shape
    ne = w_router.shape[1]
    t = nb * l
    tm = bb * tl
    nj = l // tl
    mod_spec = lambda k: pl.BlockSpec((bb, 1, d), lambda i, j: (i, 0, k))
    slot_spec = pl.BlockSpec((DEST_W, tm), lambda i, j: (0, i * nj + j))
    blk3 = pl.BlockSpec((bb, tl, d), lambda i, j: (i, j, 0))
    return pl.pallas_call(
        _mix2_kernel,
        grid=(nb // bb, nj),
        in_specs=[blk3,
                  pl.BlockSpec((d, d), lambda i, j: (0, 0), pipeline_mode=pl.Buffered(1)),
                  blk3,
                  mod_spec(2), mod_spec(3), mod_spec(4),
                  pl.BlockSpec((1, 1, d), lambda i, j: (0, 0, 0)),
                  pl.BlockSpec((ne, d), lambda i, j: (0, 0)),
                  pl.BlockSpec((ne, 1), lambda i, j: (0, 0)),
                  pl.BlockSpec((ne, LANES), lambda i, j: (0, 0))],
        out_specs=[blk3, pl.BlockSpec((tm, d // 2), lambda i, j: (i * nj + j, 0)), slot_spec, slot_spec, slot_spec,
                   pl.BlockSpec((ne, LANES), lambda i, j: (0, 0))],
        out_shape=[jax.ShapeDtypeStruct((nb, l, d), F32),
                   jax.ShapeDtypeStruct((t, d // 2), I32),
                   jax.ShapeDtypeStruct((DEST_W, t), I32),
                   jax.ShapeDtypeStruct((DEST_W, t), I32),
                   jax.ShapeDtypeStruct((DEST_W, t), F32),
                   jax.ShapeDtypeStruct((ne, LANES), F32)],
        compiler_params=_params(("arbitrary", "arbitrary"), VMEM_LIMIT),
        name="outproj_norm2_router",
    )(m3, w_out, x, mod, mod, mod, g2.reshape(1, 1, d), w_router.T, b_router.reshape(ne, 1), cnt0)


def _dispatch_kernel(zs_ref, zn_ref, dest_ref, xp_ref, xs_ref, buf_ref, zero_scr, sem, *, np_blocks):
    i = pl.program_id(0)
    tb = xp_ref.shape[0]

    def scatter(x_ref):
        def row_copy(r, kk):
            d = dest_ref[r * DEST_W + kk]
            return pltpu.make_async_copy(x_ref.at[pl.ds(r, 1), :], buf_ref.at[pl.ds(d, 1), :], sem)

        def issue(r, carry):
            for kk in range(TOP_K):
                row_copy(r, kk).start()
            return carry

        def drain(r, carry):
            for kk in range(TOP_K):
                row_copy(r, kk).wait()
            return carry

        lax.fori_loop(0, tb, issue, 0)
        lax.fori_loop(0, tb, drain, 0)

    @pl.when(i < np_blocks)
    def _():
        scatter(xp_ref)

    @pl.when(i >= np_blocks)
    def _():
        scatter(xs_ref)

    @pl.when(i == pl.num_programs(0) - 1)
    def _():
        zero_scr[...] = jnp.zeros_like(zero_scr)

        def per_expert(e, carry):
            start = zs_ref[e]

            def zero_copy(r):
                return pltpu.make_async_copy(zero_scr.at[pl.ds(0, 1), :], buf_ref.at[pl.ds(start + r, 1), :], sem)

            def issue(r, c):
                zero_copy(r).start()
                return c

            def drain(r, c):
                zero_copy(r).wait()
                return c

            lax.fori_loop(0, zn_ref[e], issue, 0)
            lax.fori_loop(0, zn_ref[e], drain, 0)
            return carry

        lax.fori_loop(0, zs_ref.shape[0], per_expert, 0)


def _dispatch(xp, xs, dest, zero_start, zero_n, rows, tb):
    tp, w = xp.shape
    ts = xs.shape[0]
    npb = tp // tb
    grid_spec = pltpu.PrefetchScalarGridSpec(
        num_scalar_prefetch=2,
        grid=(npb + ts // tb,),
        in_specs=[pl.BlockSpec((tb * DEST_W,), lambda i, zs, zn: (i,), memory_space=pltpu.SMEM),
                  pl.BlockSpec((tb, w), lambda i, zs, zn: (jnp.minimum(i, npb - 1), 0)),
                  pl.BlockSpec((tb, w), lambda i, zs, zn: (jnp.maximum(i - npb, 0), 0))],
        out_specs=pl.BlockSpec(memory_space=pl.ANY),
        scratch_shapes=[pltpu.VMEM((8, w), I32), pltpu.SemaphoreType.DMA(())],
    )
    return pl.pallas_call(
        functools.partial(_dispatch_kernel, np_blocks=npb),
        grid_spec=grid_spec,
        out_shape=jax.ShapeDtypeStruct((rows, w), I32),
        compiler_params=_params(("arbitrary",), VMEM_LIMIT),
        name="moe_dispatch",
    )(zero_start, zero_n, dest, xp, xs)


def _expert_kernel(be_ref, nv_ref, nu_ref, x_ref, wg_ref, wu_ref, wgt_ref, wut_ref, wda_ref, wdb_ref, o_ref,
                   xb_scr, hid_scr, wgb_scr, wub_scr, wdn_scr, *, n_main):
    del be_ref, nu_ref
    i = pl.program_id(0)
    j = pl.program_id(1)
    half = x_ref.shape[1]
    ft = wg_ref.shape[2]
    tail = wgt_ref.shape[2]
    ct = wda_ref.shape[2]
    nv = nv_ref[i]
    big = 4 * SUB_ROWS
    n_big = nv // 4
    rem = nv % 4

    def for_chunks(fn):
        def body(c, carry):
            fn(pl.multiple_of(c * big, big), big)
            return carry

        lax.fori_loop(0, n_big, body, 0)

        @pl.when(rem >= 2)
        def _():
            fn(pl.multiple_of(n_big * big, SUB_ROWS), 2 * SUB_ROWS)

        @pl.when(rem % 2 == 1)
        def _():
            fn(pl.multiple_of((nv - 1) * SUB_ROWS, SUB_ROWS), SUB_ROWS)

    @pl.when(nv > 0)
    def _():
        @pl.when(j == 0)
        def _():
            def unpack(r0, rows):
                hi, lo = _unpack_halves(x_ref[pl.ds(r0, rows), :])
                xb_scr[pl.ds(r0, rows), :half] = hi.astype(BF16)
                xb_scr[pl.ds(r0, rows), half:] = lo.astype(BF16)

            for_chunks(unpack)

        def up_phase(wg, wu, width):
            wgb_scr[:, :width] = wg[0].astype(BF16)
            wub_scr[:, :width] = wu[0].astype(BF16)

            def chunk(r0, rows):
                xb = xb_scr[pl.ds(r0, rows), :]
                gate = jnp.dot(xb, wgb_scr[:, :width], preferred_element_type=F32)
                up = jnp.dot(xb, wub_scr[:, :width], preferred_element_type=F32)
                hid_scr[j, pl.ds(r0, rows), :width] = (gate * _sigmoid(gate) * up).astype(BF16)

            for_chunks(chunk)

        @pl.when(j < n_main)
        def _():
            up_phase(wg_ref, wu_ref, ft)

        @pl.when(j == n_main)
        def _():
            wgb_scr[:, :tail] = wgt_ref[0].astype(BF16)
            wgb_scr[:, tail:2 * tail] = wut_ref[0].astype(BF16)

            def chunk(r0, rows):
                gu = jnp.dot(xb_scr[pl.ds(r0, rows), :], wgb_scr[:, :2 * tail], preferred_element_type=F32)
                gate, up = gu[:, :tail], gu[:, tail:]
                hid_scr[j, pl.ds(r0, rows), :tail] = (gate * _sigmoid(gate) * up).astype(BF16)

            for_chunks(chunk)

        @pl.when(j > n_main)
        def _():
            wdn_scr[:, :ct] = wda_ref[0].astype(BF16)
            wdn_scr[:, ct:] = wdb_ref[0].astype(BF16)

            def chunk(r0, rows):
                pieces = [hid_scr[t, pl.ds(r0, rows), :] for t in range(n_main)]
                pieces.append(hid_scr[n_main, pl.ds(r0, rows), :tail])
                hid = jnp.concatenate(pieces, axis=1)
                res = jnp.dot(hid, wdn_scr[...], preferred_element_type=F32)
                o_ref[pl.ds(r0, rows), :] = _pack_halves(res)

            for_chunks(chunk)

    @pl.when(j > n_main)
    def _():
        def body(c, carry):
            r0 = pl.multiple_of(c * SUB_ROWS, SUB_ROWS)
            o_ref[pl.ds(r0, SUB_ROWS), :] = jnp.zeros((SUB_ROWS, o_ref.shape[1]), I32)
            return carry

        lax.fori_loop(nv, o_ref.shape[0] // SUB_ROWS, body, 0)


def _expert_ffn(xp, wg, wu, wd, block_expert, n_valid, n_used, rb):
    rows, half = xp.shape
    d = 2 * half
    f = wg.shape[2]
    n_main = f // F_TILE
    tail = f - n_main * F_TILE
    assert tail > 0 and tail % LANES == 0 and (n_main * F_TILE) % tail == 0
    tail_blk = (n_main * F_TILE) // tail
    ct = F_TILE
    n_out = half // ct
    nblk = rows // rb

    def main_col(i, j, be, nv, nu):
        return (be[i], 0, jnp.minimum(j, n_main - 1))

    def out_col(j):
        return jnp.clip(j - (n_main + 1), 0, n_out - 1)

    grid_spec = pltpu.PrefetchScalarGridSpec(
        num_scalar_prefetch=3,
        grid=(nblk, n_main + 1 + n_out),
        in_specs=[pl.BlockSpec((rb, half), lambda i, j, be, nv, nu: (jnp.minimum(i, nu[0] - 1), 0),
                               pipeline_mode=pl.Buffered(1)),
                  pl.BlockSpec((1, d, F_TILE), main_col),
                  pl.BlockSpec((1, d, F_TILE), main_col),
                  pl.BlockSpec((1, d, tail), lambda i, j, be, nv, nu: (be[i], 0, tail_blk)),
                  pl.BlockSpec((1, d, tail), lambda i, j, be, nv, nu: (be[i], 0, tail_blk)),
                  pl.BlockSpec((1, f, ct), lambda i, j, be, nv, nu: (be[i], 0, out_col(j))),
                  pl.BlockSpec((1, f, ct), lambda i, j, be, nv, nu: (be[i], 0, n_out + out_col(j)))],
        out_specs=pl.BlockSpec((rb, ct), lambda i, j, be, nv, nu: (i, out_col(j))),
        scratch_shapes=[pltpu.VMEM((rb, d), BF16), pltpu.VMEM((n_main + 1, rb, F_TILE), BF16),
                        pltpu.VMEM((d, F_TILE), BF16), pltpu.VMEM((d, F_TILE), BF16),
                        pltpu.VMEM((f, 2 * ct), BF16)],
    )
    return pl.pallas_call(
        functools.partial(_expert_kernel, n_main=n_main),
        grid_spec=grid_spec,
        out_shape=jax.ShapeDtypeStruct((rows, half), I32),
        compiler_params=_params(("arbitrary", "arbitrary"), VMEM_LIMIT),
        name="expert_ffn",
    )(block_expert, n_valid, n_used, xp, wg, wu, wg, wu, wd, wd)


def _combine_kernel(dcur_ref, dnext_ref, wts_ref, sh_ref, h1_ref, gt_ref, g_ref, eo_ref, y_ref,
                    gbuf, sems):
    bb, tl, d = h1_ref.shape
    tb = bb * tl
    half = d // 2
    nj = pl.num_programs(1)
    step = pl.program_id(0) * nj + pl.program_id(1)
    nsteps = pl.num_programs(0) * nj
    slot = step % 2

    def row_copy(dref, sl, r, kk):
        dd = dref[r * DEST_W + kk]
        return pltpu.make_async_copy(eo_ref.at[pl.ds(dd, 1), :], gbuf.at[sl, kk, pl.ds(r, 1), :],
                                     sems.at[sl])

    def issue(dref, sl):
        def body(r, carry):
            for kk in range(TOP_K):
                row_copy(dref, sl, r, kk).start()
            return carry
        lax.fori_loop(0, tb, body, 0)

    @pl.when(step == 0)
    def _():
        issue(dcur_ref, 0)

    @pl.when(step + 1 < nsteps)
    def _():
        issue(dnext_ref, 1 - slot)

    def drain(r, carry):
        for kk in range(TOP_K):
            row_copy(dcur_ref, slot, r, kk).wait()
        return carry
    lax.fori_loop(0, tb, drain, 0)

    wts = wts_ref[...]
    s_hi, s_lo = _unpack_halves(sh_ref[...])
    for kk in range(TOP_K):
        hi, lo = _unpack_halves(gbuf[slot, kk])
        wk = wts[:, kk:kk + 1]
        s_hi = s_hi + wk * hi
        s_lo = s_lo + wk * lo
    h2_hi = h1_ref[:, :, :half] + gt_ref[:, :, :half] * s_hi.reshape(bb, tl, half)
    h2_lo = h1_ref[:, :, half:] + gt_ref[:, :, half:] * s_lo.reshape(bb, tl, half)
    ms = (jnp.sum(h2_hi * h2_hi, axis=-1, keepdims=True)
          + jnp.sum(h2_lo * h2_lo, axis=-1, keepdims=True)) / d
    inv = lax.rsqrt(ms + EPS)
    y_ref[:, :, :half] = h2_hi * inv * g_ref[:, :, :half]
    y_ref[:, :, half:] = h2_lo * inv * g_ref[:, :, half:]


def _combine(dest, wts, shared, h1, mod, g_final, eo, bb, tl):
    nb, l, d = h1.shape
    tb = bb * tl
    nj = l // tl
    nsteps = (nb // bb) * nj
    lin = lambda i, j: i * nj + j
    return pl.pallas_call(
        _combine_kernel,
        grid=(nb // bb, nj),
        in_specs=[pl.BlockSpec((tb * DEST_W,), lambda i, j: (lin(i, j),), memory_space=pltpu.SMEM),
                  pl.BlockSpec((tb * DEST_W,), lambda i, j: (jnp.minimum(lin(i, j) + 1, nsteps - 1),),
                               memory_space=pltpu.SMEM),
                  pl.BlockSpec((tb, LANES), lambda i, j: (lin(i, j), 0)),
                  pl.BlockSpec((tb, d // 2), lambda i, j: (lin(i, j), 0)),
                  pl.BlockSpec((bb, tl, d), lambda i, j: (i, j, 0)),
                  pl.BlockSpec((bb, 1, d), lambda i, j: (i, 0, 5)),
                  pl.BlockSpec((1, 1, d), lambda i, j: (0, 0, 0)),
                  pl.BlockSpec(memory_space=pl.ANY)],
        out_specs=pl.BlockSpec((bb, tl, d), lambda i, j: (i, j, 0)),
        out_shape=jax.ShapeDtypeStruct((nb, l, d), F32),
        scratch_shapes=[pltpu.VMEM((2, TOP_K, tb, d // 2), I32), pltpu.SemaphoreType.DMA((2,))],
        compiler_params=_params(("arbitrary", "arbitrary"), VMEM_LIMIT),
        name="moe_combine",
    )(dest, dest, wts, shared, h1, mod, g_final.reshape(1, 1, d), eo)


def _prep_w_in(w_in):
    d = w_in.shape[0]
    qr_w, vr_w, qf_w = H_RET * DK_RET, H_RET * DV_RET, H_FOX * DH_FOX
    sizes = (qr_w, qr_w, vr_w, vr_w, qf_w, qf_w, qf_w, H_FOX, d, d)
    offs = [0]
    for s in sizes:
        offs.append(offs[-1] + s)
    part = lambda n: w_in[:, offs[n]:offs[n + 1]]
    w_qk = jnp.concatenate([part(0), part(1) * (DK_RET ** -0.5)], axis=1).astype(BF16)
    w_b = jnp.concatenate([part(2), part(3), part(4) * (DH_FOX ** -0.5 * LOG2E), part(8), part(9)], axis=1).astype(BF16)
    w_kf = part(5).astype(BF16)
    w_vf = part(6).astype(BF16)
    w_fl = jnp.pad(part(7), ((0, 0), (0, LANES - H_FOX))).astype(BF16)
    return w_qk, w_b, w_kf, w_vf, w_fl


def _trunk_front(x, mod, lw, bb, tl, tm, fox, s0, pos0, ret_tb):
    nb, l, d = x.shape
    t = nb * l
    hn, logf = _norm1(x, lw["g_norm1"], mod, lw["w_fl"], lw["b_fl"], bb, tl)
    hn2d = hn.reshape(t, d)
    zqk = _matmul(hn2d, lw["w_qk"], F32, tm, 512, "proj_qk_ret")
    kf, kb = _matmul_heads(hn2d, lw["w_kf"], nb, l, bb, tl, "proj_k_fox")
    vf, vb = _matmul_heads(hn2d, lw["w_vf"], nb, l, bb, tl, "proj_v_fox")
    zb = _matmul(hn2d, lw["w_b"], BF16, tm, 512, "proj_rest")
    zb3 = zb.reshape(nb, l, -1)
    o_r, s_new = _retention(zqk.reshape(nb, l, -1), zb3, s0, lw["g_ret_gn"], pos0, ret_tb)
    o_f = fox(zb3, kb.reshape(nb, l, -1), vb.reshape(nb, l, -1), logf)
    m = _mix1(o_r.reshape(t, d), o_f.reshape(t, d), lw["w_ret_out"], lw["w_fox_out"], zb, tm, 512)
    return m.reshape(nb, l, d), (kf, vf, logf, s_new)


def kernel(x_prompt, x_sample, c_prompt, c_sample, cache_fox_k, cache_fox_v, cache_fox_logf, state_ret, w_ada, b_ada, g_norm1, w_in, b_forget, g_ret_gn, w_ret_out, w_fox_out, w_out, g_norm2, w_router, b_router, w_exp_gate, w_exp_up, w_exp_down, w_sh_gate, w_sh_up, w_sh_down, g_final):
    depth = w_in.shape[0]
    assert depth == 1
    nbp, s, d = x_prompt.shape
    nbs, ls, _ = x_sample.shape
    p_len = cache_fox_k.shape[2]
    ne = w_router.shape[-1]
    tp, ts = nbp * s, nbs * ls

    nb_all = nbp + nbs
    nb_pad = -(-nb_all // 8) * 8
    c_all = jnp.pad(jnp.concatenate([c_prompt, c_sample], axis=0), ((0, nb_pad - nb_all), (0, 0)))
    mod_all = _adaln(c_all, w_ada[0], b_ada[0])
    mod_p = mod_all[:nbp].reshape(nbp, 1, 6 * d)
    mod_s = mod_all[nbp:nb_all].reshape(nbs, 1, 6 * d)

    w_qk, w_b, w_kf, w_vf, w_fl = _prep_w_in(w_in[0])
    lw = dict(g_norm1=g_norm1[0], w_fl=w_fl,
              b_fl=jnp.pad(b_forget[0], (0, LANES - H_FOX)).reshape(1, LANES),
              w_qk=w_qk, w_b=w_b, w_kf=w_kf, w_vf=w_vf, g_ret_gn=g_ret_gn[0],
              w_ret_out=w_ret_out[0].astype(BF16), w_fox_out=w_fox_out[0].astype(BF16))
    w_out_b = w_out[0].astype(BF16)

    tl_p = min(s, 512)
    tm_p = min(tp, 1024)
    bb_s = max(1, min(nbs, 512 // ls))
    tm_s = min(ts, 1024)

    def fox_p(zb3, kb3, vb3, logf):
        f_rows = _cumsum_rows(logf.swapaxes(1, 2), min(s, 512), LOG2E)
        return _fox_prompt(zb3, kb3, vb3, f_rows, f_rows.swapaxes(1, 2), min(s, 1024))

    def fox_s(zb3, kb3, vb3, logf):
        full = jnp.concatenate([cache_fox_logf[0].astype(F32), logf], axis=1)
        sp = -(-(p_len + ls) // LANES) * LANES
        full = jnp.pad(full, ((0, 0), (0, sp - p_len - ls), (0, 0)))
        f_rows = _cumsum_rows(full.swapaxes(1, 2), sp, LOG2E)
        return _fox_sample(zb3, kb3, vb3, cache_fox_k, cache_fox_v, f_rows, f_rows.swapaxes(1, 2))

    s0_p = jnp.zeros((nbp, H_RET, DK_RET, DV_RET), F32)
    m_p, st_p = _trunk_front(x_prompt, mod_p, lw, 1, tl_p, tm_p, fox_p, s0_p, 0, min(s, 256))
    m_s, st_s = _trunk_front(x_sample, mod_s, lw, bb_s, ls, tm_s, fox_s, state_ret[0], p_len, ls)

    cnt0 = jnp.zeros((ne, LANES), F32)
    tl2_p = min(s, 512)
    bb2_s = max(1, min(nbs, 512 // ls))
    h1_p, hn_p, e_p, r_p, wt_p, cnt_p = _mix2(m_p, w_out_b, x_prompt, mod_p, g_norm2[0], w_router[0],
                                              b_router[0], cnt0, 1, tl2_p)
    h1_s, hn_s, e_s, r_s, wt_s, cnt_s = _mix2(m_s, w_out_b, x_sample, mod_s, g_norm2[0], w_router[0],
                                              b_router[0], cnt_p, bb2_s, ls)

    counts = cnt_s[:, 0].astype(I32)
    rblk = EXPERT_ROWS
    padded = (counts + rblk - 1) // rblk * rblk
    pad_end = jnp.cumsum(padded)
    pad_start = pad_end - padded
    t_all = tp + ts
    nblk = (t_all * TOP_K + ne * (rblk - 1) + rblk - 1) // rblk
    rows = nblk * rblk
    blk_start = jnp.arange(nblk, dtype=I32) * rblk
    block_expert = jnp.minimum(jnp.searchsorted(pad_end, blk_start, side="right"), ne - 1).astype(I32)
    n_used = (pad_end[-1] // rblk).astype(I32).reshape(1)
    valid_rows = jnp.clip(pad_start[block_expert] + counts[block_expert] - blk_start, 0, rblk)
    valid_rows = jnp.where(blk_start < pad_end[-1], valid_rows, 0)
    n_valid = ((valid_rows + SUB_ROWS - 1) // SUB_ROWS).astype(I32)
    zero_start = (pad_start + counts).astype(I32)
    zero_n = ((-counts) % SUB_ROWS).astype(I32)
    e_all = jnp.concatenate([e_p, e_s], axis=1)
    r_all = jnp.concatenate([r_p, r_s], axis=1)
    start_of = jnp.sum(jnp.where(e_all[..., None] == jnp.arange(ne, dtype=I32), pad_start.astype(I32), 0), axis=-1)
    dest = (start_of + r_all).astype(I32).T.reshape(-1)
    tok_major = lambda w: jnp.pad(w.T, ((0, 0), (0, LANES - DEST_W)))

    buf = _dispatch(hn_p, hn_s, dest, zero_start, zero_n, rows, min(tp, ts, 256))
    eo = _expert_ffn(buf, w_exp_gate[0], w_exp_up[0], w_exp_down[0], block_expert, n_valid, n_used, rblk)

    def shared_ffn(hn):
        rb = min(hn.shape[0], rblk)
        nb_ = hn.shape[0] // rb
        return _expert_ffn(hn, w_sh_gate, w_sh_up, w_sh_down, jnp.zeros((nb_,), I32),
                           jnp.full((nb_,), rb // SUB_ROWS, I32), jnp.full((1,), nb_, I32), rb)

    tlc_p = min(s, 128)
    bbc_s = max(1, min(nbs, 128 // ls))
    y_p = _combine(dest[:tp * DEST_W], tok_major(wt_p), shared_ffn(hn_p), h1_p, mod_p, g_final, eo, 1, tlc_p)
    y_s = _combine(dest[tp * DEST_W:], tok_major(wt_s), shared_ffn(hn_s), h1_s, mod_s, g_final, eo, bbc_s, ls)

    kf_p, vf_p, logf_p, s_p = st_p
    kf_s, vf_s, logf_s, s_s = st_s
    return (y_p, y_s, kf_p[None], vf_p[None], logf_p[None], s_p[None],
            kf_s[None], vf_s[None], logf_s[None], s_s[None])
```

```python
import functools
import math

import jax
import jax.numpy as jnp
from jax import lax
from jax.experimental import pallas as pl
from jax.experimental.pallas import tpu as pltpu

F32 = jnp.float32
BF16 = jnp.bfloat16
I32 = jnp.int32

CHUNK = 64
H_RET = 8
DK_RET = 128
DV_RET = 256
H_FOX = 16
DH_FOX = 128
TOP_K = 6
ROUTE_SCALE = 2.5
ROPE_BASE = 10000.0
EPS = 1e-6
LOG2E = math.log2(math.e)

LANES = 128
DEST_W = 8
EXPERT_ROWS = 2048
SUB_ROWS = 256
F_TILE = 256
VMEM_LIMIT = 56 * 1024 * 1024


def _params(sem, vmem=None):
    return pltpu.CompilerParams(dimension_semantics=sem, vmem_limit_bytes=vmem)


def _sigmoid(x):
    return 1.0 / (1.0 + jnp.exp(-x))


def _nt_dot(a, b):
    return lax.dot_general(a, b, (((1,), (1,)), ((), ())), preferred_element_type=F32)


def _pack_halves(x):
    n = x.shape[-1] // 2
    hi = pltpu.bitcast(x[:, :n].astype(BF16).astype(F32), I32)
    lo = pltpu.bitcast(x[:, n:].astype(BF16).astype(F32), I32)
    return hi | lax.shift_right_logical(lo, jnp.int32(16))


def _unpack_halves(u):
    hi = pltpu.bitcast(u & jnp.int32(-65536), F32)
    lo = pltpu.bitcast(lax.shift_left(u, jnp.int32(16)), F32)
    return hi, lo


def _ada_kernel(c_ref, w_ref, b_ref, o_ref):
    c = c_ref[...]
    s = (c * _sigmoid(c)).astype(BF16)
    o_ref[...] = jnp.dot(s, w_ref[...].astype(BF16), preferred_element_type=F32) + b_ref[...]


def _adaln(c, w_ada, b_ada):
    nb, d = c.shape
    n = w_ada.shape[1]
    tn = 1024
    return pl.pallas_call(
        _ada_kernel,
        grid=(n // tn,),
        in_specs=[pl.BlockSpec((nb, d), lambda j: (0, 0)),
                  pl.BlockSpec((d, tn), lambda j: (0, j)),
                  pl.BlockSpec((1, tn), lambda j: (0, j))],
        out_specs=pl.BlockSpec((nb, tn), lambda j: (0, j)),
        out_shape=jax.ShapeDtypeStruct((nb, n), F32),
        compiler_params=_params(("arbitrary",), VMEM_LIMIT),
        name="adaln",
    )(c, w_ada, b_ada.reshape(1, n))


def _norm1_kernel(x_ref, g_ref, sh_ref, sc_ref, wf_ref, bf_ref, hn_ref, logf_ref):
    bb, tl, d = x_ref.shape
    x = x_ref[...]
    y = x * lax.rsqrt(jnp.mean(x * x, axis=-1, keepdims=True) + EPS) * g_ref[...]
    hn = (y * (1.0 + sc_ref[...]) + sh_ref[...]).astype(BF16)
    hn_ref[...] = hn
    z = jnp.dot(hn.reshape(bb * tl, d), wf_ref[...], preferred_element_type=F32) + bf_ref[...]
    logf = jnp.minimum(z, 0.0) - jnp.log(1.0 + jnp.exp(-jnp.abs(z)))
    logf_ref[...] = logf[:, :H_FOX].reshape(bb, tl, H_FOX)


def _norm1(x, g, mod, wf, bfg, bb, tl):
    nb, l, d = x.shape
    mod_spec = lambda k: pl.BlockSpec((bb, 1, d), lambda i, j: (i, 0, k))
    return pl.pallas_call(
        _norm1_kernel,
        grid=(nb // bb, l // tl),
        in_specs=[pl.BlockSpec((bb, tl, d), lambda i, j: (i, j, 0)),
                  pl.BlockSpec((1, 1, d), lambda i, j: (0, 0, 0)),
                  mod_spec(0), mod_spec(1),
                  pl.BlockSpec((d, LANES), lambda i, j: (0, 0)),
                  pl.BlockSpec((1, LANES), lambda i, j: (0, 0))],
        out_specs=[pl.BlockSpec((bb, tl, d), lambda i, j: (i, j, 0)),
                   pl.BlockSpec((bb, tl, H_FOX), lambda i, j: (i, j, 0))],
        out_shape=[jax.ShapeDtypeStruct((nb, l, d), BF16),
                   jax.ShapeDtypeStruct((nb, l, H_FOX), F32)],
        compiler_params=_params(("arbitrary", "arbitrary"), VMEM_LIMIT),
        name="norm1",
    )(x, g.reshape(1, 1, d), mod, mod, wf, bfg)


def _mm_kernel(a_ref, w_ref, o_ref):
    o_ref[...] = jnp.dot(a_ref[...], w_ref[...], preferred_element_type=F32).astype(o_ref.dtype)


def _matmul(a, w, out_dtype, tm, tn, name):
    m, k = a.shape
    n = w.shape[1]
    return pl.pallas_call(
        _mm_kernel,
        grid=(m // tm, n // tn),
        in_specs=[pl.BlockSpec((tm, k), lambda i, j: (i, 0)),
                  pl.BlockSpec((k, tn), lambda i, j: (0, j))],
        out_specs=pl.BlockSpec((tm, tn), lambda i, j: (i, j)),
        out_shape=jax.ShapeDtypeStruct((m, n), out_dtype),
        compiler_params=_params(("arbitrary", "arbitrary"), VMEM_LIMIT),
        name=name,
    )(a, w)


def _mm_heads_kernel(a_ref, w_ref, o_ref, ob_ref):
    bb, tl, nh, dh = o_ref.shape
    res = jnp.dot(a_ref[...], w_ref[...], preferred_element_type=F32)
    ob_ref[...] = res.astype(BF16)
    for hh in range(nh):
        o_ref[:, :, hh, :] = res[:, hh * dh:(hh + 1) * dh].reshape(bb, tl, dh)


def _matmul_heads(a, w, nb, l, bb, tl, name):
    m, k = a.shape
    n = w.shape[1]
    nj = l // tl
    return pl.pallas_call(
        _mm_heads_kernel,
        grid=(nb // bb, nj),
        in_specs=[pl.BlockSpec((bb * tl, k), lambda i, j: (i * nj + j, 0)),
                  pl.BlockSpec((k, n), lambda i, j: (0, 0))],
        out_specs=[pl.BlockSpec((bb, tl, H_FOX, DH_FOX), lambda i, j: (i, j, 0, 0)),
                   pl.BlockSpec((bb * tl, n), lambda i, j: (i * nj + j, 0))],
        out_shape=[jax.ShapeDtypeStruct((nb, l, H_FOX, DH_FOX), F32),
                   jax.ShapeDtypeStruct((m, n), BF16)],
        compiler_params=_params(("arbitrary", "arbitrary"), VMEM_LIMIT),
        name=name,
    )(a, w)


def _cumsum_kernel(x_ref, o_ref, carry_ref, *, scale):
    tl = x_ref.shape[-1]

    @pl.when(pl.program_id(1) == 0)
    def _():
        carry_ref[...] = jnp.zeros_like(carry_ref)

    r = lax.broadcasted_iota(I32, (tl, tl), 0)
    c = lax.broadcasted_iota(I32, (tl, tl), 1)
    tri = (r <= c).astype(F32)
    y = jnp.dot(x_ref[0], tri, precision=lax.Precision.HIGHEST, preferred_element_type=F32)
    y = y + carry_ref[:, 0:1]
    o_ref[0] = y * scale
    carry_ref[...] = jnp.broadcast_to(y[:, tl - 1:tl], carry_ref.shape)


def _cumsum_rows(x, tl, scale):
    nb, h, s = x.shape
    return pl.pallas_call(
        functools.partial(_cumsum_kernel, scale=scale),
        grid=(nb, s // tl),
        in_specs=[pl.BlockSpec((1, h, tl), lambda b, t: (b, 0, t))],
        out_specs=pl.BlockSpec((1, h, tl), lambda b, t: (b, 0, t)),
        out_shape=jax.ShapeDtypeStruct((nb, h, s), F32),
        scratch_shapes=[pltpu.VMEM((h, LANES), F32)],
        compiler_params=_params(("arbitrary", "arbitrary"), VMEM_LIMIT),
        name="cumsum_logf",
    )(x)


def _ret_kernel(q_ref, k_ref, v_ref, g_ref, cos_ref, sin_ref, dmat_ref, qdec_ref, kdec_ref, sdec_ref,
                s0_ref, gn_ref, o_ref, sout_ref, s_scr, *, heads):
    t = pl.program_id(2)
    hs = range(heads)

    @pl.when(t == 0)
    def _():
        s_scr[...] = s0_ref[0]

    cos = cos_ref[...]
    sin = sin_ref[...]
    half = DK_RET // 2
    ks = [slice(hh * DK_RET, (hh + 1) * DK_RET) for hh in hs]
    vs = [slice(hh * DV_RET, (hh + 1) * DV_RET) for hh in hs]
    q = [q_ref[0, :, ks[hh]] for hh in hs]
    k = [k_ref[0, :, ks[hh]] for hh in hs]
    qr = [q[hh] * cos + pltpu.roll(q[hh], half, 1) * sin for hh in hs]
    kr = [k[hh] * cos + pltpu.roll(k[hh], half, 1) * sin for hh in hs]
    vb = [v_ref[0, :, vs[hh]] for hh in hs]
    state = [s_scr[hh] for hh in hs]
    s = [_nt_dot(qr[hh].astype(BF16), kr[hh].astype(BF16)) * dmat_ref[hh] for hh in hs]
    o = [jnp.dot(s[hh].astype(BF16), vb[hh], preferred_element_type=F32) for hh in hs]
    o = [o[hh] + jnp.dot((qr[hh] * qdec_ref[hh]).astype(BF16), state[hh].astype(BF16),
                         preferred_element_type=F32) for hh in hs]
    kd_t = [(kr[hh] * kdec_ref[hh]).T.astype(BF16) for hh in hs]
    s_new = [state[hh] * sdec_ref[hh] + jnp.dot(kd_t[hh], vb[hh], preferred_element_type=F32) for hh in hs]
    for hh in hs:
        s_scr[hh] = s_new[hh]

    @pl.when(t == pl.num_programs(2) - 1)
    def _():
        sout_ref[0] = s_scr[...]

    mu = [jnp.mean(o[hh], axis=-1, keepdims=True) for hh in hs]
    oc = [o[hh] - mu[hh] for hh in hs]
    var = [jnp.mean(oc[hh] * oc[hh], axis=-1, keepdims=True) for hh in hs]
    on = [oc[hh] * lax.rsqrt(var[hh] + EPS) * gn_ref[:, vs[hh]] for hh in hs]
    g = [g_ref[0, :, vs[hh]].astype(F32) for hh in hs]
    for hh in hs:
        o_ref[0, :, vs[hh]] = (on[hh] * (g[hh] * _sigmoid(g[hh]))).astype(BF16)


def _retention(zqk, zb, s0, g_ret_gn, pos0, tb, heads=4):
    nb, l, _ = zqk.shape
    nt = l // tb
    half = DK_RET // 2
    inv = 1.0 / (ROPE_BASE ** (jnp.arange(half, dtype=F32) * 2.0 / DK_RET))
    ang = (pos0 + jnp.arange(l)).astype(F32)[:, None] * inv[None, :]
    cos = jnp.concatenate([jnp.cos(ang), jnp.cos(ang)], axis=-1)
    sin = jnp.concatenate([-jnp.sin(ang), jnp.sin(ang)], axis=-1)
    lg = jnp.log1p(-jnp.exp2(-5.0 - jnp.arange(H_RET, dtype=F32)))
    i = jnp.arange(tb)
    d = (i[:, None] - i[None, :]).astype(F32)
    ci, cj = i[:, None] // CHUNK, i[None, :] // CHUNK
    expo = jnp.where(ci == cj, jnp.abs(d), d)
    dmat = jnp.where((cj <= ci)[None], jnp.exp(expo[None] * lg[:, None, None]), 0.0)
    fi = i.astype(F32)
    qdec = jnp.broadcast_to(jnp.exp((fi[None, :] + 1.0) * lg[:, None])[:, :, None], (H_RET, tb, DK_RET))
    kdec = jnp.broadcast_to(jnp.exp((tb - 1.0 - fi[None, :]) * lg[:, None])[:, :, None], (H_RET, tb, DK_RET))
    sdec = jnp.broadcast_to(jnp.exp(tb * lg)[:, None, None], (H_RET, 1, DV_RET))
    ng = H_RET // heads
    kw, vw = heads * DK_RET, heads * DV_RET
    head_spec = lambda shape: pl.BlockSpec((heads,) + shape, lambda b, g, t: (g, 0, 0))
    return pl.pallas_call(
        functools.partial(_ret_kernel, heads=heads),
        grid=(nb, ng, nt),
        in_specs=[pl.BlockSpec((1, tb, kw), lambda b, g, t: (b, t, g)),
                  pl.BlockSpec((1, tb, kw), lambda b, g, t: (b, t, ng + g)),
                  pl.BlockSpec((1, tb, vw), lambda b, g, t: (b, t, g)),
                  pl.BlockSpec((1, tb, vw), lambda b, g, t: (b, t, ng + g)),
                  pl.BlockSpec((tb, DK_RET), lambda b, g, t: (t, 0)),
                  pl.BlockSpec((tb, DK_RET), lambda b, g, t: (t, 0)),
                  head_spec((tb, tb)), head_spec((tb, DK_RET)), head_spec((tb, DK_RET)), head_spec((1, DV_RET)),
                  pl.BlockSpec((1, heads, DK_RET, DV_RET), lambda b, g, t: (b, g, 0, 0)),
                  pl.BlockSpec((1, vw), lambda b, g, t: (0, g))],
        out_specs=[pl.BlockSpec((1, tb, vw), lambda b, g, t: (b, t, g)),
                   pl.BlockSpec((1, heads, DK_RET, DV_RET), lambda b, g, t: (b, g, 0, 0))],
        out_shape=[jax.ShapeDtypeStruct((nb, l, H_RET * DV_RET), BF16),
                   jax.ShapeDtypeStruct((nb, H_RET, DK_RET, DV_RET), F32)],
        scratch_shapes=[pltpu.VMEM((heads, DK_RET, DV_RET), F32)],
        compiler_params=_params(("arbitrary", "arbitrary", "arbitrary"), VMEM_LIMIT),
        name="retention",
    )(zqk, zqk, zb, zb, cos, sin, dmat, qdec, kdec, sdec, s0, g_ret_gn.reshape(1, -1))


def _fox_prompt_kernel(q_ref, k_ref, v_ref, fk_ref, ftok_ref, o_ref, m_scr, l_scr, acc_scr, fq_scr, *, heads, tk):
    g = pl.program_id(1)
    qi = pl.program_id(2)
    tq = q_ref.shape[1]
    nck = tk // LANES
    hs = range(heads)
    sls = [slice(hh * DH_FOX, (hh + 1) * DH_FOX) for hh in hs]
    ft = ftok_ref[0]
    lane = lax.broadcasted_iota(I32, ft.shape, 1)
    for hh in hs:
        m_scr[hh] = jnp.full((tq, LANES), -jnp.inf, F32)
        l_scr[hh] = jnp.zeros((tq, LANES), F32)
        acc_scr[hh] = jnp.zeros((tq, DH_FOX), F32)
        fq = jnp.sum(jnp.where(lane == g * heads + hh, ft, 0.0), axis=-1, keepdims=True)
        fq_scr[hh] = jnp.broadcast_to(fq, (tq, LANES))
    ones = jnp.ones((tk, LANES), BF16)

    def steps(ki, masked):
        rows = pl.ds(pl.multiple_of(ki * tk, tk), tk)
        m_prev = [m_scr[hh] for hh in hs]
        t = [_nt_dot(q_ref[0, :, sls[hh]], k_ref[0, rows, sls[hh]]) for hh in hs]
        t = [t[hh] - fk_ref[0, hh, pl.ds(ki, 1), :] for hh in hs]
        if masked:
            row = lax.broadcasted_iota(I32, (tq, tk), 0)
            col = lax.broadcasted_iota(I32, (tq, tk), 1)
            t = [jnp.where(col <= row, t[hh], -jnp.inf) for hh in hs]
        m_new = [jnp.maximum(m_prev[hh], jnp.max(t[hh], axis=-1, keepdims=True) + fq_scr[hh]) for hh in hs]
        shift = [m_new[hh] - fq_scr[hh] for hh in hs]
        p = [jnp.concatenate([jnp.exp2(t[hh][:, c * LANES:(c + 1) * LANES] - shift[hh]) for c in range(nck)],
                             axis=1).astype(BF16) for hh in hs]
        alpha = [jnp.exp2(m_prev[hh] - m_new[hh]) for hh in hs]
        pv = [jnp.dot(p[hh], jnp.concatenate([v_ref[0, rows, sls[hh]], ones], axis=1),
                      preferred_element_type=F32) for hh in hs]
        for hh in hs:
            l_scr[hh] = alpha[hh] * l_scr[hh] + pv[hh][:, DH_FOX:]
            acc_scr[hh] = alpha[hh] * acc_scr[hh] + pv[hh][:, :DH_FOX]
            m_scr[hh] = m_new[hh]

    def body(ki, carry):
        steps(ki, False)
        return carry

    lax.fori_loop(0, qi, body, 0)
    steps(qi, True)
    for hh in hs:
        o_ref[0, :, sls[hh]] = (acc_scr[hh] / l_scr[hh]).astype(BF16)


def _fox_prompt(zb, kb, vb, f_rows, f_tok, tq, heads=2):
    nb, s, hd = kb.shape
    nq = s // tq
    w = heads * DH_FOX
    q_off = (2 * hd) // w
    kv_spec = pl.BlockSpec((1, s, w), lambda b, g, qi: (b, 0, g))
    return pl.pallas_call(
        functools.partial(_fox_prompt_kernel, heads=heads, tk=tq),
        grid=(nb, H_FOX // heads, nq),
        in_specs=[pl.BlockSpec((1, tq, w), lambda b, g, qi: (b, qi, q_off + g)),
                  kv_spec, kv_spec,
                  pl.BlockSpec((1, heads, nq, tq), lambda b, g, qi: (b, g, 0, 0)),
                  pl.BlockSpec((1, tq, H_FOX), lambda b, g, qi: (b, qi, 0))],
        out_specs=pl.BlockSpec((1, tq, w), lambda b, g, qi: (b, qi, g)),
        out_shape=jax.ShapeDtypeStruct((nb, s, hd), BF16),
        scratch_shapes=[pltpu.VMEM((heads, tq, LANES), F32), pltpu.VMEM((heads, tq, LANES), F32),
                        pltpu.VMEM((heads, tq, DH_FOX), F32), pltpu.VMEM((heads, tq, LANES), F32)],
        compiler_params=_params(("arbitrary",) * 3, VMEM_LIMIT),
        name="fox_prompt",
    )(zb, kb, vb, f_rows.reshape(nb, H_FOX, nq, tq), f_tok)


def _fox_sample_kernel(q_ref, ck_ref, cv_ref, kn_ref, vn_ref, fk_ref, ftok_ref, o_ref, *, heads):
    g = pl.program_id(1)
    l = q_ref.shape[1]
    p_len = ck_ref.shape[2]
    ft = ftok_ref[0]
    lane = lax.broadcasted_iota(I32, ft.shape, 1)
    row = lax.broadcasted_iota(I32, (l, l), 0)
    col = lax.broadcasted_iota(I32, (l, l), 1)
    for hh in range(heads):
        hg = g * heads + hh
        sl = slice(hh * DH_FOX, (hh + 1) * DH_FOX)
        fk = fk_ref[0, pl.ds(hg, 1), :]
        fq = jnp.sum(jnp.where(lane == hg, ft, 0.0), axis=-1, keepdims=True)
        qh = q_ref[0, :, sl]
        s1 = _nt_dot(qh, ck_ref[0, 0, :, hh, :].astype(BF16)) + fq - fk[:, :p_len]
        s2 = _nt_dot(qh, kn_ref[0, :, sl]) + fq - fk[:, p_len:p_len + l]
        s2 = jnp.where(col <= row, s2, -jnp.inf)
        m = jnp.maximum(jnp.max(s1, axis=-1, keepdims=True), jnp.max(s2, axis=-1, keepdims=True))
        p1 = jnp.exp2(s1 - m)
        p2 = jnp.exp2(s2 - m)
        den = jnp.sum(p1, axis=-1, keepdims=True) + jnp.sum(p2, axis=-1, keepdims=True)
        o = jnp.dot(p1.astype(BF16), cv_ref[0, 0, :, hh, :].astype(BF16), preferred_element_type=F32)
        o = o + jnp.dot(p2.astype(BF16), vn_ref[0, :, sl], preferred_element_type=F32)
        o_ref[0, :, sl] = (o / den).astype(BF16)


def _fox_sample(zb, kb, vb, cache_k, cache_v, f_rows, f_tok, heads=8):
    nb, l, hd = kb.shape
    p_len = cache_k.shape[2]
    sp = f_rows.shape[-1]
    w = heads * DH_FOX
    q_off = (2 * hd) // w
    assert p_len % l == 0
    new_spec = pl.BlockSpec((1, l, w), lambda b, g: (b, 0, g))
    cache_spec = pl.BlockSpec((1, 1, p_len, heads, DH_FOX), lambda b, g: (0, b, 0, g, 0))
    return pl.pallas_call(
        functools.partial(_fox_sample_kernel, heads=heads),
        grid=(nb, H_FOX // heads),
        in_specs=[pl.BlockSpec((1, l, w), lambda b, g: (b, 0, q_off + g)),
                  cache_spec, cache_spec, new_spec, new_spec,
                  pl.BlockSpec((1, H_FOX, sp), lambda b, g: (b, 0, 0)),
                  pl.BlockSpec((1, l, H_FOX), lambda b, g: (b, p_len // l, 0))],
        out_specs=new_spec,
        out_shape=jax.ShapeDtypeStruct((nb, l, hd), BF16),
        compiler_params=_params(("arbitrary", "arbitrary"), VMEM_LIMIT),
        name="fox_sample",
    )(zb, cache_k, cache_v, kb, vb, f_rows, f_tok)


def _mix1_kernel(or_ref, of_ref, w1_ref, w2_ref, gr_ref, gf_ref, o_ref):
    a = jnp.dot(or_ref[...], w1_ref[...], preferred_element_type=F32)
    b = jnp.dot(of_ref[...], w2_ref[...], preferred_element_type=F32)
    gr = _sigmoid(gr_ref[...].astype(F32))
    gf = _sigmoid(gf_ref[...].astype(F32))
    o_ref[...] = (gr * a + gf * b).astype(BF16)


def _mix1(o_r, o_f, w1, w2, zb, tm, tn):
    m, d = o_r.shape
    gr_off = (3 * d) // tn
    gf_off = (4 * d) // tn
    return pl.pallas_call(
        _mix1_kernel,
        grid=(m // tm, d // tn),
        in_specs=[pl.BlockSpec((tm, d), lambda i, j: (i, 0)),
                  pl.BlockSpec((tm, d), lambda i, j: (i, 0)),
                  pl.BlockSpec((d, tn), lambda i, j: (0, j)),
                  pl.BlockSpec((d, tn), lambda i, j: (0, j)),
                  pl.BlockSpec((tm, tn), lambda i, j: (i, gr_off + j)),
                  pl.BlockSpec((tm, tn), lambda i, j: (i, gf_off + j))],
        out_specs=pl.BlockSpec((tm, tn), lambda i, j: (i, j)),
        out_shape=jax.ShapeDtypeStruct((m, d), BF16),
        compiler_params=_params(("arbitrary", "arbitrary"), VMEM_LIMIT),
        name="mix_gates",
    )(o_r, o_f, w1, w2, zb, zb)


def _mix2_kernel(m_ref, w_ref, x_ref, gt_ref, sh_ref, sc_ref, g_ref, wrt_ref, br_ref, cnt0_ref,
                 h1_ref, hn_ref, eidx_ref, rank_ref, wts_ref, cnt_ref):
    bb, tl, d = x_ref.shape
    tm = bb * tl
    ne = wrt_ref.shape[0]

    @pl.when((pl.program_id(0) == 0) & (pl.program_id(1) == 0))
    def _():
        cnt_ref[...] = cnt0_ref[...]

    mix = jnp.dot(m_ref[...].reshape(tm, d), w_ref[...], preferred_element_type=F32)
    h1 = x_ref[...] + gt_ref[...] * mix.reshape(bb, tl, d)
    h1_ref[...] = h1
    y = h1 * lax.rsqrt(jnp.mean(h1 * h1, axis=-1, keepdims=True) + EPS) * g_ref[...]
    hn = (y * (1.0 + sc_ref[...]) + sh_ref[...]).reshape(tm, d)
    hn_ref[...] = _pack_halves(hn)

    logits = lax.dot_general(wrt_ref[...], hn, (((1,), (1,)), ((), ())), precision=lax.Precision.HIGHEST,
                             preferred_element_type=F32)
    scores = _sigmoid(logits)
    cur = scores + br_ref[...]
    eid = lax.broadcasted_iota(I32, (ne, tm), 0).astype(F32)
    slot = lax.broadcasted_iota(I32, (DEST_W, tm), 0)
    r = lax.broadcasted_iota(I32, (tm, tm), 0)
    c = lax.broadcasted_iota(I32, (tm, tm), 1)
    earlier = (r < c).astype(BF16)
    picks = []
    mask = jnp.zeros((ne, tm), F32)
    for _ in range(TOP_K):
        mx = jnp.max(cur, axis=0, keepdims=True)
        idx = jnp.min(jnp.where(cur == mx, eid, float(ne)), axis=0, keepdims=True)
        pick = eid == idx
        picks.append((idx, pick))
        mask = jnp.where(pick, 1.0, mask)
        cur = jnp.where(pick, -jnp.inf, cur)
    rank = jnp.dot(mask.astype(BF16), earlier, preferred_element_type=F32) + cnt_ref[:, 0:1]
    cnt_ref[...] = cnt_ref[...] + jnp.sum(mask, axis=1, keepdims=True)
    wsum = jnp.sum(mask * scores, axis=0, keepdims=True)
    eidx = jnp.zeros((DEST_W, tm), I32)
    rnk = jnp.zeros((DEST_W, tm), I32)
    wts = jnp.zeros((DEST_W, tm), F32)
    for kk, (idx, pick) in enumerate(picks):
        sc_k = jnp.sum(jnp.where(pick, scores, 0.0), axis=0, keepdims=True)
        rk_k = jnp.sum(jnp.where(pick, rank, 0.0), axis=0, keepdims=True)
        eidx = jnp.where(slot == kk, idx.astype(I32), eidx)
        rnk = jnp.where(slot == kk, rk_k.astype(I32), rnk)
        wts = jnp.where(slot == kk, sc_k / wsum * ROUTE_SCALE, wts)
    eidx_ref[...] = eidx
    rank_ref[...] = rnk
    wts_ref[...] = wts


def _mix2(m3, w_out, x, mod, g2, w_router, b_router, cnt0, bb, tl):
    nb, l, d = x.shape
    ne = w_router.shape[1]
    t = nb * l
    tm = bb * tl
    nj = l // tl
    mod_spec = lambda k: pl.BlockSpec((bb, 1, d), lambda i, j: (i, 0, k))
    slot_spec = pl.BlockSpec((DEST_W, tm), lambda i, j: (0, i * nj + j))
    blk3 = pl.BlockSpec((bb, tl, d), lambda i, j: (i, j, 0))
    return pl.pallas_call(
        _mix2_kernel,
        grid=(nb // bb, nj),
        in_specs=[blk3,
                  pl.BlockSpec((d, d), lambda i, j: (0, 0), pipeline_mode=pl.Buffered(1)),
                  blk3,
                  mod_spec(2), mod_spec(3), mod_spec(4),
                  pl.BlockSpec((1, 1, d), lambda i, j: (0, 0, 0)),
                  pl.BlockSpec((ne, d), lambda i, j: (0, 0)),
                  pl.BlockSpec((ne, 1), lambda i, j: (0, 0)),
                  pl.BlockSpec((ne, LANES), lambda i, j: (0, 0))],
        out_specs=[blk3, pl.BlockSpec((tm, d // 2), lambda i, j: (i * nj + j, 0)), slot_spec, slot_spec, slot_spec,
                   pl.BlockSpec((ne, LANES), lambda i, j: (0, 0))],
        out_shape=[jax.ShapeDtypeStruct((nb, l, d), F32),
                   jax.ShapeDtypeStruct((t, d // 2), I32),
                   jax.ShapeDtypeStruct((DEST_W, t), I32),
                   jax.ShapeDtypeStruct((DEST_W, t), I32),
                   jax.ShapeDtypeStruct((DEST_W, t), F32),
                   jax.ShapeDtypeStruct((ne, LANES), F32)],
        compiler_params=_params(("arbitrary", "arbitrary"), VMEM_LIMIT),
        name="outproj_norm2_router",
    )(m3, w_out, x, mod, mod, mod, g2.reshape(1, 1, d), w_router.T, b_router.reshape(ne, 1), cnt0)


def _dispatch_kernel(zs_ref, zn_ref, dest_ref, xp_ref, xs_ref, buf_ref, zero_scr, sem, *, np_blocks):
    i = pl.program_id(0)
    tb = xp_ref.shape[0]

    def scatter(x_ref):
        def row_copy(r, kk):
            d = dest_ref[r * DEST_W + kk]
            return pltpu.make_async_copy(x_ref.at[pl.ds(r, 1), :], buf_ref.at[pl.ds(d, 1), :], sem)

        def issue(r, carry):
            for kk in range(TOP_K):
                row_copy(r, kk).start()
            return carry

        def drain(r, carry):
            for kk in range(TOP_K):
                row_copy(r, kk).wait()
            return carry

        lax.fori_loop(0, tb, issue, 0)
        lax.fori_loop(0, tb, drain, 0)

    @pl.when(i < np_blocks)
    def _():
        scatter(xp_ref)

    @pl.when(i >= np_blocks)
    def _():
        scatter(xs_ref)

    @pl.when(i == pl.num_programs(0) - 1)
    def _():
        zero_scr[...] = jnp.zeros_like(zero_scr)

        def per_expert(e, carry):
            start = zs_ref[e]

            def zero_copy(r):
                return pltpu.make_async_copy(zero_scr.at[pl.ds(0, 1), :], buf_ref.at[pl.ds(start + r, 1), :], sem)

            def issue(r, c):
                zero_copy(r).start()
                return c

            def drain(r, c):
                zero_copy(r).wait()
                return c

            lax.fori_loop(0, zn_ref[e], issue, 0)
            lax.fori_loop(0, zn_ref[e], drain, 0)
            return carry

        lax.fori_loop(0, zs_ref.shape[0], per_expert, 0)


def _dispatch(xp, xs, dest, zero_start, zero_n, rows, tb):
    tp, w = xp.shape
    ts = xs.shape[0]
    npb = tp // tb
    grid_spec = pltpu.PrefetchScalarGridSpec(
        num_scalar_prefetch=2,
        grid=(npb + ts // tb,),
        in_specs=[pl.BlockSpec((tb * DEST_W,), lambda i, zs, zn: (i,), memory_space=pltpu.SMEM),
                  pl.BlockSpec((tb, w), lambda i, zs, zn: (jnp.minimum(i, npb - 1), 0)),
                  pl.BlockSpec((tb, w), lambda i, zs, zn: (jnp.maximum(i - npb, 0), 0))],
        out_specs=pl.BlockSpec(memory_space=pl.ANY),
        scratch_shapes=[pltpu.VMEM((8, w), I32), pltpu.SemaphoreType.DMA(())],
    )
    return pl.pallas_call(
        functools.partial(_dispatch_kernel, np_blocks=npb),
        grid_spec=grid_spec,
        out_shape=jax.ShapeDtypeStruct((rows, w), I32),
        compiler_params=_params(("arbitrary",), VMEM_LIMIT),
        name="moe_dispatch",
    )(zero_start, zero_n, dest, xp, xs)


def _expert_kernel(be_ref, nv_ref, nu_ref, x_ref, wg_ref, wu_ref, wgt_ref, wut_ref, wda_ref, wdb_ref, o_ref,
                   xb_scr, hid_scr, wgb_scr, wub_scr, wdn_scr, *, n_main):
    del be_ref, nu_ref
    i = pl.program_id(0)
    j = pl.program_id(1)
    half = x_ref.shape[1]
    ft = wg_ref.shape[2]
    tail = wgt_ref.shape[2]
    ct = wda_ref.shape[2]
    nv = nv_ref[i]
    big = 4 * SUB_ROWS
    n_big = nv // 4
    rem = nv % 4

    def for_chunks(fn):
        def body(c, carry):
            fn(pl.multiple_of(c * big, big), big)
            return carry

        lax.fori_loop(0, n_big, body, 0)

        @pl.when(rem >= 2)
        def _():
            fn(pl.multiple_of(n_big * big, SUB_ROWS), 2 * SUB_ROWS)

        @pl.when(rem % 2 == 1)
        def _():
            fn(pl.multiple_of((nv - 1) * SUB_ROWS, SUB_ROWS), SUB_ROWS)

    @pl.when(nv > 0)
    def _():
        @pl.when(j == 0)
        def _():
            def unpack(r0, rows):
                hi, lo = _unpack_halves(x_ref[pl.ds(r0, rows), :])
                xb_scr[pl.ds(r0, rows), :half] = hi.astype(BF16)
                xb_scr[pl.ds(r0, rows), half:] = lo.astype(BF16)

            for_chunks(unpack)

        def up_phase(wg, wu, width):
            wgb_scr[:, :width] = wg[0].astype(BF16)
            wub_scr[:, :width] = wu[0].astype(BF16)

            def chunk(r0, rows):
                xb = xb_scr[pl.ds(r0, rows), :]
                gate = jnp.dot(xb, wgb_scr[:, :width], preferred_element_type=F32)
                up = jnp.dot(xb, wub_scr[:, :width], preferred_element_type=F32)
                hid_scr[j, pl.ds(r0, rows), :width] = (gate * _sigmoid(gate) * up).astype(BF16)

            for_chunks(chunk)

        @pl.when(j < n_main)
        def _():
            up_phase(wg_ref, wu_ref, ft)

        @pl.when(j == n_main)
        def _():
            wgb_scr[:, :tail] = wgt_ref[0].astype(BF16)
            wgb_scr[:, tail:2 * tail] = wut_ref[0].astype(BF16)

            def chunk(r0, rows):
                gu = jnp.dot(xb_scr[pl.ds(r0, rows), :], wgb_scr[:, :2 * tail], preferred_element_type=F32)
                gate, up = gu[:, :tail], gu[:, tail:]
                hid_scr[j, pl.ds(r0, rows), :tail] = (gate * _sigmoid(gate) * up).astype(BF16)

            for_chunks(chunk)

        @pl.when(j > n_main)
        def _():
            wdn_scr[:, :ct] = wda_ref[0].astype(BF16)
            wdn_scr[:, ct:] = wdb_ref[0].astype(BF16)

            def chunk(r0, rows):
                pieces = [hid_scr[t, pl.ds(r0, rows), :] for t in range(n_main)]
                pieces.append(hid_scr[n_main, pl.ds(r0, rows), :tail])
                hid = jnp.concatenate(pieces, axis=1)
                res = jnp.dot(hid, wdn_scr[...], preferred_element_type=F32)
                o_ref[pl.ds(r0, rows), :] = _pack_halves(res)

            for_chunks(chunk)

    @pl.when(j > n_main)
    def _():
        def body(c, carry):
            r0 = pl.multiple_of(c * SUB_ROWS, SUB_ROWS)
            o_ref[pl.ds(r0, SUB_ROWS), :] = jnp.zeros((SUB_ROWS, o_ref.shape[1]), I32)
            return carry

        lax.fori_loop(nv, o_ref.shape[0] // SUB_ROWS, body, 0)


def _expert_ffn(xp, wg, wu, wd, block_expert, n_valid, n_used, rb):
    rows, half = xp.shape
    d = 2 * half
    f = wg.shape[2]
    n_main = f // F_TILE
    tail = f - n_main * F_TILE
    assert tail > 0 and tail % LANES == 0 and (n_main * F_TILE) % tail == 0
    tail_blk = (n_main * F_TILE) // tail
    ct = F_TILE
    n_out = half // ct
    nblk = rows // rb

    def main_col(i, j, be, nv, nu):
        return (be[i], 0, jnp.minimum(j, n_main - 1))

    def out_col(j):
        return jnp.clip(j - (n_main + 1), 0, n_out - 1)

    grid_spec = pltpu.PrefetchScalarGridSpec(
        num_scalar_prefetch=3,
        grid=(nblk, n_main + 1 + n_out),
        in_specs=[pl.BlockSpec((rb, half), lambda i, j, be, nv, nu: (jnp.minimum(i, nu[0] - 1), 0)),
                  pl.BlockSpec((1, d, F_TILE), main_col),
                  pl.BlockSpec((1, d, F_TILE), main_col),
                  pl.BlockSpec((1, d, tail), lambda i, j, be, nv, nu: (be[i], 0, tail_blk)),
                  pl.BlockSpec((1, d, tail), lambda i, j, be, nv, nu: (be[i], 0, tail_blk)),
                  pl.BlockSpec((1, f, ct), lambda i, j, be, nv, nu: (be[i], 0, out_col(j))),
                  pl.BlockSpec((1, f, ct), lambda i, j, be, nv, nu: (be[i], 0, n_out + out_col(j)))],
        out_specs=pl.BlockSpec((rb, ct), lambda i, j, be, nv, nu: (i, out_col(j))),
        scratch_shapes=[pltpu.VMEM((rb, d), BF16), pltpu.VMEM((n_main + 1, rb, F_TILE), BF16),
                        pltpu.VMEM((d, F_TILE), BF16), pltpu.VMEM((d, F_TILE), BF16),
                        pltpu.VMEM((f, 2 * ct), BF16)],
    )
    return pl.pallas_call(
        functools.partial(_expert_kernel, n_main=n_main),
        grid_spec=grid_spec,
        out_shape=jax.ShapeDtypeStruct((rows, half), I32),
        compiler_params=_params(("arbitrary", "arbitrary"), VMEM_LIMIT),
        name="expert_ffn",
    )(block_expert, n_valid, n_used, xp, wg, wu, wg, wu, wd, wd)


def _combine_kernel(dcur_ref, dnext_ref, wts_ref, sh_ref, h1_ref, gt_ref, g_ref, eo_ref, y_ref,
                    gbuf, sems):
    bb, tl, d = h1_ref.shape
    tb = bb * tl
    half = d // 2
    nj = pl.num_programs(1)
    step = pl.program_id(0) * nj + pl.program_id(1)
    nsteps = pl.num_programs(0) * nj
    slot = step % 2

    def row_copy(dref, sl, r, kk):
        dd = dref[r * DEST_W + kk]
        return pltpu.make_async_copy(eo_ref.at[pl.ds(dd, 1), :], gbuf.at[sl, kk, pl.ds(r, 1), :],
                                     sems.at[sl])

    def issue(dref, sl):
        def body(r, carry):
            for kk in range(TOP_K):
                row_copy(dref, sl, r, kk).start()
            return carry
        lax.fori_loop(0, tb, body, 0)

    @pl.when(step == 0)
    def _():
        issue(dcur_ref, 0)

    @pl.when(step + 1 < nsteps)
    def _():
        issue(dnext_ref, 1 - slot)

    def drain(r, carry):
        for kk in range(TOP_K):
            row_copy(dcur_ref, slot, r, kk).wait()
        return carry
    lax.fori_loop(0, tb, drain, 0)

    wts = wts_ref[...]
    s_hi, s_lo = _unpack_halves(sh_ref[...])
    for kk in range(TOP_K):
        hi, lo = _unpack_halves(gbuf[slot, kk])
        wk = wts[:, kk:kk + 1]
        s_hi = s_hi + wk * hi
        s_lo = s_lo + wk * lo
    h2_hi = h1_ref[:, :, :half] + gt_ref[:, :, :half] * s_hi.reshape(bb, tl, half)
    h2_lo = h1_ref[:, :, half:] + gt_ref[:, :, half:] * s_lo.reshape(bb, tl, half)
    ms = (jnp.sum(h2_hi * h2_hi, axis=-1, keepdims=True)
          + jnp.sum(h2_lo * h2_lo, axis=-1, keepdims=True)) / d
    inv = lax.rsqrt(ms + EPS)
    y_ref[:, :, :half] = h2_hi * inv * g_ref[:, :, :half]
    y_ref[:, :, half:] = h2_lo * inv * g_ref[:, :, half:]


def _combine(dest, wts, shared, h1, mod, g_final, eo, bb, tl):
    nb, l, d = h1.shape
    tb = bb * tl
    nj = l // tl
    nsteps = (nb // bb) * nj
    lin = lambda i, j: i * nj + j
    return pl.pallas_call(
        _combine_kernel,
        grid=(nb // bb, nj),
        in_specs=[pl.BlockSpec((tb * DEST_W,), lambda i, j: (lin(i, j),), memory_space=pltpu.SMEM),
                  pl.BlockSpec((tb * DEST_W,), lambda i, j: (jnp.minimum(lin(i, j) + 1, nsteps - 1),),
                               memory_space=pltpu.SMEM),
                  pl.BlockSpec((tb, LANES), lambda i, j: (lin(i, j), 0)),
                  pl.BlockSpec((tb, d // 2), lambda i, j: (lin(i, j), 0)),
                  pl.BlockSpec((bb, tl, d), lambda i, j: (i, j, 0)),
                  pl.BlockSpec((bb, 1, d), lambda i, j: (i, 0, 5)),
                  pl.BlockSpec((1, 1, d), lambda i, j: (0, 0, 0)),
                  pl.BlockSpec(memory_space=pl.ANY)],
        out_specs=pl.BlockSpec((bb, tl, d), lambda i, j: (i, j, 0)),
        out_shape=jax.ShapeDtypeStruct((nb, l, d), F32),
        scratch_shapes=[pltpu.VMEM((2, TOP_K, tb, d // 2), I32), pltpu.SemaphoreType.DMA((2,))],
        compiler_params=_params(("arbitrary", "arbitrary"), VMEM_LIMIT),
        name="moe_combine",
    )(dest, dest, wts, shared, h1, mod, g_final.reshape(1, 1, d), eo)


def _prep_w_in(w_in):
    d = w_in.shape[0]
    qr_w, vr_w, qf_w = H_RET * DK_RET, H_RET * DV_RET, H_FOX * DH_FOX
    sizes = (qr_w, qr_w, vr_w, vr_w, qf_w, qf_w, qf_w, H_FOX, d, d)
    offs = [0]
    for s in sizes:
        offs.append(offs[-1] + s)
    part = lambda n: w_in[:, offs[n]:offs[n + 1]]
    w_qk = jnp.concatenate([part(0), part(1) * (DK_RET ** -0.5)], axis=1).astype(BF16)
    w_b = jnp.concatenate([part(2), part(3), part(4) * (DH_FOX ** -0.5 * LOG2E), part(8), part(9)], axis=1).astype(BF16)
    w_kf = part(5).astype(BF16)
    w_vf = part(6).astype(BF16)
    w_fl = jnp.pad(part(7), ((0, 0), (0, LANES - H_FOX))).astype(BF16)
    return w_qk, w_b, w_kf, w_vf, w_fl


def _trunk_front(x, mod, lw, bb, tl, tm, fox, s0, pos0, ret_tb):
    nb, l, d = x.shape
    t = nb * l
    hn, logf = _norm1(x, lw["g_norm1"], mod, lw["w_fl"], lw["b_fl"], bb, tl)
    hn2d = hn.reshape(t, d)
    zqk = _matmul(hn2d, lw["w_qk"], F32, tm, 512, "proj_qk_ret")
    kf, kb = _matmul_heads(hn2d, lw["w_kf"], nb, l, bb, tl, "proj_k_fox")
    vf, vb = _matmul_heads(hn2d, lw["w_vf"], nb, l, bb, tl, "proj_v_fox")
    zb = _matmul(hn2d, lw["w_b"], BF16, tm, 512, "proj_rest")
    zb3 = zb.reshape(nb, l, -1)
    o_r, s_new = _retention(zqk.reshape(nb, l, -1), zb3, s0, lw["g_ret_gn"], pos0, ret_tb)
    o_f = fox(zb3, kb.reshape(nb, l, -1), vb.reshape(nb, l, -1), logf)
    m = _mix1(o_r.reshape(t, d), o_f.reshape(t, d), lw["w_ret_out"], lw["w_fox_out"], zb, tm, 512)
    return m.reshape(nb, l, d), (kf, vf, logf, s_new)


def kernel(x_prompt, x_sample, c_prompt, c_sample, cache_fox_k, cache_fox_v, cache_fox_logf, state_ret, w_ada, b_ada, g_norm1, w_in, b_forget, g_ret_gn, w_ret_out, w_fox_out, w_out, g_norm2, w_router, b_router, w_exp_gate, w_exp_up, w_exp_down, w_sh_gate, w_sh_up, w_sh_down, g_final):
    depth = w_in.shape[0]
    assert depth == 1
    nbp, s, d = x_prompt.shape
    nbs, ls, _ = x_sample.shape
    p_len = cache_fox_k.shape[2]
    ne = w_router.shape[-1]
    tp, ts = nbp * s, nbs * ls

    nb_all = nbp + nbs
    nb_pad = -(-nb_all // 8) * 8
    c_all = jnp.pad(jnp.concatenate([c_prompt, c_sample], axis=0), ((0, nb_pad - nb_all), (0, 0)))
    mod_all = _adaln(c_all, w_ada[0], b_ada[0])
    mod_p = mod_all[:nbp].reshape(nbp, 1, 6 * d)
    mod_s = mod_all[nbp:nb_all].reshape(nbs, 1, 6 * d)

    w_qk, w_b, w_kf, w_vf, w_fl = _prep_w_in(w_in[0])
    lw = dict(g_norm1=g_norm1[0], w_fl=w_fl,
              b_fl=jnp.pad(b_forget[0], (0, LANES - H_FOX)).reshape(1, LANES),
              w_qk=w_qk, w_b=w_b, w_kf=w_kf, w_vf=w_vf, g_ret_gn=g_ret_gn[0],
              w_ret_out=w_ret_out[0].astype(BF16), w_fox_out=w_fox_out[0].astype(BF16))
    w_out_b = w_out[0].astype(BF16)

    tl_p = min(s, 512)
    tm_p = min(tp, 1024)
    bb_s = max(1, min(nbs, 512 // ls))
    tm_s = min(ts, 1024)

    def fox_p(zb3, kb3, vb3, logf):
        f_rows = _cumsum_rows(logf.swapaxes(1, 2), min(s, 512), LOG2E)
        return _fox_prompt(zb3, kb3, vb3, f_rows, f_rows.swapaxes(1, 2), min(s, 1024))

    def fox_s(zb3, kb3, vb3, logf):
        full = jnp.concatenate([cache_fox_logf[0].astype(F32), logf], axis=1)
        sp = -(-(p_len + ls) // LANES) * LANES
        full = jnp.pad(full, ((0, 0), (0, sp - p_len - ls), (0, 0)))
        f_rows = _cumsum_rows(full.swapaxes(1, 2), sp, LOG2E)
        return _fox_sample(zb3, kb3, vb3, cache_fox_k, cache_fox_v, f_rows, f_rows.swapaxes(1, 2))

    s0_p = jnp.zeros((nbp, H_RET, DK_RET, DV_RET), F32)
    m_p, st_p = _trunk_front(x_prompt, mod_p, lw, 1, tl_p, tm_p, fox_p, s0_p, 0, min(s, 256))
    m_s, st_s = _trunk_front(x_sample, mod_s, lw, bb_s, ls, tm_s, fox_s, state_ret[0], p_len, ls)

    cnt0 = jnp.zeros((ne, LANES), F32)
    tl2_p = min(s, 512)
    bb2_s = max(1, min(nbs, 512 // ls))
    h1_p, hn_p, e_p, r_p, wt_p, cnt_p = _mix2(m_p, w_out_b, x_prompt, mod_p, g_norm2[0], w_router[0],
                                              b_router[0], cnt0, 1, tl2_p)
    h1_s, hn_s, e_s, r_s, wt_s, cnt_s = _mix2(m_s, w_out_b, x_sample, mod_s, g_norm2[0], w_router[0],
                                              b_router[0], cnt_p, bb2_s, ls)

    counts = cnt_s[:, 0].astype(I32)
    rblk = EXPERT_ROWS
    padded = (counts + rblk - 1) // rblk * rblk
    pad_end = jnp.cumsum(padded)
    pad_start = pad_end - padded
    t_all = tp + ts
    nblk = (t_all * TOP_K + ne * (rblk - 1) + rblk - 1) // rblk
    rows = nblk * rblk
    blk_start = jnp.arange(nblk, dtype=I32) * rblk
    block_expert = jnp.minimum(jnp.searchsorted(pad_end, blk_start, side="right"), ne - 1).astype(I32)
    n_used = (pad_end[-1] // rblk).astype(I32).reshape(1)
    valid_rows = jnp.clip(pad_start[block_expert] + counts[block_expert] - blk_start, 0, rblk)
    valid_rows = jnp.where(blk_start < pad_end[-1], valid_rows, 0)
    n_valid = ((valid_rows + SUB_ROWS - 1) // SUB_ROWS).astype(I32)
    zero_start = (pad_start + counts).astype(I32)
    zero_n = ((-counts) % SUB_ROWS).astype(I32)
    e_all = jnp.concatenate([e_p, e_s], axis=1)
    r_all = jnp.concatenate([r_p, r_s], axis=1)
    start_of = jnp.sum(jnp.where(e_all[..., None] == jnp.arange(ne, dtype=I32), pad_start.astype(I32), 0), axis=-1)
    dest = (start_of + r_all).astype(I32).T.reshape(-1)
    tok_major = lambda w: jnp.pad(w.T, ((0, 0), (0, LANES - DEST_W)))

    buf = _dispatch(hn_p, hn_s, dest, zero_start, zero_n, rows, min(tp, ts, 256))
    eo = _expert_ffn(buf, w_exp_gate[0], w_exp_up[0], w_exp_down[0], block_expert, n_valid, n_used, rblk)

    def shared_ffn(hn):
        rb = min(hn.shape[0], rblk)
        nb_ = hn.shape[0] // rb
        return _expert_ffn(hn, w_sh_gate, w_sh_up, w_sh_down, jnp.zeros((nb_,), I32),
                           jnp.full((nb_,), rb // SUB_ROWS, I32), jnp.full((1,), nb_, I32), rb)

    tlc_p = min(s, 128)
    bbc_s = max(1, min(nbs, 128 // ls))
    y_p = _combine(dest[:tp * DEST_W], tok_major(wt_p), shared_ffn(hn_p), h1_p, mod_p, g_final, eo, 1, tlc_p)
    y_s = _combine(dest[tp * DEST_W:], tok_major(wt_s), shared_ffn(hn_s), h1_s, mod_s, g_final, eo, bbc_s, ls)

    kf_p, vf_p, logf_p, s_p = st_p
    kf_s, vf_s, logf_s, s_s = st_s
    return (y_p, y_s, kf_p[None], vf_p[None], logf_p[None], s_p[None],
            kf_s[None], vf_s[None], logf_s[None], s_s[None])
```

```python
import functools
import math

import jax
import jax.numpy as jnp
from jax import lax
from jax.experimental import pallas as pl
from jax.experimental.pallas import tpu as pltpu

F32 = jnp.float32
BF16 = jnp.bfloat16
I32 = jnp.int32

CHUNK = 64
H_RET = 8
DK_RET = 128
DV_RET = 256
H_FOX = 16
DH_FOX = 128
TOP_K = 6
ROUTE_SCALE = 2.5
ROPE_BASE = 10000.0
EPS = 1e-6
LOG2E = math.log2(math.e)

LANES = 128
DEST_W = 8
EXPERT_ROWS = 2048
SUB_ROWS = 256
F_TILE = 256
VMEM_LIMIT = 56 * 1024 * 1024


def _params(sem, vmem=None):
    return pltpu.CompilerParams(dimension_semantics=sem, vmem_limit_bytes=vmem)


def _sigmoid(x):
    return 1.0 / (1.0 + jnp.exp(-x))


def _nt_dot(a, b):
    return lax.dot_general(a, b, (((1,), (1,)), ((), ())), preferred_element_type=F32)


def _pack_halves(x):
    n = x.shape[-1] // 2
    hi = pltpu.bitcast(x[:, :n].astype(BF16).astype(F32), I32)
    lo = pltpu.bitcast(x[:, n:].astype(BF16).astype(F32), I32)
    return hi | lax.shift_right_logical(lo, jnp.int32(16))


def _unpack_halves(u):
    hi = pltpu.bitcast(u & jnp.int32(-65536), F32)
    lo = pltpu.bitcast(lax.shift_left(u, jnp.int32(16)), F32)
    return hi, lo


def _ada_kernel(c_ref, w_ref, b_ref, o_ref):
    c = c_ref[...]
    s = (c * _sigmoid(c)).astype(BF16)
    o_ref[...] = jnp.dot(s, w_ref[...].astype(BF16), preferred_element_type=F32) + b_ref[...]


def _adaln(c, w_ada, b_ada):
    nb, d = c.shape
    n = w_ada.shape[1]
    tn = 1024
    return pl.pallas_call(
        _ada_kernel,
        grid=(n // tn,),
        in_specs=[pl.BlockSpec((nb, d), lambda j: (0, 0)),
                  pl.BlockSpec((d, tn), lambda j: (0, j)),
                  pl.BlockSpec((1, tn), lambda j: (0, j))],
        out_specs=pl.BlockSpec((nb, tn), lambda j: (0, j)),
        out_shape=jax.ShapeDtypeStruct((nb, n), F32),
        compiler_params=_params(("arbitrary",), VMEM_LIMIT),
        name="adaln",
    )(c, w_ada, b_ada.reshape(1, n))


def _norm1_kernel(x_ref, g_ref, sh_ref, sc_ref, wf_ref, bf_ref, hn_ref, logf_ref):
    bb, tl, d = x_ref.shape
    x = x_ref[...]
    y = x * lax.rsqrt(jnp.mean(x * x, axis=-1, keepdims=True) + EPS) * g_ref[...]
    hn = (y * (1.0 + sc_ref[...]) + sh_ref[...]).astype(BF16)
    hn_ref[...] = hn
    z = jnp.dot(hn.reshape(bb * tl, d), wf_ref[...], preferred_element_type=F32) + bf_ref[...]
    logf = jnp.minimum(z, 0.0) - jnp.log(1.0 + jnp.exp(-jnp.abs(z)))
    logf_ref[...] = logf[:, :H_FOX].reshape(bb, tl, H_FOX)


def _norm1(x, g, mod, wf, bfg, bb, tl):
    nb, l, d = x.shape
    mod_spec = lambda k: pl.BlockSpec((bb, 1, d), lambda i, j: (i, 0, k))
    return pl.pallas_call(
        _norm1_kernel,
        grid=(nb // bb, l // tl),
        in_specs=[pl.BlockSpec((bb, tl, d), lambda i, j: (i, j, 0)),
                  pl.BlockSpec((1, 1, d), lambda i, j: (0, 0, 0)),
                  mod_spec(0), mod_spec(1),
                  pl.BlockSpec((d, LANES), lambda i, j: (0, 0)),
                  pl.BlockSpec((1, LANES), lambda i, j: (0, 0))],
        out_specs=[pl.BlockSpec((bb, tl, d), lambda i, j: (i, j, 0)),
                   pl.BlockSpec((bb, tl, H_FOX), lambda i, j: (i, j, 0))],
        out_shape=[jax.ShapeDtypeStruct((nb, l, d), BF16),
                   jax.ShapeDtypeStruct((nb, l, H_FOX), F32)],
        compiler_params=_params(("arbitrary", "arbitrary"), VMEM_LIMIT),
        name="norm1",
    )(x, g.reshape(1, 1, d), mod, mod, wf, bfg)


def _mm_kernel(a_ref, w_ref, o_ref):
    o_ref[...] = jnp.dot(a_ref[...], w_ref[...], preferred_element_type=F32).astype(o_ref.dtype)


def _matmul(a, w, out_dtype, tm, tn, name):
    m, k = a.shape
    n = w.shape[1]
    return pl.pallas_call(
        _mm_kernel,
        grid=(m // tm, n // tn),
        in_specs=[pl.BlockSpec((tm, k), lambda i, j: (i, 0)),
                  pl.BlockSpec((k, tn), lambda i, j: (0, j))],
        out_specs=pl.BlockSpec((tm, tn), lambda i, j: (i, j)),
        out_shape=jax.ShapeDtypeStruct((m, n), out_dtype),
        compiler_params=_params(("arbitrary", "arbitrary"), VMEM_LIMIT),
        name=name,
    )(a, w)


def _mm_heads_kernel(a_ref, w_ref, o_ref, ob_ref):
    bb, tl, nh, dh = o_ref.shape
    res = jnp.dot(a_ref[...], w_ref[...], preferred_element_type=F32)
    ob_ref[...] = res.astype(BF16)
    for hh in range(nh):
        o_ref[:, :, hh, :] = res[:, hh * dh:(hh + 1) * dh].reshape(bb, tl, dh)


def _matmul_heads(a, w, nb, l, bb, tl, name):
    m, k = a.shape
    n = w.shape[1]
    nj = l // tl
    return pl.pallas_call(
        _mm_heads_kernel,
        grid=(nb // bb, nj),
        in_specs=[pl.BlockSpec((bb * tl, k), lambda i, j: (i * nj + j, 0)),
                  pl.BlockSpec((k, n), lambda i, j: (0, 0))],
        out_specs=[pl.BlockSpec((bb, tl, H_FOX, DH_FOX), lambda i, j: (i, j, 0, 0)),
                   pl.BlockSpec((bb * tl, n), lambda i, j: (i * nj + j, 0))],
        out_shape=[jax.ShapeDtypeStruct((nb, l, H_FOX, DH_FOX), F32),
                   jax.ShapeDtypeStruct((m, n), BF16)],
        compiler_params=_params(("arbitrary", "arbitrary"), VMEM_LIMIT),
        name=name,
    )(a, w)


def _cumsum_kernel(x_ref, o_ref, carry_ref, *, scale):
    tl = x_ref.shape[-1]

    @pl.when(pl.program_id(1) == 0)
    def _():
        carry_ref[...] = jnp.zeros_like(carry_ref)

    r = lax.broadcasted_iota(I32, (tl, tl), 0)
    c = lax.broadcasted_iota(I32, (tl, tl), 1)
    tri = (r <= c).astype(F32)
    y = jnp.dot(x_ref[0], tri, precision=lax.Precision.HIGHEST, preferred_element_type=F32)
    y = y + carry_ref[:, 0:1]
    o_ref[0] = y * scale
    carry_ref[...] = jnp.broadcast_to(y[:, tl - 1:tl], carry_ref.shape)


def _cumsum_rows(x, tl, scale):
    nb, h, s = x.shape
    return pl.pallas_call(
        functools.partial(_cumsum_kernel, scale=scale),
        grid=(nb, s // tl),
        in_specs=[pl.BlockSpec((1, h, tl), lambda b, t: (b, 0, t))],
        out_specs=pl.BlockSpec((1, h, tl), lambda b, t: (b, 0, t)),
        out_shape=jax.ShapeDtypeStruct((nb, h, s), F32),
        scratch_shapes=[pltpu.VMEM((h, LANES), F32)],
        compiler_params=_params(("arbitrary", "arbitrary"), VMEM_LIMIT),
        name="cumsum_logf",
    )(x)


def _ret_kernel(q_ref, k_ref, v_ref, g_ref, cos_ref, sin_ref, dmat_ref, qdec_ref, kdec_ref, sdec_ref,
                s0_ref, gn_ref, o_ref, sout_ref, s_scr, *, heads):
    t = pl.program_id(2)
    hs = range(heads)

    @pl.when(t == 0)
    def _():
        s_scr[...] = s0_ref[0]

    cos = cos_ref[...]
    sin = sin_ref[...]
    half = DK_RET // 2
    ks = [slice(hh * DK_RET, (hh + 1) * DK_RET) for hh in hs]
    vs = [slice(hh * DV_RET, (hh + 1) * DV_RET) for hh in hs]
    q = [q_ref[0, :, ks[hh]] for hh in hs]
    k = [k_ref[0, :, ks[hh]] for hh in hs]
    qr = [q[hh] * cos + pltpu.roll(q[hh], half, 1) * sin for hh in hs]
    kr = [k[hh] * cos + pltpu.roll(k[hh], half, 1) * sin for hh in hs]
    vb = [v_ref[0, :, vs[hh]] for hh in hs]
    state = [s_scr[hh] for hh in hs]
    s = [_nt_dot(qr[hh].astype(BF16), kr[hh].astype(BF16)) * dmat_ref[hh] for hh in hs]
    o = [jnp.dot(s[hh].astype(BF16), vb[hh], preferred_element_type=F32) for hh in hs]
    o = [o[hh] + jnp.dot((qr[hh] * qdec_ref[hh]).astype(BF16), state[hh].astype(BF16),
                         preferred_element_type=F32) for hh in hs]
    kd_t = [(kr[hh] * kdec_ref[hh]).T.astype(BF16) for hh in hs]
    s_new = [state[hh] * sdec_ref[hh] + jnp.dot(kd_t[hh], vb[hh], preferred_element_type=F32) for hh in hs]
    for hh in hs:
        s_scr[hh] = s_new[hh]

    @pl.when(t == pl.num_programs(2) - 1)
    def _():
        sout_ref[0] = s_scr[...]

    mu = [jnp.mean(o[hh], axis=-1, keepdims=True) for hh in hs]
    oc = [o[hh] - mu[hh] for hh in hs]
    var = [jnp.mean(oc[hh] * oc[hh], axis=-1, keepdims=True) for hh in hs]
    on = [oc[hh] * lax.rsqrt(var[hh] + EPS) * gn_ref[:, vs[hh]] for hh in hs]
    g = [g_ref[0, :, vs[hh]].astype(F32) for hh in hs]
    for hh in hs:
        o_ref[0, :, vs[hh]] = (on[hh] * (g[hh] * _sigmoid(g[hh]))).astype(BF16)


def _retention(zqk, zb, s0, g_ret_gn, pos0, tb, heads=4):
    nb, l, _ = zqk.shape
    nt = l // tb
    half = DK_RET // 2
    inv = 1.0 / (ROPE_BASE ** (jnp.arange(half, dtype=F32) * 2.0 / DK_RET))
    ang = (pos0 + jnp.arange(l)).astype(F32)[:, None] * inv[None, :]
    cos = jnp.concatenate([jnp.cos(ang), jnp.cos(ang)], axis=-1)
    sin = jnp.concatenate([-jnp.sin(ang), jnp.sin(ang)], axis=-1)
    lg = jnp.log1p(-jnp.exp2(-5.0 - jnp.arange(H_RET, dtype=F32)))
    i = jnp.arange(tb)
    d = (i[:, None] - i[None, :]).astype(F32)
    ci, cj = i[:, None] // CHUNK, i[None, :] // CHUNK
    expo = jnp.where(ci == cj, jnp.abs(d), d)
    dmat = jnp.where((cj <= ci)[None], jnp.exp(expo[None] * lg[:, None, None]), 0.0)
    fi = i.astype(F32)
    qdec = jnp.broadcast_to(jnp.exp((fi[None, :] + 1.0) * lg[:, None])[:, :, None], (H_RET, tb, DK_RET))
    kdec = jnp.broadcast_to(jnp.exp((tb - 1.0 - fi[None, :]) * lg[:, None])[:, :, None], (H_RET, tb, DK_RET))
    sdec = jnp.broadcast_to(jnp.exp(tb * lg)[:, None, None], (H_RET, 1, DV_RET))
    ng = H_RET // heads
    kw, vw = heads * DK_RET, heads * DV_RET
    head_spec = lambda shape: pl.BlockSpec((heads,) + shape, lambda b, g, t: (g, 0, 0))
    return pl.pallas_call(
        functools.partial(_ret_kernel, heads=heads),
        grid=(nb, ng, nt),
        in_specs=[pl.BlockSpec((1, tb, kw), lambda b, g, t: (b, t, g)),
                  pl.BlockSpec((1, tb, kw), lambda b, g, t: (b, t, ng + g)),
                  pl.BlockSpec((1, tb, vw), lambda b, g, t: (b, t, g)),
                  pl.BlockSpec((1, tb, vw), lambda b, g, t: (b, t, ng + g)),
                  pl.BlockSpec((tb, DK_RET), lambda b, g, t: (t, 0)),
                  pl.BlockSpec((tb, DK_RET), lambda b, g, t: (t, 0)),
                  head_spec((tb, tb)), head_spec((tb, DK_RET)), head_spec((tb, DK_RET)), head_spec((1, DV_RET)),
                  pl.BlockSpec((1, heads, DK_RET, DV_RET), lambda b, g, t: (b, g, 0, 0)),
                  pl.BlockSpec((1, vw), lambda b, g, t: (0, g))],
        out_specs=[pl.BlockSpec((1, tb, vw), lambda b, g, t: (b, t, g)),
                   pl.BlockSpec((1, heads, DK_RET, DV_RET), lambda b, g, t: (b, g, 0, 0))],
        out_shape=[jax.ShapeDtypeStruct((nb, l, H_RET * DV_RET), BF16),
                   jax.ShapeDtypeStruct((nb, H_RET, DK_RET, DV_RET), F32)],
        scratch_shapes=[pltpu.VMEM((heads, DK_RET, DV_RET), F32)],
        compiler_params=_params(("arbitrary", "arbitrary", "arbitrary"), VMEM_LIMIT),
        name="retention",
    )(zqk, zqk, zb, zb, cos, sin, dmat, qdec, kdec, sdec, s0, g_ret_gn.reshape(1, -1))


def _fox_prompt_kernel(q_ref, k_ref, v_ref, fk_ref, ftok_ref, o_ref, m_scr, l_scr, acc_scr, fq_scr, *, heads, tk):
    g = pl.program_id(1)
    qi = pl.program_id(2)
    tq = q_ref.shape[1]
    nck = tk // LANES
    hs = range(heads)
    sls = [slice(hh * DH_FOX, (hh + 1) * DH_FOX) for hh in hs]
    ft = ftok_ref[0]
    lane = lax.broadcasted_iota(I32, ft.shape, 1)
    for hh in hs:
        m_scr[hh] = jnp.full((tq, LANES), -jnp.inf, F32)
        l_scr[hh] = jnp.zeros((tq, LANES), F32)
        acc_scr[hh] = jnp.zeros((tq, DH_FOX), F32)
        fq = jnp.sum(jnp.where(lane == g * heads + hh, ft, 0.0), axis=-1, keepdims=True)
        fq_scr[hh] = jnp.broadcast_to(fq, (tq, LANES))
    ones = jnp.ones((tk, LANES), BF16)

    def steps(ki, masked):
        rows = pl.ds(pl.multiple_of(ki * tk, tk), tk)
        m_prev = [m_scr[hh] for hh in hs]
        t = [_nt_dot(q_ref[0, :, sls[hh]], k_ref[0, rows, sls[hh]]) for hh in hs]
        t = [t[hh] - fk_ref[0, hh, pl.ds(ki, 1), :] for hh in hs]
        if masked:
            row = lax.broadcasted_iota(I32, (tq, tk), 0)
            col = lax.broadcasted_iota(I32, (tq, tk), 1)
            t = [jnp.where(col <= row, t[hh], -jnp.inf) for hh in hs]
        m_new = [jnp.maximum(m_prev[hh], jnp.max(t[hh], axis=-1, keepdims=True) + fq_scr[hh]) for hh in hs]
        shift = [m_new[hh] - fq_scr[hh] for hh in hs]
        p = [jnp.concatenate([jnp.exp2(t[hh][:, c * LANES:(c + 1) * LANES] - shift[hh]) for c in range(nck)],
                             axis=1).astype(BF16) for hh in hs]
        alpha = [jnp.exp2(m_prev[hh] - m_new[hh]) for hh in hs]
        pv = [jnp.dot(p[hh], jnp.concatenate([v_ref[0, rows, sls[hh]], ones], axis=1),
                      preferred_element_type=F32) for hh in hs]
        for hh in hs:
            l_scr[hh] = alpha[hh] * l_scr[hh] + pv[hh][:, DH_FOX:]
            acc_scr[hh] = alpha[hh] * acc_scr[hh] + pv[hh][:, :DH_FOX]
            m_scr[hh] = m_new[hh]

    def body(ki, carry):
        steps(ki, False)
        return carry

    lax.fori_loop(0, qi, body, 0)
    steps(qi, True)
    for hh in hs:
        o_ref[0, :, sls[hh]] = (acc_scr[hh] / l_scr[hh]).astype(BF16)


def _fox_prompt(zb, kb, vb, f_rows, f_tok, tq, heads=2):
    nb, s, hd = kb.shape
    nq = s // tq
    w = heads * DH_FOX
    q_off = (2 * hd) // w
    kv_spec = pl.BlockSpec((1, s, w), lambda b, g, qi: (b, 0, g))
    return pl.pallas_call(
        functools.partial(_fox_prompt_kernel, heads=heads, tk=tq),
        grid=(nb, H_FOX // heads, nq),
        in_specs=[pl.BlockSpec((1, tq, w), lambda b, g, qi: (b, qi, q_off + g)),
                  kv_spec, kv_spec,
                  pl.BlockSpec((1, heads, nq, tq), lambda b, g, qi: (b, g, 0, 0)),
                  pl.BlockSpec((1, tq, H_FOX), lambda b, g, qi: (b, qi, 0))],
        out_specs=pl.BlockSpec((1, tq, w), lambda b, g, qi: (b, qi, g)),
        out_shape=jax.ShapeDtypeStruct((nb, s, hd), BF16),
        scratch_shapes=[pltpu.VMEM((heads, tq, LANES), F32), pltpu.VMEM((heads, tq, LANES), F32),
                        pltpu.VMEM((heads, tq, DH_FOX), F32), pltpu.VMEM((heads, tq, LANES), F32)],
        compiler_params=_params(("arbitrary",) * 3, VMEM_LIMIT),
        name="fox_prompt",
    )(zb, kb, vb, f_rows.reshape(nb, H_FOX, nq, tq), f_tok)


def _fox_sample_kernel(q_ref, ck_ref, cv_ref, kn_ref, vn_ref, fk_ref, ftok_ref, o_ref, *, heads):
    g = pl.program_id(1)
    l = q_ref.shape[1]
    p_len = ck_ref.shape[2]
    ft = ftok_ref[0]
    lane = lax.broadcasted_iota(I32, ft.shape, 1)
    row = lax.broadcasted_iota(I32, (l, l), 0)
    col = lax.broadcasted_iota(I32, (l, l), 1)
    hs = range(heads)
    sls = [slice(hh * DH_FOX, (hh + 1) * DH_FOX) for hh in hs]
    fk = [fk_ref[0, pl.ds(g * heads + hh, 1), :] for hh in hs]
    fq = [jnp.sum(jnp.where(lane == g * heads + hh, ft, 0.0), axis=-1, keepdims=True) for hh in hs]
    kc = [ck_ref[0, 0, :, hh, :].astype(BF16) for hh in hs]
    s1 = [_nt_dot(q_ref[0, :, sls[hh]], kc[hh]) + fq[hh] - fk[hh][:, :p_len] for hh in hs]
    s2 = [_nt_dot(q_ref[0, :, sls[hh]], kn_ref[0, :, sls[hh]]) + fq[hh] - fk[hh][:, p_len:p_len + l] for hh in hs]
    s2 = [jnp.where(col <= row, s2[hh], -jnp.inf) for hh in hs]
    m = [jnp.maximum(jnp.max(s1[hh], axis=-1, keepdims=True), jnp.max(s2[hh], axis=-1, keepdims=True)) for hh in hs]
    p1 = [jnp.exp2(s1[hh] - m[hh]) for hh in hs]
    p2 = [jnp.exp2(s2[hh] - m[hh]) for hh in hs]
    den = [jnp.sum(p1[hh], axis=-1, keepdims=True) + jnp.sum(p2[hh], axis=-1, keepdims=True) for hh in hs]
    vc = [cv_ref[0, 0, :, hh, :].astype(BF16) for hh in hs]
    o = [jnp.dot(p1[hh].astype(BF16), vc[hh], preferred_element_type=F32)
         + jnp.dot(p2[hh].astype(BF16), vn_ref[0, :, sls[hh]], preferred_element_type=F32) for hh in hs]
    for hh in hs:
        o_ref[0, :, sls[hh]] = (o[hh] / den[hh]).astype(BF16)


def _fox_sample(zb, kb, vb, cache_k, cache_v, f_rows, f_tok, heads=8):
    nb, l, hd = kb.shape
    p_len = cache_k.shape[2]
    sp = f_rows.shape[-1]
    w = heads * DH_FOX
    q_off = (2 * hd) // w
    assert p_len % l == 0
    new_spec = pl.BlockSpec((1, l, w), lambda b, g: (b, 0, g))
    cache_spec = pl.BlockSpec((1, 1, p_len, heads, DH_FOX), lambda b, g: (0, b, 0, g, 0))
    return pl.pallas_call(
        functools.partial(_fox_sample_kernel, heads=heads),
        grid=(nb, H_FOX // heads),
        in_specs=[pl.BlockSpec((1, l, w), lambda b, g: (b, 0, q_off + g)),
                  cache_spec, cache_spec, new_spec, new_spec,
                  pl.BlockSpec((1, H_FOX, sp), lambda b, g: (b, 0, 0)),
                  pl.BlockSpec((1, l, H_FOX), lambda b, g: (b, p_len // l, 0))],
        out_specs=new_spec,
        out_shape=jax.ShapeDtypeStruct((nb, l, hd), BF16),
        compiler_params=_params(("arbitrary", "arbitrary"), VMEM_LIMIT),
        name="fox_sample",
    )(zb, cache_k, cache_v, kb, vb, f_rows, f_tok)


def _mix1_kernel(or_ref, of_ref, w1_ref, w2_ref, gr_ref, gf_ref, o_ref):
    a = jnp.dot(or_ref[...], w1_ref[...], preferred_element_type=F32)
    b = jnp.dot(of_ref[...], w2_ref[...], preferred_element_type=F32)
    gr = _sigmoid(gr_ref[...].astype(F32))
    gf = _sigmoid(gf_ref[...].astype(F32))
    o_ref[...] = (gr * a + gf * b).astype(BF16)


def _mix1(o_r, o_f, w1, w2, zb, tm, tn):
    m, d = o_r.shape
    gr_off = (3 * d) // tn
    gf_off = (4 * d) // tn
    return pl.pallas_call(
        _mix1_kernel,
        grid=(m // tm, d // tn),
        in_specs=[pl.BlockSpec((tm, d), lambda i, j: (i, 0)),
                  pl.BlockSpec((tm, d), lambda i, j: (i, 0)),
                  pl.BlockSpec((d, tn), lambda i, j: (0, j)),
                  pl.BlockSpec((d, tn), lambda i, j: (0, j)),
                  pl.BlockSpec((tm, tn), lambda i, j: (i, gr_off + j)),
                  pl.BlockSpec((tm, tn), lambda i, j: (i, gf_off + j))],
        out_specs=pl.BlockSpec((tm, tn), lambda i, j: (i, j)),
        out_shape=jax.ShapeDtypeStruct((m, d), BF16),
        compiler_params=_params(("arbitrary", "arbitrary"), VMEM_LIMIT),
        name="mix_gates",
    )(o_r, o_f, w1, w2, zb, zb)


def _mix2_kernel(m_ref, w_ref, x_ref, gt_ref, sh_ref, sc_ref, g_ref, wrt_ref, br_ref, cnt0_ref,
                 h1_ref, hn_ref, eidx_ref, rank_ref, wts_ref, cnt_ref):
    bb, tl, d = x_ref.shape
    tm = bb * tl
    ne = wrt_ref.shape[0]

    @pl.when((pl.program_id(0) == 0) & (pl.program_id(1) == 0))
    def _():
        cnt_ref[...] = cnt0_ref[...]

    mix = jnp.dot(m_ref[...].reshape(tm, d), w_ref[...], preferred_element_type=F32)
    h1 = x_ref[...] + gt_ref[...] * mix.reshape(bb, tl, d)
    h1_ref[...] = h1
    y = h1 * lax.rsqrt(jnp.mean(h1 * h1, axis=-1, keepdims=True) + EPS) * g_ref[...]
    hn = (y * (1.0 + sc_ref[...]) + sh_ref[...]).reshape(tm, d)
    hn_ref[...] = _pack_halves(hn)

    logits = lax.dot_general(wrt_ref[...], hn, (((1,), (1,)), ((), ())), precision=lax.Precision.HIGHEST,
                             preferred_element_type=F32)
    scores = _sigmoid(logits)
    cur = scores + br_ref[...]
    eid = lax.broadcasted_iota(I32, (ne, tm), 0).astype(F32)
    slot = lax.broadcasted_iota(I32, (DEST_W, tm), 0)
    r = lax.broadcasted_iota(I32, (tm, tm), 0)
    c = lax.broadcasted_iota(I32, (tm, tm), 1)
    earlier = (r < c).astype(BF16)
    picks = []
    mask = jnp.zeros((ne, tm), F32)
    for _ in range(TOP_K):
        mx = jnp.max(cur, axis=0, keepdims=True)
        idx = jnp.min(jnp.where(cur == mx, eid, float(ne)), axis=0, keepdims=True)
        pick = eid == idx
        picks.append((idx, pick))
        mask = jnp.where(pick, 1.0, mask)
        cur = jnp.where(pick, -jnp.inf, cur)
    rank = jnp.dot(mask.astype(BF16), earlier, preferred_element_type=F32) + cnt_ref[:, 0:1]
    cnt_ref[...] = cnt_ref[...] + jnp.sum(mask, axis=1, keepdims=True)
    wsum = jnp.sum(mask * scores, axis=0, keepdims=True)
    eidx = jnp.zeros((DEST_W, tm), I32)
    rnk = jnp.zeros((DEST_W, tm), I32)
    wts = jnp.zeros((DEST_W, tm), F32)
    for kk, (idx, pick) in enumerate(picks):
        sc_k = jnp.sum(jnp.where(pick, scores, 0.0), axis=0, keepdims=True)
        rk_k = jnp.sum(jnp.where(pick, rank, 0.0), axis=0, keepdims=True)
        eidx = jnp.where(slot == kk, idx.astype(I32), eidx)
        rnk = jnp.where(slot == kk, rk_k.astype(I32), rnk)
        wts = jnp.where(slot == kk, sc_k / wsum * ROUTE_SCALE, wts)
    eidx_ref[...] = eidx
    rank_ref[...] = rnk
    wts_ref[...] = wts


def _mix2(m3, w_out, x, mod, g2, w_router, b_router, cnt0, bb, tl):
    nb, l, d = x.shape
    ne = w_router.shape[1]
    t = nb * l
    tm = bb * tl
    nj = l // tl
    mod_spec = lambda k: pl.BlockSpec((bb, 1, d), lambda i, j: (i, 0, k))
    slot_spec = pl.BlockSpec((DEST_W, tm), lambda i, j: (0, i * nj + j))
    blk3 = pl.BlockSpec((bb, tl, d), lambda i, j: (i, j, 0))
    return pl.pallas_call(
        _mix2_kernel,
        grid=(nb // bb, nj),
        in_specs=[blk3,
                  pl.BlockSpec((d, d), lambda i, j: (0, 0), pipeline_mode=pl.Buffered(1)),
                  blk3,
                  mod_spec(2), mod_spec(3), mod_spec(4),
                  pl.BlockSpec((1, 1, d), lambda i, j: (0, 0, 0)),
                  pl.BlockSpec((ne, d), lambda i, j: (0, 0)),
                  pl.BlockSpec((ne, 1), lambda i, j: (0, 0)),
                  pl.BlockSpec((ne, LANES), lambda i, j: (0, 0))],
        out_specs=[blk3, pl.BlockSpec((tm, d // 2), lambda i, j: (i * nj + j, 0)), slot_spec, slot_spec, slot_spec,
                   pl.BlockSpec((ne, LANES), lambda i, j: (0, 0))],
        out_shape=[jax.ShapeDtypeStruct((nb, l, d), F32),
                   jax.ShapeDtypeStruct((t, d // 2), I32),
                   jax.ShapeDtypeStruct((DEST_W, t), I32),
                   jax.ShapeDtypeStruct((DEST_W, t), I32),
                   jax.ShapeDtypeStruct((DEST_W, t), F32),
                   jax.ShapeDtypeStruct((ne, LANES), F32)],
        compiler_params=_params(("arbitrary", "arbitrary"), VMEM_LIMIT),
        name="outproj_norm2_router",
    )(m3, w_out, x, mod, mod, mod, g2.reshape(1, 1, d), w_router.T, b_router.reshape(ne, 1), cnt0)


def _dispatch_kernel(zs_ref, zn_ref, dest_ref, xp_ref, xs_ref, buf_ref, zero_scr, sem, *, np_blocks):
    i = pl.program_id(0)
    tb = xp_ref.shape[0]

    def scatter(x_ref):
        def row_copy(r, kk):
            d = dest_ref[r * DEST_W + kk]
            return pltpu.make_async_copy(x_ref.at[pl.ds(r, 1), :], buf_ref.at[pl.ds(d, 1), :], sem)

        def issue(r, carry):
            for kk in range(TOP_K):
                row_copy(r, kk).start()
            return carry

        def drain(r, carry):
            for kk in range(TOP_K):
                row_copy(r, kk).wait()
            return carry

        lax.fori_loop(0, tb, issue, 0)
        lax.fori_loop(0, tb, drain, 0)

    @pl.when(i < np_blocks)
    def _():
        scatter(xp_ref)

    @pl.when(i >= np_blocks)
    def _():
        scatter(xs_ref)

    @pl.when(i == pl.num_programs(0) - 1)
    def _():
        zero_scr[...] = jnp.zeros_like(zero_scr)

        def per_expert(e, carry):
            start = zs_ref[e]

            def zero_copy(r):
                return pltpu.make_async_copy(zero_scr.at[pl.ds(0, 1), :], buf_ref.at[pl.ds(start + r, 1), :], sem)

            def issue(r, c):
                zero_copy(r).start()
                return c

            def drain(r, c):
                zero_copy(r).wait()
                return c

            lax.fori_loop(0, zn_ref[e], issue, 0)
            lax.fori_loop(0, zn_ref[e], drain, 0)
            return carry

        lax.fori_loop(0, zs_ref.shape[0], per_expert, 0)


def _dispatch(xp, xs, dest, zero_start, zero_n, rows, tb):
    tp, w = xp.shape
    ts = xs.shape[0]
    npb = tp // tb
    grid_spec = pltpu.PrefetchScalarGridSpec(
        num_scalar_prefetch=2,
        grid=(npb + ts // tb,),
        in_specs=[pl.BlockSpec((tb * DEST_W,), lambda i, zs, zn: (i,), memory_space=pltpu.SMEM),
                  pl.BlockSpec((tb, w), lambda i, zs, zn: (jnp.minimum(i, npb - 1), 0)),
                  pl.BlockSpec((tb, w), lambda i, zs, zn: (jnp.maximum(i - npb, 0), 0))],
        out_specs=pl.BlockSpec(memory_space=pl.ANY),
        scratch_shapes=[pltpu.VMEM((8, w), I32), pltpu.SemaphoreType.DMA(())],
    )
    return pl.pallas_call(
        functools.partial(_dispatch_kernel, np_blocks=npb),
        grid_spec=grid_spec,
        out_shape=jax.ShapeDtypeStruct((rows, w), I32),
        compiler_params=_params(("arbitrary",), VMEM_LIMIT),
        name="moe_dispatch",
    )(zero_start, zero_n, dest, xp, xs)


def _expert_kernel(be_ref, nv_ref, nu_ref, x_ref, wg_ref, wu_ref, wgt_ref, wut_ref, wda_ref, wdb_ref, o_ref,
                   xb_scr, hid_scr, wgb_scr, wub_scr, wdn_scr, *, n_main):
    del be_ref, nu_ref
    i = pl.program_id(0)
    j = pl.program_id(1)
    half = x_ref.shape[1]
    ft = wg_ref.shape[2]
    tail = wgt_ref.shape[2]
    ct = wda_ref.shape[2]
    nv = nv_ref[i]
    big = 4 * SUB_ROWS
    n_big = nv // 4
    rem = nv % 4

    def for_chunks(fn):
        def body(c, carry):
            fn(pl.multiple_of(c * big, big), big)
            return carry

        lax.fori_loop(0, n_big, body, 0)

        @pl.when(rem >= 2)
        def _():
            fn(pl.multiple_of(n_big * big, SUB_ROWS), 2 * SUB_ROWS)

        @pl.when(rem % 2 == 1)
        def _():
            fn(pl.multiple_of((nv - 1) * SUB_ROWS, SUB_ROWS), SUB_ROWS)

    @pl.when(nv > 0)
    def _():
        @pl.when(j == 0)
        def _():
            def unpack(r0, rows):
                hi, lo = _unpack_halves(x_ref[pl.ds(r0, rows), :])
                xb_scr[pl.ds(r0, rows), :half] = hi.astype(BF16)
                xb_scr[pl.ds(r0, rows), half:] = lo.astype(BF16)

            for_chunks(unpack)

        def up_phase(wg, wu, width):
            wgb_scr[:, :width] = wg[0].astype(BF16)
            wub_scr[:, :width] = wu[0].astype(BF16)

            def chunk(r0, rows):
                xb = xb_scr[pl.ds(r0, rows), :]
                gate = jnp.dot(xb, wgb_scr[:, :width], preferred_element_type=F32)
                up = jnp.dot(xb, wub_scr[:, :width], preferred_element_type=F32)
                hid_scr[j, pl.ds(r0, rows), :width] = (gate * _sigmoid(gate) * up).astype(BF16)

            for_chunks(chunk)

        @pl.when(j < n_main)
        def _():
            up_phase(wg_ref, wu_ref, ft)

        @pl.when(j == n_main)
        def _():
            wgb_scr[:, :tail] = wgt_ref[0].astype(BF16)
            wgb_scr[:, tail:2 * tail] = wut_ref[0].astype(BF16)

            def chunk(r0, rows):
                gu = jnp.dot(xb_scr[pl.ds(r0, rows), :], wgb_scr[:, :2 * tail], preferred_element_type=F32)
                gate, up = gu[:, :tail], gu[:, tail:]
                hid_scr[j, pl.ds(r0, rows), :tail] = (gate * _sigmoid(gate) * up).astype(BF16)

            for_chunks(chunk)

        @pl.when(j > n_main)
        def _():
            wdn_scr[:, :ct] = wda_ref[0].astype(BF16)
            wdn_scr[:, ct:] = wdb_ref[0].astype(BF16)

            def chunk(r0, rows):
                pieces = [hid_scr[t, pl.ds(r0, rows), :] for t in range(n_main)]
                pieces.append(hid_scr[n_main, pl.ds(r0, rows), :tail])
                hid = jnp.concatenate(pieces, axis=1)
                res = jnp.dot(hid, wdn_scr[...], preferred_element_type=F32)
                o_ref[pl.ds(r0, rows), :] = _pack_halves(res)

            for_chunks(chunk)

    @pl.when(j > n_main)
    def _():
        def body(c, carry):
            r0 = pl.multiple_of(c * SUB_ROWS, SUB_ROWS)
            o_ref[pl.ds(r0, SUB_ROWS), :] = jnp.zeros((SUB_ROWS, o_ref.shape[1]), I32)
            return carry

        lax.fori_loop(nv, o_ref.shape[0] // SUB_ROWS, body, 0)


def _expert_ffn(xp, wg, wu, wd, block_expert, n_valid, n_used, rb):
    rows, half = xp.shape
    d = 2 * half
    f = wg.shape[2]
    n_main = f // F_TILE
    tail = f - n_main * F_TILE
    assert tail > 0 and tail % LANES == 0 and (n_main * F_TILE) % tail == 0
    tail_blk = (n_main * F_TILE) // tail
    ct = F_TILE
    n_out = half // ct
    nblk = rows // rb

    def main_col(i, j, be, nv, nu):
        return (be[i], 0, jnp.minimum(j, n_main - 1))

    def out_col(j):
        return jnp.clip(j - (n_main + 1), 0, n_out - 1)

    grid_spec = pltpu.PrefetchScalarGridSpec(
        num_scalar_prefetch=3,
        grid=(nblk, n_main + 1 + n_out),
        in_specs=[pl.BlockSpec((rb, half), lambda i, j, be, nv, nu: (jnp.minimum(i, nu[0] - 1), 0)),
                  pl.BlockSpec((1, d, F_TILE), main_col),
                  pl.BlockSpec((1, d, F_TILE), main_col),
                  pl.BlockSpec((1, d, tail), lambda i, j, be, nv, nu: (be[i], 0, tail_blk)),
                  pl.BlockSpec((1, d, tail), lambda i, j, be, nv, nu: (be[i], 0, tail_blk)),
                  pl.BlockSpec((1, f, ct), lambda i, j, be, nv, nu: (be[i], 0, out_col(j))),
                  pl.BlockSpec((1, f, ct), lambda i, j, be, nv, nu: (be[i], 0, n_out + out_col(j)))],
        out_specs=pl.BlockSpec((rb, ct), lambda i, j, be, nv, nu: (i, out_col(j))),
        scratch_shapes=[pltpu.VMEM((rb, d), BF16), pltpu.VMEM((n_main + 1, rb, F_TILE), BF16),
                        pltpu.VMEM((d, F_TILE), BF16), pltpu.VMEM((d, F_TILE), BF16),
                        pltpu.VMEM((f, 2 * ct), BF16)],
    )
    return pl.pallas_call(
        functools.partial(_expert_kernel, n_main=n_main),
        grid_spec=grid_spec,
        out_shape=jax.ShapeDtypeStruct((rows, half), I32),
        compiler_params=_params(("arbitrary", "arbitrary"), VMEM_LIMIT),
        name="expert_ffn",
    )(block_expert, n_valid, n_used, xp, wg, wu, wg, wu, wd, wd)


def _combine_kernel(dcur_ref, dnext_ref, wts_ref, sh_ref, h1_ref, gt_ref, g_ref, eo_ref, y_ref,
                    gbuf, sems):
    bb, tl, d = h1_ref.shape
    tb = bb * tl
    half = d // 2
    nj = pl.num_programs(1)
    step = pl.program_id(0) * nj + pl.program_id(1)
    nsteps = pl.num_programs(0) * nj
    slot = step % 2

    def row_copy(dref, sl, r, kk):
        dd = dref[r * DEST_W + kk]
        return pltpu.make_async_copy(eo_ref.at[pl.ds(dd, 1), :], gbuf.at[sl, kk, pl.ds(r, 1), :],
                                     sems.at[sl])

    def issue(dref, sl):
        def body(r, carry):
            for kk in range(TOP_K):
                row_copy(dref, sl, r, kk).start()
            return carry
        lax.fori_loop(0, tb, body, 0)

    @pl.when(step == 0)
    def _():
        issue(dcur_ref, 0)

    @pl.when(step + 1 < nsteps)
    def _():
        issue(dnext_ref, 1 - slot)

    def drain(r, carry):
        for kk in range(TOP_K):
            row_copy(dcur_ref, slot, r, kk).wait()
        return carry
    lax.fori_loop(0, tb, drain, 0)

    wts = wts_ref[...]
    s_hi, s_lo = _unpack_halves(sh_ref[...])
    for kk in range(TOP_K):
        hi, lo = _unpack_halves(gbuf[slot, kk])
        wk = wts[:, kk:kk + 1]
        s_hi = s_hi + wk * hi
        s_lo = s_lo + wk * lo
    h2_hi = h1_ref[:, :, :half] + gt_ref[:, :, :half] * s_hi.reshape(bb, tl, half)
    h2_lo = h1_ref[:, :, half:] + gt_ref[:, :, half:] * s_lo.reshape(bb, tl, half)
    ms = (jnp.sum(h2_hi * h2_hi, axis=-1, keepdims=True)
          + jnp.sum(h2_lo * h2_lo, axis=-1, keepdims=True)) / d
    inv = lax.rsqrt(ms + EPS)
    y_ref[:, :, :half] = h2_hi * inv * g_ref[:, :, :half]
    y_ref[:, :, half:] = h2_lo * inv * g_ref[:, :, half:]


def _combine(dest, wts, shared, h1, mod, g_final, eo, bb, tl):
    nb, l, d = h1.shape
    tb = bb * tl
    nj = l // tl
    nsteps = (nb // bb) * nj
    lin = lambda i, j: i * nj + j
    return pl.pallas_call(
        _combine_kernel,
        grid=(nb // bb, nj),
        in_specs=[pl.BlockSpec((tb * DEST_W,), lambda i, j: (lin(i, j),), memory_space=pltpu.SMEM),
                  pl.BlockSpec((tb * DEST_W,), lambda i, j: (jnp.minimum(lin(i, j) + 1, nsteps - 1),),
                               memory_space=pltpu.SMEM),
                  pl.BlockSpec((tb, LANES), lambda i, j: (lin(i, j), 0)),
                  pl.BlockSpec((tb, d // 2), lambda i, j: (lin(i, j), 0)),
                  pl.BlockSpec((bb, tl, d), lambda i, j: (i, j, 0)),
                  pl.BlockSpec((bb, 1, d), lambda i, j: (i, 0, 5)),
                  pl.BlockSpec((1, 1, d), lambda i, j: (0, 0, 0)),
                  pl.BlockSpec(memory_space=pl.ANY)],
        out_specs=pl.BlockSpec((bb, tl, d), lambda i, j: (i, j, 0)),
        out_shape=jax.ShapeDtypeStruct((nb, l, d), F32),
        scratch_shapes=[pltpu.VMEM((2, TOP_K, tb, d // 2), I32), pltpu.SemaphoreType.DMA((2,))],
        compiler_params=_params(("arbitrary", "arbitrary"), VMEM_LIMIT),
        name="moe_combine",
    )(dest, dest, wts, shared, h1, mod, g_final.reshape(1, 1, d), eo)


def _prep_w_in(w_in):
    d = w_in.shape[0]
    qr_w, vr_w, qf_w = H_RET * DK_RET, H_RET * DV_RET, H_FOX * DH_FOX
    sizes = (qr_w, qr_w, vr_w, vr_w, qf_w, qf_w, qf_w, H_FOX, d, d)
    offs = [0]
    for s in sizes:
        offs.append(offs[-1] + s)
    part = lambda n: w_in[:, offs[n]:offs[n + 1]]
    w_qk = jnp.concatenate([part(0), part(1) * (DK_RET ** -0.5)], axis=1).astype(BF16)
    w_b = jnp.concatenate([part(2), part(3), part(4) * (DH_FOX ** -0.5 * LOG2E), part(8), part(9)], axis=1).astype(BF16)
    w_kf = part(5).astype(BF16)
    w_vf = part(6).astype(BF16)
    w_fl = jnp.pad(part(7), ((0, 0), (0, LANES - H_FOX))).astype(BF16)
    return w_qk, w_b, w_kf, w_vf, w_fl


def _trunk_front(x, mod, lw, bb, tl, tm, fox, s0, pos0, ret_tb):
    nb, l, d = x.shape
    t = nb * l
    hn, logf = _norm1(x, lw["g_norm1"], mod, lw["w_fl"], lw["b_fl"], bb, tl)
    hn2d = hn.reshape(t, d)
    zqk = _matmul(hn2d, lw["w_qk"], F32, tm, 512, "proj_qk_ret")
    kf, kb = _matmul_heads(hn2d, lw["w_kf"], nb, l, bb, tl, "proj_k_fox")
    vf, vb = _matmul_heads(hn2d, lw["w_vf"], nb, l, bb, tl, "proj_v_fox")
    zb = _matmul(hn2d, lw["w_b"], BF16, tm, 512, "proj_rest")
    zb3 = zb.reshape(nb, l, -1)
    o_r, s_new = _retention(zqk.reshape(nb, l, -1), zb3, s0, lw["g_ret_gn"], pos0, ret_tb)
    o_f = fox(zb3, kb.reshape(nb, l, -1), vb.reshape(nb, l, -1), logf)
    m = _mix1(o_r.reshape(t, d), o_f.reshape(t, d), lw["w_ret_out"], lw["w_fox_out"], zb, tm, 512)
    return m.reshape(nb, l, d), (kf, vf, logf, s_new)


def kernel(x_prompt, x_sample, c_prompt, c_sample, cache_fox_k, cache_fox_v, cache_fox_logf, state_ret, w_ada, b_ada, g_norm1, w_in, b_forget, g_ret_gn, w_ret_out, w_fox_out, w_out, g_norm2, w_router, b_router, w_exp_gate, w_exp_up, w_exp_down, w_sh_gate, w_sh_up, w_sh_down, g_final):
    depth = w_in.shape[0]
    assert depth == 1
    nbp, s, d = x_prompt.shape
    nbs, ls, _ = x_sample.shape
    p_len = cache_fox_k.shape[2]
    ne = w_router.shape[-1]
    tp, ts = nbp * s, nbs * ls

    nb_all = nbp + nbs
    nb_pad = -(-nb_all // 8) * 8
    c_all = jnp.pad(jnp.concatenate([c_prompt, c_sample], axis=0), ((0, nb_pad - nb_all), (0, 0)))
    mod_all = _adaln(c_all, w_ada[0], b_ada[0])
    mod_p = mod_all[:nbp].reshape(nbp, 1, 6 * d)
    mod_s = mod_all[nbp:nb_all].reshape(nbs, 1, 6 * d)

    w_qk, w_b, w_kf, w_vf, w_fl = _prep_w_in(w_in[0])
    lw = dict(g_norm1=g_norm1[0], w_fl=w_fl,
              b_fl=jnp.pad(b_forget[0], (0, LANES - H_FOX)).reshape(1, LANES),
              w_qk=w_qk, w_b=w_b, w_kf=w_kf, w_vf=w_vf, g_ret_gn=g_ret_gn[0],
              w_ret_out=w_ret_out[0].astype(BF16), w_fox_out=w_fox_out[0].astype(BF16))
    w_out_b = w_out[0].astype(BF16)

    tl_p = min(s, 512)
    tm_p = min(tp, 1024)
    bb_s = max(1, min(nbs, 512 // ls))
    tm_s = min(ts, 1024)

    def fox_p(zb3, kb3, vb3, logf):
        f_rows = _cumsum_rows(logf.swapaxes(1, 2), min(s, 512), LOG2E)
        return _fox_prompt(zb3, kb3, vb3, f_rows, f_rows.swapaxes(1, 2), min(s, 1024))

    def fox_s(zb3, kb3, vb3, logf):
        full = jnp.concatenate([cache_fox_logf[0].astype(F32), logf], axis=1)
        sp = -(-(p_len + ls) // LANES) * LANES
        full = jnp.pad(full, ((0, 0), (0, sp - p_len - ls), (0, 0)))
        f_rows = _cumsum_rows(full.swapaxes(1, 2), sp, LOG2E)
        return _fox_sample(zb3, kb3, vb3, cache_fox_k, cache_fox_v, f_rows, f_rows.swapaxes(1, 2))

    s0_p = jnp.zeros((nbp, H_RET, DK_RET, DV_RET), F32)
    m_p, st_p = _trunk_front(x_prompt, mod_p, lw, 1, tl_p, tm_p, fox_p, s0_p, 0, min(s, 256))
    m_s, st_s = _trunk_front(x_sample, mod_s, lw, bb_s, ls, tm_s, fox_s, state_ret[0], p_len, ls)

    cnt0 = jnp.zeros((ne, LANES), F32)
    tl2_p = min(s, 512)
    bb2_s = max(1, min(nbs, 512 // ls))
    h1_p, hn_p, e_p, r_p, wt_p, cnt_p = _mix2(m_p, w_out_b, x_prompt, mod_p, g_norm2[0], w_router[0],
                                              b_router[0], cnt0, 1, tl2_p)
    h1_s, hn_s, e_s, r_s, wt_s, cnt_s = _mix2(m_s, w_out_b, x_sample, mod_s, g_norm2[0], w_router[0],
                                              b_router[0], cnt_p, bb2_s, ls)

    counts = cnt_s[:, 0].astype(I32)
    rblk = EXPERT_ROWS
    padded = (counts + rblk - 1) // rblk * rblk
    pad_end = jnp.cumsum(padded)
    pad_start = pad_end - padded
    t_all = tp + ts
    nblk = (t_all * TOP_K + ne * (rblk - 1) + rblk - 1) // rblk
    rows = nblk * rblk
    blk_start = jnp.arange(nblk, dtype=I32) * rblk
    block_expert = jnp.minimum(jnp.searchsorted(pad_end, blk_start, side="right"), ne - 1).astype(I32)
    n_used = (pad_end[-1] // rblk).astype(I32).reshape(1)
    valid_rows = jnp.clip(pad_start[block_expert] + counts[block_expert] - blk_start, 0, rblk)
    valid_rows = jnp.where(blk_start < pad_end[-1], valid_rows, 0)
    n_valid = ((valid_rows + SUB_ROWS - 1) // SUB_ROWS).astype(I32)
    zero_start = (pad_start + counts).astype(I32)
    zero_n = ((-counts) % SUB_ROWS).astype(I32)
    e_all = jnp.concatenate([e_p, e_s], axis=1)
    r_all = jnp.concatenate([r_p, r_s], axis=1)
    start_of = jnp.sum(jnp.where(e_all[..., None] == jnp.arange(ne, dtype=I32), pad_start.astype(I32), 0), axis=-1)
    dest = (start_of + r_all).astype(I32).T.reshape(-1)
    tok_major = lambda w: jnp.pad(w.T, ((0, 0), (0, LANES - DEST_W)))

    buf = _dispatch(hn_p, hn_s, dest, zero_start, zero_n, rows, min(tp, ts, 256))
    eo = _expert_ffn(buf, w_exp_gate[0], w_exp_up[0], w_exp_down[0], block_expert, n_valid, n_used, rblk)

    def shared_ffn(hn):
        rb = min(hn.shape[0], rblk)
        nb_ = hn.shape[0] // rb
        return _expert_ffn(hn, w_sh_gate, w_sh_up, w_sh_down, jnp.zeros((nb_,), I32),
                           jnp.full((nb_,), rb // SUB_ROWS, I32), jnp.full((1,), nb_, I32), rb)

    tlc_p = min(s, 128)
    bbc_s = max(1, min(nbs, 128 // ls))
    y_p = _combine(dest[:tp * DEST_W], tok_major(wt_p), shared_ffn(hn_p), h1_p, mod_p, g_final, eo, 1, tlc_p)
    y_s = _combine(dest[tp * DEST_W:], tok_major(wt_s), shared_ffn(hn_s), h1_s, mod_s, g_final, eo, bbc_s, ls)

    kf_p, vf_p, logf_p, s_p = st_p
    kf_s, vf_s, logf_s, s_s = st_s
    return (y_p, y_s, kf_p[None], vf_p[None], logf_p[None], s_p[None],
            kf_s[None], vf_s[None], logf_s[None], s_s[None])
```
